```python
import math
import jax, jax.numpy as jnp
from jax import lax
import numpy as np

D_MODEL = 1024
BATCH = 8
SEQ = 2048
DEPTH = 2

CTX_LEN = 256
GRID_W = 64
D_MIX = D_MODEL
NA_HEAD_DIM = 64
NA_WIDTH = 3 * D_MIX // 8
NA_HEADS = NA_WIDTH // NA_HEAD_DIM
WIN_ROWS = 8
WIN_COLS = 16
MLA_V_DIM = 64
MLA_WIDTH = 3 * D_MIX // 8
MLA_HEADS = MLA_WIDTH // MLA_V_DIM
MLA_NOPE = 64
MLA_ROPE = 32
MLA_QK = MLA_NOPE + MLA_ROPE
Q_LORA = 3 * D_MODEL // 8
KV_LORA = D_MODEL // 4
S5_WIDTH = D_MIX - NA_WIDTH - MLA_WIDTH
S5_GROUP = 16
S5_GROUPS = S5_WIDTH // S5_GROUP
S5_STATE = 64
DT_MIN = 0.001
DT_MAX = 0.1
D_FF = 2816
ROPE_THETA = 10000.0
EPS = 1e-6
Q_BLOCK = 128
N_MOD = 9
NEG_INF = -1e30
IN_SIZES = (NA_WIDTH, NA_WIDTH, NA_WIDTH, Q_LORA, KV_LORA, MLA_ROPE, S5_WIDTH)
IN_COLS = 3 * NA_WIDTH + Q_LORA + KV_LORA + MLA_ROPE + S5_WIDTH

kernel_name = "hybrid_na_mla_s5_prefix_dit"


def rms_norm(x, g):
    x32 = x.astype(jnp.float32)
    y = x32 * lax.rsqrt(jnp.mean(x32 * x32, axis=-1, keepdims=True) + EPS)
    return (y * g.astype(jnp.float32)).astype(x.dtype)


def modulate(x, g, shift, scale):
    return rms_norm(x, g) * (1 + scale) + shift


def swiglu(x, w_gu, w_down):
    gate, up = jnp.split(x @ w_gu, 2, axis=-1)
    return (jax.nn.silu(gate) * up) @ w_down


def split_cols(z):
    parts, off = [], 0
    for s in IN_SIZES:
        parts.append(z[..., off:off + s])
        off += s
    return parts


def attend(q, k, v, scale):
    s = jnp.einsum('bqhd,bkhd->bhqk', q, k).astype(jnp.float32) * scale
    p = jax.nn.softmax(s, axis=-1).astype(v.dtype)
    return jnp.einsum('bhqk,bkhd->bqhd', p, v)


def attend_joint(q, k1, v1, k2, v2, scale):
    s = jnp.concatenate([jnp.einsum('bqhd,bkhd->bhqk', q, k1),
                         jnp.einsum('bqhd,bkhd->bhqk', q, k2)], axis=-1).astype(jnp.float32) * scale
    p = jax.nn.softmax(s, axis=-1).astype(v1.dtype)
    n1 = k1.shape[1]
    return (jnp.einsum('bhqk,bkhd->bqhd', p[..., :n1], v1)
            + jnp.einsum('bhqk,bkhd->bqhd', p[..., n1:], v2))


def blocked_joint_attention(q, k, v, kc, vc, scale):
    B, N, H, d = q.shape
    nb = N // Q_BLOCK
    qb = q.reshape(B, nb, Q_BLOCK, H, d).transpose(1, 0, 2, 3, 4)
    ob = lax.map(lambda qi: attend_joint(qi, k, v, kc, vc, scale), qb)
    return ob.transpose(1, 0, 2, 3, 4).reshape(B, N, H * v.shape[-1])


def rope_2d(x, row, col):
    half = x.shape[-1] // 2
    quarter = half // 2
    inv = ROPE_THETA ** (-jnp.arange(quarter, dtype=jnp.float32) / quarter)

    def rot(y, pos):
        ang = pos[:, None] * inv
        cos = jnp.cos(ang)[:, None, :].astype(y.dtype)
        sin = jnp.sin(ang)[:, None, :].astype(y.dtype)
        y1, y2 = y[..., :quarter], y[..., quarter:]
        return jnp.concatenate([y1 * cos - y2 * sin, y2 * cos + y1 * sin], axis=-1)

    return jnp.concatenate([rot(x[..., :half], row), rot(x[..., half:], col)], axis=-1)


def rope_tail(x, row, col):
    return jnp.concatenate([x[..., :MLA_NOPE], rope_2d(x[..., MLA_NOPE:], row, col)], axis=-1)


def na_attention(q, k, v, kc, vc, rpb):
    B, N, H, hd = q.shape
    rows = N // GRID_W
    kh = min(WIN_ROWS, rows)
    r = jnp.arange(rows)
    row_start = jnp.clip(r - kh // 2, 0, rows - kh)
    key_rows = row_start[:, None] + jnp.arange(kh)[None, :]
    qg = q.reshape(B, rows, GRID_W, H, hd)
    kg = k.reshape(B, rows, GRID_W, H, hd)[:, key_rows].reshape(B, rows, kh * GRID_W, H, hd)
    vg = v.reshape(B, rows, GRID_W, H, hd)[:, key_rows].reshape(B, rows, kh * GRID_W, H, hd)
    cq = jnp.arange(GRID_W)
    col_start = jnp.clip(cq - WIN_COLS // 2, 0, GRID_W - WIN_COLS)
    kcol = jnp.tile(jnp.arange(GRID_W), kh)
    in_win = (kcol[None, :] >= col_start[:, None]) & (kcol[None, :] < col_start[:, None] + WIN_COLS)
    d_row = jnp.repeat(key_rows - r[:, None], GRID_W, axis=1)
    d_col = jnp.clip(kcol[None, :] - cq[:, None], -(WIN_COLS - 1), WIN_COLS - 1)
    bias = rpb[:, d_row[:, None, :] + WIN_ROWS - 1, d_col[None, :, :] + WIN_COLS - 1]
    scale = hd ** -0.5
    s_win = jnp.einsum('brqhd,brkhd->bhrqk', qg, kg).astype(jnp.float32) * scale + bias[None].astype(jnp.float32)
    s_win = jnp.where(in_win[None, None, None], s_win, NEG_INF)
    s_ctx = jnp.einsum('brqhd,bkhd->bhrqk', qg, kc).astype(jnp.float32) * scale
    p = jax.nn.softmax(jnp.concatenate([s_win, s_ctx], axis=-1), axis=-1).astype(v.dtype)
    kw = kh * GRID_W
    out = (jnp.einsum('bhrqk,brkhd->brqhd', p[..., :kw], vg)
           + jnp.einsum('bhrqk,bkhd->brqhd', p[..., kw:], vc))
    return out.reshape(B, N, H * hd)


def mla_queries(cq, g_cq, w_uq, g_q):
    B, N, _ = cq.shape
    q = (rms_norm(cq, g_cq) @ w_uq).reshape(B, N, MLA_HEADS, MLA_QK)
    return rms_norm(q, g_q)


def mla_keys_values(ckv, kr, g_ckv, w_ukv, g_k):
    B, N, _ = ckv.shape
    kv = (rms_norm(ckv, g_ckv) @ w_ukv).reshape(B, N, MLA_HEADS, MLA_NOPE + MLA_V_DIM)
    k_nope, v = kv[..., :MLA_NOPE], kv[..., MLA_NOPE:]
    k_rope = jnp.broadcast_to(kr[:, :, None, :], (B, N, MLA_HEADS, MLA_ROPE))
    k = rms_norm(jnp.concatenate([k_nope, k_rope], axis=-1), g_k)
    return k, v


def s5_discretize(lam_re, lam_im, log_dt, b_re, b_im):
    lam = lam_re.astype(jnp.float32) + 1j * lam_im.astype(jnp.float32)
    dt = jnp.exp(log_dt.astype(jnp.float32))[:, None]
    lam_bar = jnp.exp(lam * dt)
    b = b_re.astype(jnp.float32) + 1j * b_im.astype(jnp.float32)
    b_bar = ((lam_bar - 1) / lam)[..., None] * b
    return lam_bar, b_bar


def _scan_op(e1, e2):
    a1, b1 = e1
    a2, b2 = e2
    return a1 * a2, a2 * b1 + b2


def linear_scan(lam_bar, bu, h0, reverse):
    if h0 is not None:
        first = -1 if reverse else 0
        bu = bu.at[:, first].add(lam_bar * h0)
    a = jnp.broadcast_to(lam_bar, bu.shape)
    _, h = lax.associative_scan(_scan_op, (a, bu), axis=1, reverse=reverse)
    return h


def s5_mixer(ux, uc, need_ctx_out, lam_re, lam_im, log_dt, b_re, b_im, c_re, c_im, d_skip, w_glu, b_glu):
    B, N, _ = ux.shape
    L = uc.shape[1]
    uxg = ux.reshape(B, N, S5_GROUPS, S5_GROUP).astype(jnp.complex64)
    ucg = uc.reshape(B, L, S5_GROUPS, S5_GROUP).astype(jnp.complex64)
    ys_x, ys_c = [], []
    for d in range(2):
        reverse = d == 1
        lam_bar, b_bar = s5_discretize(lam_re[d], lam_im[d], log_dt[d], b_re[d], b_im[d])
        c_mat = c_re[d].astype(jnp.float32) + 1j * c_im[d].astype(jnp.float32)
        h_c = linear_scan(lam_bar, jnp.einsum('blgm,gpm->blgp', ucg, b_bar), None, reverse)
        h_last = h_c[:, 0] if reverse else h_c[:, -1]
        h_x = linear_scan(lam_bar, jnp.einsum('bngm,gpm->bngp', uxg, b_bar), h_last, reverse)
        ys_x.append(jnp.einsum('bngp,gmp->bngm', h_x, c_mat).real)
        if need_ctx_out:
            ys_c.append(jnp.einsum('blgp,gmp->blgm', h_c, c_mat).real)

    def readout(ys, u):
        y = (ys[0] + ys[1]).reshape(u.shape).astype(u.dtype) + d_skip * u
        a, g = jnp.split(jax.nn.gelu(y) @ w_glu + b_glu, 2, axis=-1)
        return a * jax.nn.sigmoid(g)

    ox = readout(ys_x, ux)
    oc = readout(ys_c, uc) if need_ctx_out else None
    return ox, oc


def token_mixers(zx, zc, need_ctx_out, row, col, na_qk_g, na_rpb, mla_cq_g, mla_ckv_g, mla_w_uq,
                 mla_w_ukv, mla_qk_g, s5_lam_re, s5_lam_im, s5_log_dt, s5_b_re, s5_b_im, s5_c_re,
                 s5_c_im, s5_d, s5_w_glu, s5_b_glu):
    B, N, _ = zx.shape
    L = zc.shape[1]
    qa_x, ka_x, va_x, cq_x, ckv_x, kr_x, u_x = split_cols(zx)
    qa_c, ka_c, va_c, cq_c, ckv_c, kr_c, u_c = split_cols(zc)

    def heads(t):
        return t.reshape(t.shape[0], t.shape[1], NA_HEADS, NA_HEAD_DIM)

    ka_ch = rms_norm(heads(ka_c), na_qk_g[1])
    va_ch = heads(va_c)
    a_x = na_attention(rms_norm(heads(qa_x), na_qk_g[0]), rms_norm(heads(ka_x), na_qk_g[1]),
                       heads(va_x), ka_ch, va_ch, na_rpb)
    kb_c, vb_c = mla_keys_values(ckv_c, kr_c, mla_ckv_g, mla_w_ukv, mla_qk_g[1])
    kb_x, vb_x = mla_keys_values(ckv_x, kr_x, mla_ckv_g, mla_w_ukv, mla_qk_g[1])
    qb_x = rope_tail(mla_queries(cq_x, mla_cq_g, mla_w_uq, mla_qk_g[0]), row, col)
    kb_x = rope_tail(kb_x, row, col)
    b_x = blocked_joint_attention(qb_x, kb_x, vb_x, kb_c, vb_c, MLA_QK ** -0.5)
    s_x, s_c = s5_mixer(u_x, u_c, need_ctx_out, s5_lam_re, s5_lam_im, s5_log_dt, s5_b_re, s5_b_im,
                        s5_c_re, s5_c_im, s5_d, s5_w_glu, s5_b_glu)
    ox = jnp.concatenate([a_x, b_x, s_x], axis=-1)
    if not need_ctx_out:
        return ox, None
    a_c = attend(rms_norm(heads(qa_c), na_qk_g[0]), ka_ch, va_ch, NA_HEAD_DIM ** -0.5).reshape(B, L, NA_WIDTH)
    qb_c = mla_queries(cq_c, mla_cq_g, mla_w_uq, mla_qk_g[0])
    b_c = attend(qb_c, kb_c, vb_c, MLA_QK ** -0.5).reshape(B, L, MLA_WIDTH)
    oc = jnp.concatenate([a_c, b_c, s_c], axis=-1)
    return ox, oc


def setup_inputs(seed: int = 0) -> dict:
    key = jax.random.key(seed)
    keys = jax.random.split(key, 32)
    ctr = [0]
    f32 = jnp.float32

    def nk():
        ctr[0] += 1
        return keys[ctr[0] - 1]

    def nrm(shape, std):
        return std * jax.random.normal(nk(), shape, f32)

    def gain(shape):
        return 1.0 + 0.02 * jax.random.normal(nk(), shape, f32)

    G, P, M = S5_GROUPS, S5_STATE, S5_GROUP
    lam_im_base = jnp.broadcast_to(math.pi * jnp.arange(P, dtype=f32), (DEPTH, 2, G, P))
    return {
        "x": nrm((BATCH, SEQ, D_MODEL), 1.0),
        "c": nrm((BATCH, D_MODEL), 1.0),
        "ctx": nrm((BATCH, CTX_LEN, D_MODEL), 1.0),
        "c_ctx": nrm((D_MODEL,), 1.0),
        "w_mod": nrm((DEPTH, D_MODEL, N_MOD * D_MODEL), 0.2 * D_MODEL ** -0.5),
        "b_mod": nrm((DEPTH, N_MOD * D_MODEL), 0.02),
        "norm_g": gain((DEPTH, 3, D_MODEL)),
        "ffn_w_gu": nrm((DEPTH, 2, D_MODEL, 2 * D_FF), D_MODEL ** -0.5),
        "ffn_w_down": nrm((DEPTH, 2, D_FF, D_MODEL), D_FF ** -0.5),
        "w_in": nrm((DEPTH, D_MODEL, IN_COLS), D_MODEL ** -0.5),
        "w_out": nrm((DEPTH, D_MIX, D_MODEL), D_MIX ** -0.5),
        "na_qk_g": gain((DEPTH, 2, NA_HEAD_DIM)),
        "na_rpb": nrm((DEPTH, NA_HEADS, 2 * WIN_ROWS - 1, 2 * WIN_COLS - 1), 0.02),
        "mla_cq_g": gain((DEPTH, Q_LORA)),
        "mla_ckv_g": gain((DEPTH, KV_LORA)),
        "mla_w_uq": nrm((DEPTH, Q_LORA, MLA_HEADS * MLA_QK), Q_LORA ** -0.5),
        "mla_w_ukv": nrm((DEPTH, KV_LORA, MLA_HEADS * (MLA_NOPE + MLA_V_DIM)), KV_LORA ** -0.5),
        "mla_qk_g": gain((DEPTH, 2, MLA_QK)),
        "s5_lam_re": -0.5 + nrm((DEPTH, 2, G, P), 0.01),
        "s5_lam_im": lam_im_base + nrm((DEPTH, 2, G, P), 0.01),
        "s5_log_dt": jax.random.uniform(nk(), (DEPTH, 2, G), f32, math.log(DT_MIN), math.log(DT_MAX)),
        "s5_b_re": nrm((DEPTH, 2, G, P, M), (2 * M) ** -0.5),
        "s5_b_im": nrm((DEPTH, 2, G, P, M), (2 * M) ** -0.5),
        "s5_c_re": nrm((DEPTH, 2, G, M, P), P ** -0.5),
        "s5_c_im": nrm((DEPTH, 2, G, M, P), P ** -0.5),
        "s5_d": nrm((DEPTH, S5_WIDTH), 1.0),
        "s5_w_glu": nrm((DEPTH, S5_WIDTH, 2 * S5_WIDTH), S5_WIDTH ** -0.5),
        "s5_b_glu": nrm((DEPTH, 2 * S5_WIDTH), 0.02),
    }


def reference(x, c, ctx, c_ctx, w_mod, b_mod, norm_g, ffn_w_gu, ffn_w_down, w_in, w_out, na_qk_g, na_rpb,
              mla_cq_g, mla_ckv_g, mla_w_uq, mla_w_ukv, mla_qk_g, s5_lam_re, s5_lam_im, s5_log_dt,
              s5_b_re, s5_b_im, s5_c_re, s5_c_im, s5_d, s5_w_glu, s5_b_glu):
    B, N, _ = x.shape
    t = jnp.arange(N)
    row = (t // GRID_W).astype(jnp.float32)
    col = (t % GRID_W).astype(jnp.float32)
    hx, hc = x, ctx
    for l in range(DEPTH):
        need_ctx_out = l < DEPTH - 1
        mx = (jax.nn.silu(c) @ w_mod[l] + b_mod[l]).reshape(B, N_MOD, 1, D_MODEL)
        mc = (jax.nn.silu(c_ctx) @ w_mod[l] + b_mod[l]).reshape(N_MOD, D_MODEL)
        hx = hx + 0.5 * mx[:, 2] * swiglu(modulate(hx, norm_g[l, 0], mx[:, 0], mx[:, 1]), ffn_w_gu[l, 0], ffn_w_down[l, 0])
        hc = hc + 0.5 * mc[2] * swiglu(modulate(hc, norm_g[l, 0], mc[0], mc[1]), ffn_w_gu[l, 0], ffn_w_down[l, 0])
        zx = modulate(hx, norm_g[l, 1], mx[:, 3], mx[:, 4]) @ w_in[l]
        zc = modulate(hc, norm_g[l, 1], mc[3], mc[4]) @ w_in[l]
        ox, oc = token_mixers(zx, zc, need_ctx_out, row, col, na_qk_g[l], na_rpb[l], mla_cq_g[l],
                              mla_ckv_g[l], mla_w_uq[l], mla_w_ukv[l], mla_qk_g[l], s5_lam_re[l],
                              s5_lam_im[l], s5_log_dt[l], s5_b_re[l], s5_b_im[l], s5_c_re[l], s5_c_im[l],
                              s5_d[l], s5_w_glu[l], s5_b_glu[l])
        hx = hx + mx[:, 5] * (ox @ w_out[l])
        hx = hx + 0.5 * mx[:, 8] * swiglu(modulate(hx, norm_g[l, 2], mx[:, 6], mx[:, 7]), ffn_w_gu[l, 1], ffn_w_down[l, 1])
        if need_ctx_out:
            hc = hc + mc[5] * (oc @ w_out[l])
            hc = hc + 0.5 * mc[8] * swiglu(modulate(hc, norm_g[l, 2], mc[6], mc[7]), ffn_w_gu[l, 1], ffn_w_down[l, 1])
    return hx
```

```python
import functools
import math

import jax
import jax.numpy as jnp
from jax import lax
from jax.experimental import pallas as pl
from jax.experimental.pallas import tpu as pltpu

D_MODEL = 1024
BATCH = 8
SEQ = 2048
DEPTH = 2
CTX_LEN = 256
S_ALL = CTX_LEN + SEQ
GRID_W = 64
GRID_ROWS = SEQ // GRID_W
NA_HEAD_DIM = 64
NA_WIDTH = 384
NA_HEADS = 6
WIN_ROWS = 8
WIN_COLS = 16
MLA_V_DIM = 64
MLA_WIDTH = 384
MLA_HEADS = 6
MLA_NOPE = 64
MLA_ROPE = 32
MLA_QK = 96
Q_LORA = 384
KV_LORA = 256
S5_WIDTH = 256
S5_GROUP = 16
S5_GROUPS = 16
S5_STATE = 64
S5_LANES = S5_GROUPS * S5_STATE
D_FF = 2816
ROPE_THETA = 10000.0
EPS = 1e-6
N_MOD = 9
NEG_INF = -1e30

LANES = 128
TM = 256
N_TILES = S_ALL // TM
FF_CHUNK = 256
N_FF_CHUNKS = D_FF // FF_CHUNK
HEAD_PAD = 128
S5_T = 128
S5_ROWS = S5_T * BATCH
S5_CHUNKS = S_ALL // S5_T
S5_CTX_CHUNKS = CTX_LEN // S5_T
VMEM_LIMIT = 56 * 1024 * 1024

ZC_QA, ZC_KA, ZC_VA, ZC_CQ, ZC_CKV, ZC_U, ZC_KR = 0, 384, 768, 1152, 1536, 1792, 2048
Z_COLS = 2176

f32 = jnp.float32
bf16 = jnp.bfloat16


def _dot(a, b):
    return jnp.dot(a, b, preferred_element_type=f32)


def _dot_nt(a, b):
    return lax.dot_general(a, b, (((1,), (1,)), ((), ())), preferred_element_type=f32)


def _sigmoid(x):
    return 1.0 / (1.0 + jnp.exp(-x))


def _rms(x, n):
    return lax.rsqrt(jnp.sum(x * x, axis=-1, keepdims=True) / n + EPS)


def _modulated(x, g, shift, scale):
    y = x * _rms(x, D_MODEL)
    return (y * g) * (1.0 + scale) + shift


def _mod_kernel(c_ref, w_ref, b_ref, o_ref):
    c = c_ref[...]
    a = (c * _sigmoid(c)).astype(bf16)
    o_ref[0] = _dot(a, w_ref[0].astype(bf16)) + b_ref[0]


def _mod_call(cvec, w_mod, b_mod):
    rows = cvec.shape[0]
    return pl.pallas_call(
        _mod_kernel,
        grid=(DEPTH, N_MOD),
        in_specs=[
            pl.BlockSpec((rows, D_MODEL), lambda l, j: (0, 0)),
            pl.BlockSpec((1, D_MODEL, D_MODEL), lambda l, j: (l, 0, j)),
            pl.BlockSpec((1, 1, D_MODEL), lambda l, j: (l, 0, j)),
        ],
        out_specs=pl.BlockSpec((1, rows, D_MODEL), lambda l, j: (l, 0, j)),
        out_shape=jax.ShapeDtypeStruct((DEPTH, rows, N_MOD * D_MODEL), f32),
        compiler_params=pltpu.CompilerParams(
            dimension_semantics=("arbitrary", "arbitrary"), vmem_limit_bytes=VMEM_LIMIT),
        name="adaln_mod",
    )(cvec, w_mod, b_mod.reshape(DEPTH, 1, N_MOD * D_MODEL))


def _mod_spec():
    return pl.BlockSpec((1, 1, N_MOD, D_MODEL), lambda b, j: (b, jnp.minimum(j, 1), 0, 0))


def _tok_spec(width):
    return pl.BlockSpec((1, TM, width), lambda b, j: (b, j, 0))


def _full_spec(shape):
    zeros = (0,) * len(shape)
    return pl.BlockSpec(shape, lambda b, j: zeros)


def _ffn_kernel(h_ref, mod_ref, g_ref, wgu_ref, wd_ref, o_ref, xm_ref, acc_ref, *, i0):
    x = h_ref[0]
    shift = mod_ref[0, 0, i0:i0 + 1, :]
    scale = mod_ref[0, 0, i0 + 1:i0 + 2, :]
    gate = mod_ref[0, 0, i0 + 2:i0 + 3, :]
    xm_ref[...] = _modulated(x, g_ref[...], shift, scale).astype(bf16)
    acc_ref[...] = jnp.zeros_like(acc_ref)

    def body(c, carry):
        gu = _dot(xm_ref[...], wgu_ref[c])
        gt = gu[:, :FF_CHUNK]
        up = gu[:, FF_CHUNK:]
        a = ((gt * _sigmoid(gt)) * up).astype(bf16)
        acc_ref[...] += _dot(a, wd_ref[c])
        return carry

    lax.fori_loop(0, N_FF_CHUNKS, body, 0)
    o_ref[0] = x + (0.5 * gate) * acc_ref[...]


def _ffn_call(h, mod, g, wgu, wd, i0):
    return pl.pallas_call(
        functools.partial(_ffn_kernel, i0=i0),
        grid=(BATCH, N_TILES),
        in_specs=[
            _tok_spec(D_MODEL),
            _mod_spec(),
            _full_spec((1, D_MODEL)),
            _full_spec((N_FF_CHUNKS, D_MODEL, 2 * FF_CHUNK)),
            _full_spec((N_FF_CHUNKS, FF_CHUNK, D_MODEL)),
        ],
        out_specs=_tok_spec(D_MODEL),
        out_shape=jax.ShapeDtypeStruct((BATCH, S_ALL, D_MODEL), f32),
        scratch_shapes=[pltpu.VMEM((TM, D_MODEL), bf16), pltpu.VMEM((TM, D_MODEL), f32)],
        compiler_params=pltpu.CompilerParams(
            dimension_semantics=("arbitrary", "arbitrary"), vmem_limit_bytes=VMEM_LIMIT),
        name="swiglu_half_step",
    )(h, mod, g, wgu, wd)


def _pair_head_norm(x, g):
    lane = lax.broadcasted_iota(jnp.int32, (1, LANES), 1)
    lo = lane < NA_HEAD_DIM
    outs = []
    for p in range(NA_WIDTH // LANES):
        xb = x[:, p * LANES:(p + 1) * LANES]
        sq = xb * xb
        s_lo = jnp.sum(jnp.where(lo, sq, 0.0), axis=-1, keepdims=True)
        s_hi = jnp.sum(jnp.where(lo, 0.0, sq), axis=-1, keepdims=True)
        r = jnp.where(lo, lax.rsqrt(s_lo / NA_HEAD_DIM + EPS), lax.rsqrt(s_hi / NA_HEAD_DIM + EPS))
        outs.append((xb * r) * g[:, p * LANES:(p + 1) * LANES])
    return outs


def _inproj_kernel(h_ref, mod_ref, g_ref, win_ref, naq_g_ref, nak_g_ref, cq_g_ref, ckv_g_ref,
                   wq_ref, wkv_ref, gq_ref, gqp_ref, gk_ref, gkp_ref, cos_ref, sin_ref,
                   qn_ref, kn_ref, vn_ref, qm_ref, km_ref, vm_ref, u_ref):
    x = h_ref[0]
    shift = mod_ref[0, 0, 3:4, :]
    scale = mod_ref[0, 0, 4:5, :]
    xm = _modulated(x, g_ref[...], shift, scale).astype(bf16)
    z = _dot(xm, win_ref[...])

    qn = _pair_head_norm(z[:, ZC_QA:ZC_QA + NA_WIDTH], naq_g_ref[...])
    kn = _pair_head_norm(z[:, ZC_KA:ZC_KA + NA_WIDTH], nak_g_ref[...])
    for p in range(NA_WIDTH // LANES):
        qn_ref[0, :, p * LANES:(p + 1) * LANES] = qn[p].astype(bf16)
        kn_ref[0, :, p * LANES:(p + 1) * LANES] = kn[p].astype(bf16)
    vn_ref[0] = z[:, ZC_VA:ZC_VA + NA_WIDTH].astype(bf16)
    u_ref[0] = z[:, ZC_U:ZC_U + S5_WIDTH]

    cos_t = cos_ref[...]
    sin_t = sin_ref[...]

    cq = z[:, ZC_CQ:ZC_CQ + Q_LORA]
    ncq = ((cq * _rms(cq, Q_LORA)) * cq_g_ref[...]).astype(bf16)
    qq = _dot(ncq, wq_ref[...])
    gq = gq_ref[...]
    gqp = gqp_ref[...]
    for hd in range(MLA_HEADS):
        pre = qq[:, hd * HEAD_PAD:(hd + 1) * HEAD_PAD]
        perm = qq[:, (MLA_HEADS + hd) * HEAD_PAD:(MLA_HEADS + hd + 1) * HEAD_PAD]
        r = _rms(pre, MLA_QK)
        qm_ref[0, :, hd * HEAD_PAD:(hd + 1) * HEAD_PAD] = (
            ((pre * gq) * cos_t + (perm * gqp) * sin_t) * r).astype(bf16)

    ckv = z[:, ZC_CKV:ZC_CKV + KV_LORA]
    nkv = ((ckv * _rms(ckv, KV_LORA)) * ckv_g_ref[...]).astype(bf16)
    kv = _dot(nkv, wkv_ref[...])
    vm_ref[0] = kv[:, MLA_HEADS * HEAD_PAD:].astype(bf16)
    krb = z[:, ZC_KR:ZC_KR + LANES]
    lane = lax.broadcasted_iota(jnp.int32, (1, LANES), 1)
    rope_lanes = (lane >= MLA_NOPE) & (lane < MLA_QK)
    kr_a = jnp.where(rope_lanes, pltpu.roll(krb, MLA_NOPE, axis=1), 0.0)
    kr_b = jnp.where(rope_lanes, pltpu.roll(krb, MLA_ROPE, axis=1), 0.0)
    gk = gk_ref[...]
    gkp = gkp_ref[...]
    rot_part = (kr_b * gkp) * sin_t
    for hd in range(MLA_HEADS):
        kfull = kv[:, hd * HEAD_PAD:(hd + 1) * HEAD_PAD] + kr_a
        r = _rms(kfull, MLA_QK)
        km_ref[0, :, hd * HEAD_PAD:(hd + 1) * HEAD_PAD] = (
            ((kfull * gk) * cos_t + rot_part) * r).astype(bf16)


def _inproj_call(h, mod, g, win, naq_g, nak_g, cq_g, ckv_g, wq, wkv, gq, gqp, gk, gkp, cos_t, sin_t):
    qk_w = MLA_HEADS * HEAD_PAD
    tab_spec = pl.BlockSpec((TM, LANES), lambda b, j: (j, 0))
    return pl.pallas_call(
        _inproj_kernel,
        grid=(BATCH, N_TILES),
        in_specs=[
            _tok_spec(D_MODEL), _mod_spec(), _full_spec((1, D_MODEL)),
            _full_spec((D_MODEL, Z_COLS)),
            _full_spec((1, NA_WIDTH)), _full_spec((1, NA_WIDTH)),
            _full_spec((1, Q_LORA)), _full_spec((1, KV_LORA)),
            _full_spec((Q_LORA, 2 * qk_w)), _full_spec((KV_LORA, qk_w + MLA_WIDTH)),
            _full_spec((1, LANES)), _full_spec((1, LANES)),
            _full_spec((1, LANES)), _full_spec((1, LANES)),
            tab_spec, tab_spec,
        ],
        out_specs=[
            _tok_spec(NA_WIDTH), _tok_spec(NA_WIDTH), _tok_spec(NA_WIDTH),
            _tok_spec(qk_w), _tok_spec(qk_w), _tok_spec(MLA_WIDTH), _tok_spec(S5_WIDTH),
        ],
        out_shape=[
            jax.ShapeDtypeStruct((BATCH, S_ALL, NA_WIDTH), bf16),
            jax.ShapeDtypeStruct((BATCH, S_ALL, NA_WIDTH), bf16),
            jax.ShapeDtypeStruct((BATCH, S_ALL, NA_WIDTH), bf16),
            jax.ShapeDtypeStruct((BATCH, S_ALL, qk_w), bf16),
            jax.ShapeDtypeStruct((BATCH, S_ALL, qk_w), bf16),
            jax.ShapeDtypeStruct((BATCH, S_ALL, MLA_WIDTH), bf16),
            jax.ShapeDtypeStruct((BATCH, S_ALL, S5_WIDTH), f32),
        ],
        compiler_params=pltpu.CompilerParams(
            dimension_semantics=("arbitrary", "arbitrary"), vmem_limit_bytes=VMEM_LIMIT),
        name="in_proj_heads",
    )(h, mod, g, win, naq_g, nak_g, cq_g, ckv_g, wq, wkv, gq, gqp, gk, gkp, cos_t, sin_t)


def _softmax_pv(score_blocks, value_blocks):
    m = None
    for s in score_blocks:
        mi = jnp.max(s, axis=-1, keepdims=True)
        m = mi if m is None else jnp.maximum(m, mi)
    den = None
    acc = None
    for s, v in zip(score_blocks, value_blocks):
        p = jnp.exp(s - m)
        li = jnp.sum(p, axis=-1, keepdims=True)
        oi = _dot(p.astype(bf16), v)
        den = li if den is None else den + li
        acc = oi if acc is None else acc + oi
    return acc / den


NA_ROWS_PER_TILE = TM // GRID_W
NA_WIN_KEYS = WIN_ROWS * GRID_W


def _na_kernel(q_ref, k_ref, v_ref, bias_ref, o_ref):
    j = pl.program_id(2)
    scale = NA_HEAD_DIM ** -0.5
    lane = lax.broadcasted_iota(jnp.int32, (1, LANES), 1)
    lo = lane < NA_HEAD_DIM
    kc = k_ref[0, 0:CTX_LEN, :]
    vc = v_ref[0, 0:CTX_LEN, :]

    @pl.when(j == 0)
    def _():
        q = q_ref[0]
        outs = []
        for hh in range(2):
            qh = jnp.where(lo if hh == 0 else ~lo, q, jnp.zeros_like(q))
            s = _dot_nt(qh, kc) * scale
            outs.append(_softmax_pv([s], [vc]))
        o_ref[0] = jnp.where(lo, outs[0], outs[1]).astype(bf16)

    @pl.when(j > 0)
    def _():
        for rr in range(NA_ROWS_PER_TILE):
            r = (j - 1) * NA_ROWS_PER_TILE + rr
            rs = jnp.clip(r - WIN_ROWS // 2, 0, GRID_ROWS - WIN_ROWS)
            var = r - rs
            start = pl.multiple_of(CTX_LEN + rs * GRID_W, GRID_W)
            q = q_ref[0, rr * GRID_W:(rr + 1) * GRID_W, :]
            kw = k_ref[0, pl.ds(start, NA_WIN_KEYS), :]
            vw = v_ref[0, pl.ds(start, NA_WIN_KEYS), :]
            outs = []
            for hh in range(2):
                qh = jnp.where(lo if hh == 0 else ~lo, q, jnp.zeros_like(q))
                s_w = _dot_nt(qh, kw) * scale + bias_ref[hh, var]
                s_c = _dot_nt(qh, kc) * scale
                outs.append(_softmax_pv([s_w, s_c], [vw, vc]))
            o_ref[0, rr * GRID_W:(rr + 1) * GRID_W, :] = jnp.where(lo, outs[0], outs[1]).astype(bf16)


def _na_call(qn, kn, vn, bias):
    n_pairs = NA_WIDTH // LANES
    return pl.pallas_call(
        _na_kernel,
        grid=(BATCH, n_pairs, N_TILES),
        in_specs=[
            pl.BlockSpec((1, TM, LANES), lambda b, p, j: (b, j, p)),
            pl.BlockSpec((1, S_ALL, LANES), lambda b, p, j: (b, 0, p)),
            pl.BlockSpec((1, S_ALL, LANES), lambda b, p, j: (b, 0, p)),
            pl.BlockSpec((2, WIN_ROWS, GRID_W, NA_WIN_KEYS), lambda b, p, j: (p, 0, 0, 0)),
        ],
        out_specs=pl.BlockSpec((1, TM, LANES), lambda b, p, j: (b, j, p)),
        out_shape=jax.ShapeDtypeStruct((BATCH, S_ALL, NA_WIDTH), bf16),
        compiler_params=pltpu.CompilerParams(
            dimension_semantics=("arbitrary", "arbitrary", "arbitrary"), vmem_limit_bytes=VMEM_LIMIT),
        name="na_attention",
    )(qn, kn, vn, bias)


def _mla_kernel(q_ref, k_ref, v_ref, o_ref):
    j = pl.program_id(2)
    scale = MLA_QK ** -0.5
    lane = lax.broadcasted_iota(jnp.int32, (1, LANES), 1)
    lo = lane < MLA_V_DIM

    def attend(n_keys):
        outs = []
        v = v_ref[0, 0:n_keys, :]
        for hh in range(2):
            q = q_ref[0, :, hh * HEAD_PAD:(hh + 1) * HEAD_PAD]
            k = k_ref[0, 0:n_keys, hh * HEAD_PAD:(hh + 1) * HEAD_PAD]
            s = _dot_nt(q, k) * scale
            outs.append(_softmax_pv([s], [v]))
        o_ref[0] = jnp.where(lo, outs[0], outs[1]).astype(bf16)

    @pl.when(j == 0)
    def _():
        attend(CTX_LEN)

    @pl.when(j > 0)
    def _():
        attend(S_ALL)


def _mla_call(qm, km, vm):
    n_pairs = MLA_HEADS // 2
    return pl.pallas_call(
        _mla_kernel,
        grid=(BATCH, n_pairs, N_TILES),
        in_specs=[
            pl.BlockSpec((1, TM, 2 * HEAD_PAD), lambda b, p, j: (b, j, p)),
            pl.BlockSpec((1, S_ALL, 2 * HEAD_PAD), lambda b, p, j: (b, 0, p)),
            pl.BlockSpec((1, S_ALL, LANES), lambda b, p, j: (b, 0, p)),
        ],
        out_specs=pl.BlockSpec((1, TM, LANES), lambda b, p, j: (b, j, p)),
        out_shape=jax.ShapeDtypeStruct((BATCH, S_ALL, MLA_WIDTH), bf16),
        compiler_params=pltpu.CompilerParams(
            dimension_semantics=("arbitrary", "arbitrary", "arbitrary"), vmem_limit_bytes=VMEM_LIMIT),
        name="mla_attention",
    )(qm, km, vm)


def _s5_scan_kernel(u_ref, bmat_ref, cmat_ref, lam_ref, y_ref, h_ref, st_ref, *, reverse):
    i = pl.program_id(0)

    @pl.when(i == 0)
    def _():
        st_ref[...] = jnp.zeros_like(st_ref)

    h_ref[...] = _dot(u_ref[...].astype(bf16), bmat_ref[...])

    def step(t, carry):
        hr, hi = carry
        row = (S5_T - 1 - t) if reverse else t
        off = pl.multiple_of(row * BATCH, BATCH)
        lr = lam_ref[:, 0:S5_LANES]
        li = lam_ref[:, S5_LANES:]
        nr = (lr * hr - li * hi) + h_ref[pl.ds(off, BATCH), 0:S5_LANES]
        ni = (lr * hi + li * hr) + h_ref[pl.ds(off, BATCH), S5_LANES:]
        h_ref[pl.ds(off, BATCH), 0:S5_LANES] = nr
        h_ref[pl.ds(off, BATCH), S5_LANES:] = ni
        return nr, ni

    hr, hi = lax.fori_loop(0, S5_T, step, (st_ref[:, 0:S5_LANES], st_ref[:, S5_LANES:]))
    st_ref[:, 0:S5_LANES] = hr
    st_ref[:, S5_LANES:] = hi

    blk = 256
    for rb in range(S5_ROWS // blk):
        y_ref[rb * blk:(rb + 1) * blk, :] = _dot(
            h_ref[rb * blk:(rb + 1) * blk, :].astype(bf16), cmat_ref[...])


def _s5_scan_call(u_t, bmat, cmat, lam, reverse):
    if reverse:
        def chunk(i):
            return jnp.where(i < S5_CTX_CHUNKS, S5_CTX_CHUNKS - 1 - i, S5_CHUNKS + S5_CTX_CHUNKS - 1 - i)
    else:
        def chunk(i):
            return i
    return pl.pallas_call(
        functools.partial(_s5_scan_kernel, reverse=reverse),
        grid=(S5_CHUNKS,),
        in_specs=[
            pl.BlockSpec((S5_ROWS, S5_WIDTH), lambda i: (chunk(i), 0)),
            pl.BlockSpec((S5_WIDTH, 2 * S5_LANES), lambda i: (0, 0)),
            pl.BlockSpec((2 * S5_LANES, S5_WIDTH), lambda i: (0, 0)),
            pl.BlockSpec((BATCH, 2 * S5_LANES), lambda i: (0, 0)),
        ],
        out_specs=pl.BlockSpec((S5_ROWS, S5_WIDTH), lambda i: (chunk(i), 0)),
        out_shape=jax.ShapeDtypeStruct((S_ALL * BATCH, S5_WIDTH), f32),
        scratch_shapes=[pltpu.VMEM((S5_ROWS, 2 * S5_LANES), f32), pltpu.VMEM((BATCH, 2 * S5_LANES), f32)],
        compiler_params=pltpu.CompilerParams(
            dimension_semantics=("arbitrary",), vmem_limit_bytes=VMEM_LIMIT),
        name="s5_scan_bwd" if reverse else "s5_scan_fwd",
    )(u_t, bmat, cmat, lam)


def _s5_out_kernel(yf_ref, yb_ref, u_ref, d_ref, w_ref, b_ref, o_ref):
    y = (yf_ref[...] + yb_ref[...]) + d_ref[...] * u_ref[...]
    c0 = math.sqrt(2.0 / math.pi)
    gl = 0.5 * y * (1.0 + jnp.tanh(c0 * (y + 0.044715 * (y * y * y))))
    o = _dot(gl.astype(bf16), w_ref[...]) + b_ref[...]
    o_ref[...] = (o[:, :S5_WIDTH] * _sigmoid(o[:, S5_WIDTH:])).astype(bf16)


def _s5_out_call(yf, yb, u_t, d, w_glu, b_glu):
    rows = 1024
    row_spec = pl.BlockSpec((rows, S5_WIDTH), lambda i: (i, 0))
    return pl.pallas_call(
        _s5_out_kernel,
        grid=(S_ALL * BATCH // rows,),
        in_specs=[
            row_spec, row_spec, row_spec,
            pl.BlockSpec((1, S5_WIDTH), lambda i: (0, 0)),
            pl.BlockSpec((S5_WIDTH, 2 * S5_WIDTH), lambda i: (0, 0)),
            pl.BlockSpec((1, 2 * S5_WIDTH), lambda i: (0, 0)),
        ],
        out_specs=row_spec,
        out_shape=jax.ShapeDtypeStruct((S_ALL * BATCH, S5_WIDTH), bf16),
        compiler_params=pltpu.CompilerParams(
            dimension_semantics=("arbitrary",), vmem_limit_bytes=VMEM_LIMIT),
        name="s5_readout",
    )(yf, yb, u_t, d, w_glu, b_glu)


def _outproj_kernel(h_ref, mod_ref, a_ref, b_ref, s_ref, w_ref, o_ref):
    gate = mod_ref[0, 0, 5:6, :]
    o = _dot(a_ref[0], w_ref[0:NA_WIDTH, :])
    o += _dot(b_ref[0], w_ref[NA_WIDTH:NA_WIDTH + MLA_WIDTH, :])
    o += _dot(s_ref[0], w_ref[NA_WIDTH + MLA_WIDTH:, :])
    o_ref[0] = h_ref[0] + gate * o


def _outproj_call(h, mod, a, b, s, w_out):
    return pl.pallas_call(
        _outproj_kernel,
        grid=(BATCH, N_TILES),
        in_specs=[
            _tok_spec(D_MODEL), _mod_spec(), _tok_spec(NA_WIDTH), _tok_spec(MLA_WIDTH),
            _tok_spec(S5_WIDTH), _full_spec((D_MODEL, D_MODEL)),
        ],
        out_specs=_tok_spec(D_MODEL),
        out_shape=jax.ShapeDtypeStruct((BATCH, S_ALL, D_MODEL), f32),
        compiler_params=pltpu.CompilerParams(
            dimension_semantics=("arbitrary", "arbitrary"), vmem_limit_bytes=VMEM_LIMIT),
        name="mixer_out_proj",
    )(h, mod, a, b, s, w_out)


def _rope_perm():
    half = MLA_ROPE // 2
    quarter = half // 2
    idx, sign = [], []
    for j in range(MLA_ROPE):
        if (j % half) < quarter:
            idx.append(j + quarter)
            sign.append(-1.0)
        else:
            idx.append(j - quarter)
            sign.append(1.0)
    return jnp.array(idx, jnp.int32), jnp.array(sign, f32)


def _rope_tables():
    quarter = MLA_ROPE // 4
    t = jnp.arange(SEQ)
    row = (t // GRID_W).astype(f32)
    col = (t % GRID_W).astype(f32)
    inv = ROPE_THETA ** (-jnp.arange(quarter, dtype=f32) / quarter)
    ang_r = row[:, None] * inv
    ang_c = col[:, None] * inv
    cos32 = jnp.concatenate([jnp.cos(ang_r), jnp.cos(ang_r), jnp.cos(ang_c), jnp.cos(ang_c)], axis=-1)
    sin32 = jnp.concatenate([jnp.sin(ang_r), jnp.sin(ang_r), jnp.sin(ang_c), jnp.sin(ang_c)], axis=-1)
    cos32 = jnp.concatenate([jnp.ones((CTX_LEN, MLA_ROPE), f32), cos32], axis=0)
    sin32 = jnp.concatenate([jnp.zeros((CTX_LEN, MLA_ROPE), f32), sin32], axis=0)
    pad = jnp.zeros((S_ALL, HEAD_PAD - MLA_QK), f32)
    cos_t = jnp.concatenate([jnp.ones((S_ALL, MLA_NOPE), f32), cos32, pad], axis=-1)
    sin_t = jnp.concatenate([jnp.zeros((S_ALL, MLA_NOPE), f32), sin32, pad], axis=-1)
    return cos_t, sin_t


def _na_bias_table(rpb):
    var = jnp.arange(WIN_ROWS)
    i = jnp.arange(WIN_ROWS)
    cq = jnp.arange(GRID_W)
    kc = jnp.arange(GRID_W)
    d_row = i[None, :] - var[:, None] + WIN_ROWS - 1
    d_col = jnp.clip(kc[None, :] - cq[:, None], -(WIN_COLS - 1), WIN_COLS - 1) + WIN_COLS - 1
    col_start = jnp.clip(cq - WIN_COLS // 2, 0, GRID_W - WIN_COLS)
    in_win = (kc[None, :] >= col_start[:, None]) & (kc[None, :] < col_start[:, None] + WIN_COLS)
    tab = rpb[:, d_row[:, :, None, None], d_col[None, None, :, :]]
    tab = jnp.where(in_win[None, None, None], tab, NEG_INF)
    tab = tab.transpose(0, 1, 3, 2, 4)
    return tab.reshape(NA_HEADS, WIN_ROWS, GRID_W, NA_WIN_KEYS).astype(f32)


def _pad_heads(w, width):
    k = w.shape[0]
    w = w.reshape(k, MLA_HEADS, width)
    w = jnp.pad(w, ((0, 0), (0, 0), (0, HEAD_PAD - width)))
    return w.reshape(k, MLA_HEADS * HEAD_PAD)


def _layer_params(l, ffn_w_gu, ffn_w_down, w_in, w_out, na_qk_g, na_rpb, mla_cq_g, mla_ckv_g, mla_w_uq,
                  mla_w_ukv, mla_qk_g, s5_lam_re, s5_lam_im, s5_log_dt, s5_b_re, s5_b_im, s5_c_re,
                  s5_c_im, s5_d, s5_w_glu, s5_b_glu):
    p = {}
    perm_idx, perm_sign = _rope_perm()
    for k in range(2):
        wgu = ffn_w_gu[l, k]
        gate_w = wgu[:, :D_FF].reshape(D_MODEL, N_FF_CHUNKS, FF_CHUNK)
        up_w = wgu[:, D_FF:].reshape(D_MODEL, N_FF_CHUNKS, FF_CHUNK)
        p[f"wgu{k}"] = jnp.concatenate([gate_w, up_w], axis=-1).transpose(1, 0, 2).astype(bf16)
        p[f"wd{k}"] = ffn_w_down[l, k].reshape(N_FF_CHUNKS, FF_CHUNK, D_MODEL).astype(bf16)
    wi = w_in[l]
    o_cq = 3 * NA_WIDTH
    o_ckv = o_cq + Q_LORA
    o_kr = o_ckv + KV_LORA
    o_u = o_kr + MLA_ROPE
    w_kr = wi[:, o_kr:o_kr + MLA_ROPE]
    w_kr_perm = w_kr[:, perm_idx] * perm_sign[None, :]
    p["win"] = jnp.concatenate([
        wi[:, :o_cq], wi[:, o_cq:o_ckv], wi[:, o_ckv:o_kr], wi[:, o_u:o_u + S5_WIDTH],
        w_kr, w_kr_perm, jnp.zeros((D_MODEL, LANES - 2 * MLA_ROPE), f32)], axis=-1).astype(bf16)
    p["naq_g"] = jnp.tile(na_qk_g[l, 0], NA_HEADS)[None, :]
    p["nak_g"] = jnp.tile(na_qk_g[l, 1], NA_HEADS)[None, :]
    p["cq_g"] = mla_cq_g[l][None, :]
    p["ckv_g"] = mla_ckv_g[l][None, :]
    wuq = mla_w_uq[l].reshape(Q_LORA, MLA_HEADS, MLA_QK)
    rope_cols = wuq[:, :, MLA_NOPE:]
    partner = jnp.concatenate([jnp.zeros((Q_LORA, MLA_HEADS, MLA_NOPE), f32),
                               rope_cols[:, :, perm_idx] * perm_sign[None, None, :]], axis=-1)
    p["wq"] = jnp.concatenate([_pad_heads(wuq.reshape(Q_LORA, -1), MLA_QK),
                               _pad_heads(partner.reshape(Q_LORA, -1), MLA_QK)], axis=-1).astype(bf16)
    wukv = mla_w_ukv[l].reshape(KV_LORA, MLA_HEADS, MLA_NOPE + MLA_V_DIM)
    p["wkv"] = jnp.concatenate([_pad_heads(wukv[:, :, :MLA_NOPE].reshape(KV_LORA, -1), MLA_NOPE),
                                wukv[:, :, MLA_NOPE:].reshape(KV_LORA, -1)], axis=-1).astype(bf16)

    def pad_gain(g):
        tail = g[MLA_NOPE:]
        z = jnp.zeros((HEAD_PAD - MLA_QK,), f32)
        full = jnp.concatenate([g, z])[None, :]
        part = jnp.concatenate([jnp.zeros((MLA_NOPE,), f32), tail[perm_idx], z])[None, :]
        return full, part

    p["gq"], p["gqp"] = pad_gain(mla_qk_g[l, 0])
    p["gk"], p["gkp"] = pad_gain(mla_qk_g[l, 1])
    p["na_bias"] = _na_bias_table(na_rpb[l])
    eye_g = jnp.eye(S5_GROUPS, dtype=f32)
    for d in range(2):
        lam = s5_lam_re[l, d].astype(f32) + 1j * s5_lam_im[l, d].astype(f32)
        dt = jnp.exp(s5_log_dt[l, d].astype(f32))[:, None]
        lam_bar = jnp.exp(lam * dt)
        b = s5_b_re[l, d].astype(f32) + 1j * s5_b_im[l, d].astype(f32)
        b_bar = ((lam_bar - 1) / lam)[..., None] * b
        b_re = jnp.einsum('gpm,gh->gmhp', b_bar.real, eye_g).reshape(S5_WIDTH, S5_LANES)
        b_im = jnp.einsum('gpm,gh->gmhp', b_bar.imag, eye_g).reshape(S5_WIDTH, S5_LANES)
        p[f"bmat{d}"] = jnp.concatenate([b_re, b_im], axis=-1).astype(bf16)
        c_re = jnp.einsum('gmp,gh->gphm', s5_c_re[l, d].astype(f32), eye_g).reshape(S5_LANES, S5_WIDTH)
        c_im = jnp.einsum('gmp,gh->gphm', s5_c_im[l, d].astype(f32), eye_g).reshape(S5_LANES, S5_WIDTH)
        p[f"cmat{d}"] = jnp.concatenate([c_re, -c_im], axis=0).astype(bf16)
        lam_row = jnp.concatenate([lam_bar.real.reshape(-1), lam_bar.imag.reshape(-1)])
        p[f"lam{d}"] = jnp.broadcast_to(lam_row[None, :], (BATCH, 2 * S5_LANES)).astype(f32)
    p["s5_d"] = s5_d[l][None, :]
    p["w_glu"] = s5_w_glu[l].astype(bf16)
    p["b_glu"] = s5_b_glu[l][None, :]
    p["w_out"] = w_out[l].astype(bf16)
    return p


def kernel(x, c, ctx, c_ctx, w_mod, b_mod, norm_g, ffn_w_gu, ffn_w_down, w_in, w_out, na_qk_g, na_rpb, mla_cq_g, mla_ckv_g, mla_w_uq, mla_w_ukv, mla_qk_g, s5_lam_re, s5_lam_im, s5_log_dt, s5_b_re, s5_b_im, s5_c_re, s5_c_im, s5_d, s5_w_glu, s5_b_glu):
    assert x.shape == (BATCH, SEQ, D_MODEL) and ctx.shape == (BATCH, CTX_LEN, D_MODEL)
    mod_rows = 16
    cvec = jnp.concatenate([c, c_ctx[None, :], jnp.zeros((mod_rows - BATCH - 1, D_MODEL), f32)], axis=0)
    mod_all = _mod_call(cvec, w_mod, b_mod).reshape(DEPTH, mod_rows, N_MOD, D_MODEL)
    cos_t, sin_t = _rope_tables()
    h = jnp.concatenate([ctx, x], axis=1)
    for l in range(DEPTH):
        p = _layer_params(l, ffn_w_gu, ffn_w_down, w_in, w_out, na_qk_g, na_rpb, mla_cq_g, mla_ckv_g,
                          mla_w_uq, mla_w_ukv, mla_qk_g, s5_lam_re, s5_lam_im, s5_log_dt, s5_b_re,
                          s5_b_im, s5_c_re, s5_c_im, s5_d, s5_w_glu, s5_b_glu)
        mod_c = jnp.broadcast_to(mod_all[l, BATCH][None], (BATCH, N_MOD, D_MODEL))
        mod = jnp.stack([mod_c, mod_all[l, :BATCH]], axis=1)
        h = _ffn_call(h, mod, norm_g[l, 0][None, :], p["wgu0"], p["wd0"], 0)
        qn, kn, vn, qm, km, vm, u = _inproj_call(
            h, mod, norm_g[l, 1][None, :], p["win"], p["naq_g"], p["nak_g"], p["cq_g"], p["ckv_g"],
            p["wq"], p["wkv"], p["gq"], p["gqp"], p["gk"], p["gkp"], cos_t, sin_t)
        a = _na_call(qn, kn, vn, p["na_bias"])
        bm = _mla_call(qm, km, vm)
        u_t = u.transpose(1, 0, 2).reshape(S_ALL * BATCH, S5_WIDTH)
        yf = _s5_scan_call(u_t, p["bmat0"], p["cmat0"], p["lam0"], reverse=False)
        yb = _s5_scan_call(u_t, p["bmat1"], p["cmat1"], p["lam1"], reverse=True)
        s_t = _s5_out_call(yf, yb, u_t, p["s5_d"], p["w_glu"], p["b_glu"])
        s = s_t.reshape(S_ALL, BATCH, S5_WIDTH).transpose(1, 0, 2)
        h = _outproj_call(h, mod, a, bm, s, p["w_out"])
        h = _ffn_call(h, mod, norm_g[l, 2][None, :], p["wgu1"], p["wd1"], 6)
    return h[:, CTX_LEN:, :]
```

```python
import functools
import math

import jax
import jax.numpy as jnp
from jax import lax
from jax.experimental import pallas as pl
from jax.experimental.pallas import tpu as pltpu

D_MODEL = 1024
BATCH = 8
SEQ = 2048
DEPTH = 2
CTX_LEN = 256
S_ALL = CTX_LEN + SEQ
GRID_W = 64
GRID_ROWS = SEQ // GRID_W
NA_HEAD_DIM = 64
NA_WIDTH = 384
NA_HEADS = 6
WIN_ROWS = 8
WIN_COLS = 16
MLA_V_DIM = 64
MLA_WIDTH = 384
MLA_HEADS = 6
MLA_NOPE = 64
MLA_ROPE = 32
MLA_QK = 96
Q_LORA = 384
KV_LORA = 256
S5_WIDTH = 256
S5_GROUP = 16
S5_GROUPS = 16
S5_STATE = 64
S5_LANES = S5_GROUPS * S5_STATE
D_FF = 2816
ROPE_THETA = 10000.0
EPS = 1e-6
N_MOD = 9
NEG_INF = -1e30

LANES = 128
TM = 256
N_TILES = S_ALL // TM
FF_CHUNK = 1408
N_FF_CHUNKS = D_FF // FF_CHUNK
HEAD_PAD = 128
S5_T = 128
S5_ROWS = S5_T * BATCH
S5_CHUNKS = S_ALL // S5_T
S5_CTX_CHUNKS = CTX_LEN // S5_T
VMEM_LIMIT = 56 * 1024 * 1024

ZC_QA, ZC_KA, ZC_VA, ZC_CQ, ZC_CKV, ZC_U, ZC_KR = 0, 384, 768, 1152, 1536, 1792, 2048
Z_COLS = 2176

f32 = jnp.float32
bf16 = jnp.bfloat16


def _dot(a, b):
    return jnp.dot(a, b, preferred_element_type=f32)


def _dot_nt(a, b):
    return lax.dot_general(a, b, (((1,), (1,)), ((), ())), preferred_element_type=f32)


def _sigmoid(x):
    return 1.0 / (1.0 + jnp.exp(-x))


def _rms(x, n):
    return lax.rsqrt(jnp.sum(x * x, axis=-1, keepdims=True) / n + EPS)


def _modulated(x, g, shift, scale):
    y = x * _rms(x, D_MODEL)
    return (y * g) * (1.0 + scale) + shift


def _mod_kernel(c_ref, w_ref, b_ref, o_ref):
    c = c_ref[...]
    a = (c * _sigmoid(c)).astype(bf16)
    o_ref[0] = _dot(a, w_ref[0].astype(bf16)) + b_ref[0]


def _mod_call(cvec, w_mod, b_mod):
    rows = cvec.shape[0]
    return pl.pallas_call(
        _mod_kernel,
        grid=(DEPTH, N_MOD),
        in_specs=[
            pl.BlockSpec((rows, D_MODEL), lambda l, j: (0, 0)),
            pl.BlockSpec((1, D_MODEL, D_MODEL), lambda l, j: (l, 0, j)),
            pl.BlockSpec((1, 1, D_MODEL), lambda l, j: (l, 0, j)),
        ],
        out_specs=pl.BlockSpec((1, rows, D_MODEL), lambda l, j: (l, 0, j)),
        out_shape=jax.ShapeDtypeStruct((DEPTH, rows, N_MOD * D_MODEL), f32),
        compiler_params=pltpu.CompilerParams(
            dimension_semantics=("arbitrary", "arbitrary"), vmem_limit_bytes=VMEM_LIMIT),
        name="adaln_mod",
    )(cvec, w_mod, b_mod.reshape(DEPTH, 1, N_MOD * D_MODEL))


def _mod_spec():
    return pl.BlockSpec((1, 1, N_MOD, D_MODEL), lambda b, j: (b, jnp.minimum(j, 1), 0, 0))


def _tok_spec(width):
    return pl.BlockSpec((1, TM, width), lambda b, j: (b, j, 0))


def _full_spec(shape):
    zeros = (0,) * len(shape)
    return pl.BlockSpec(shape, lambda b, j: zeros)


def _ffn_kernel(h_ref, mod_ref, g_ref, wgu_ref, wd_ref, o_ref, xm_ref, acc_ref, *, i0):
    x = h_ref[0]
    shift = mod_ref[0, 0, i0:i0 + 1, :]
    scale = mod_ref[0, 0, i0 + 1:i0 + 2, :]
    gate = mod_ref[0, 0, i0 + 2:i0 + 3, :]
    xm_ref[...] = _modulated(x, g_ref[...], shift, scale).astype(bf16)
    acc_ref[...] = jnp.zeros_like(acc_ref)

    def body(c, carry):
        gu = _dot(xm_ref[...], wgu_ref[c])
        gt = gu[:, :FF_CHUNK]
        up = gu[:, FF_CHUNK:]
        a = ((gt * _sigmoid(gt)) * up).astype(bf16)
        acc_ref[...] += _dot(a, wd_ref[c])
        return carry

    lax.fori_loop(0, N_FF_CHUNKS, body, 0)
    o_ref[0] = x + (0.5 * gate) * acc_ref[...]


def _ffn_call(h, mod, g, wgu, wd, i0):
    return pl.pallas_call(
        functools.partial(_ffn_kernel, i0=i0),
        grid=(BATCH, N_TILES),
        in_specs=[
            _tok_spec(D_MODEL),
            _mod_spec(),
            _full_spec((1, D_MODEL)),
            _full_spec((N_FF_CHUNKS, D_MODEL, 2 * FF_CHUNK)),
            _full_spec((N_FF_CHUNKS, FF_CHUNK, D_MODEL)),
        ],
        out_specs=_tok_spec(D_MODEL),
        out_shape=jax.ShapeDtypeStruct((BATCH, S_ALL, D_MODEL), f32),
        scratch_shapes=[pltpu.VMEM((TM, D_MODEL), bf16), pltpu.VMEM((TM, D_MODEL), f32)],
        compiler_params=pltpu.CompilerParams(
            dimension_semantics=("arbitrary", "arbitrary"), vmem_limit_bytes=VMEM_LIMIT),
        name="swiglu_half_step",
    )(h, mod, g, wgu, wd)


def _pair_head_norm(x, g):
    lane = lax.broadcasted_iota(jnp.int32, (1, LANES), 1)
    lo = lane < NA_HEAD_DIM
    outs = []
    for p in range(NA_WIDTH // LANES):
        xb = x[:, p * LANES:(p + 1) * LANES]
        sq = xb * xb
        s_lo = jnp.sum(jnp.where(lo, sq, 0.0), axis=-1, keepdims=True)
        s_hi = jnp.sum(jnp.where(lo, 0.0, sq), axis=-1, keepdims=True)
        r = jnp.where(lo, lax.rsqrt(s_lo / NA_HEAD_DIM + EPS), lax.rsqrt(s_hi / NA_HEAD_DIM + EPS))
        outs.append((xb * r) * g[:, p * LANES:(p + 1) * LANES])
    return outs


def _inproj_kernel(h_ref, mod_ref, g_ref, win_ref, naq_g_ref, nak_g_ref, cq_g_ref, ckv_g_ref,
                   wq_ref, wkv_ref, gq_ref, gqp_ref, gk_ref, gkp_ref, cos_ref, sin_ref,
                   qn_ref, kn_ref, vn_ref, qm_ref, km_ref, vm_ref, u_ref):
    x = h_ref[0]
    shift = mod_ref[0, 0, 3:4, :]
    scale = mod_ref[0, 0, 4:5, :]
    xm = _modulated(x, g_ref[...], shift, scale).astype(bf16)
    z = _dot(xm, win_ref[...])

    qn = _pair_head_norm(z[:, ZC_QA:ZC_QA + NA_WIDTH], naq_g_ref[...])
    kn = _pair_head_norm(z[:, ZC_KA:ZC_KA + NA_WIDTH], nak_g_ref[...])
    for p in range(NA_WIDTH // LANES):
        qn_ref[0, :, p * LANES:(p + 1) * LANES] = qn[p].astype(bf16)
        kn_ref[0, :, p * LANES:(p + 1) * LANES] = kn[p].astype(bf16)
    vn_ref[0] = z[:, ZC_VA:ZC_VA + NA_WIDTH].astype(bf16)
    u_ref[0] = z[:, ZC_U:ZC_U + S5_WIDTH]

    cos_t = cos_ref[...]
    sin_t = sin_ref[...]

    cq = z[:, ZC_CQ:ZC_CQ + Q_LORA]
    ncq = ((cq * _rms(cq, Q_LORA)) * cq_g_ref[...]).astype(bf16)
    qq = _dot(ncq, wq_ref[...])
    gq = gq_ref[...]
    gqp = gqp_ref[...]
    for hd in range(MLA_HEADS):
        pre = qq[:, hd * HEAD_PAD:(hd + 1) * HEAD_PAD]
        perm = qq[:, (MLA_HEADS + hd) * HEAD_PAD:(MLA_HEADS + hd + 1) * HEAD_PAD]
        r = _rms(pre, MLA_QK)
        qm_ref[0, :, hd * HEAD_PAD:(hd + 1) * HEAD_PAD] = (
            ((pre * gq) * cos_t + (perm * gqp) * sin_t) * r).astype(bf16)

    ckv = z[:, ZC_CKV:ZC_CKV + KV_LORA]
    nkv = ((ckv * _rms(ckv, KV_LORA)) * ckv_g_ref[...]).astype(bf16)
    kv = _dot(nkv, wkv_ref[...])
    vm_ref[0] = kv[:, MLA_HEADS * HEAD_PAD:].astype(bf16)
    krb = z[:, ZC_KR:ZC_KR + LANES]
    lane = lax.broadcasted_iota(jnp.int32, (1, LANES), 1)
    rope_lanes = (lane >= MLA_NOPE) & (lane < MLA_QK)
    kr_a = jnp.where(rope_lanes, pltpu.roll(krb, MLA_NOPE, axis=1), 0.0)
    kr_b = jnp.where(rope_lanes, pltpu.roll(krb, MLA_ROPE, axis=1), 0.0)
    gk = gk_ref[...]
    gkp = gkp_ref[...]
    rot_part = (kr_b * gkp) * sin_t
    for hd in range(MLA_HEADS):
        kfull = kv[:, hd * HEAD_PAD:(hd + 1) * HEAD_PAD] + kr_a
        r = _rms(kfull, MLA_QK)
        km_ref[0, :, hd * HEAD_PAD:(hd + 1) * HEAD_PAD] = (
            ((kfull * gk) * cos_t + rot_part) * r).astype(bf16)


def _inproj_call(h, mod, g, win, naq_g, nak_g, cq_g, ckv_g, wq, wkv, gq, gqp, gk, gkp, cos_t, sin_t):
    qk_w = MLA_HEADS * HEAD_PAD
    tab_spec = pl.BlockSpec((TM, LANES), lambda b, j: (j, 0))
    return pl.pallas_call(
        _inproj_kernel,
        grid=(BATCH, N_TILES),
        in_specs=[
            _tok_spec(D_MODEL), _mod_spec(), _full_spec((1, D_MODEL)),
            _full_spec((D_MODEL, Z_COLS)),
            _full_spec((1, NA_WIDTH)), _full_spec((1, NA_WIDTH)),
            _full_spec((1, Q_LORA)), _full_spec((1, KV_LORA)),
            _full_spec((Q_LORA, 2 * qk_w)), _full_spec((KV_LORA, qk_w + MLA_WIDTH)),
            _full_spec((1, LANES)), _full_spec((1, LANES)),
            _full_spec((1, LANES)), _full_spec((1, LANES)),
            tab_spec, tab_spec,
        ],
        out_specs=[
            _tok_spec(NA_WIDTH), _tok_spec(NA_WIDTH), _tok_spec(NA_WIDTH),
            _tok_spec(qk_w), _tok_spec(qk_w), _tok_spec(MLA_WIDTH), _tok_spec(S5_WIDTH),
        ],
        out_shape=[
            jax.ShapeDtypeStruct((BATCH, S_ALL, NA_WIDTH), bf16),
            jax.ShapeDtypeStruct((BATCH, S_ALL, NA_WIDTH), bf16),
            jax.ShapeDtypeStruct((BATCH, S_ALL, NA_WIDTH), bf16),
            jax.ShapeDtypeStruct((BATCH, S_ALL, qk_w), bf16),
            jax.ShapeDtypeStruct((BATCH, S_ALL, qk_w), bf16),
            jax.ShapeDtypeStruct((BATCH, S_ALL, MLA_WIDTH), bf16),
            jax.ShapeDtypeStruct((BATCH, S_ALL, S5_WIDTH), f32),
        ],
        compiler_params=pltpu.CompilerParams(
            dimension_semantics=("arbitrary", "arbitrary"), vmem_limit_bytes=VMEM_LIMIT),
        name="in_proj_heads",
    )(h, mod, g, win, naq_g, nak_g, cq_g, ckv_g, wq, wkv, gq, gqp, gk, gkp, cos_t, sin_t)


def _softmax_pv(score_blocks, value_blocks):
    m = None
    for s in score_blocks:
        mi = jnp.max(s, axis=-1, keepdims=True)
        m = mi if m is None else jnp.maximum(m, mi)
    den = None
    acc = None
    for s, v in zip(score_blocks, value_blocks):
        p = jnp.exp(s - m)
        li = jnp.sum(p, axis=-1, keepdims=True)
        oi = _dot(p.astype(bf16), v)
        den = li if den is None else den + li
        acc = oi if acc is None else acc + oi
    return acc / den


NA_ROWS_PER_TILE = TM // GRID_W
NA_WIN_KEYS = WIN_ROWS * GRID_W


def _na_kernel(q_ref, k_ref, v_ref, bias_ref, o_ref):
    j = pl.program_id(2)
    scale = NA_HEAD_DIM ** -0.5
    lane = lax.broadcasted_iota(jnp.int32, (1, LANES), 1)
    lo = lane < NA_HEAD_DIM
    kc = k_ref[0, 0:CTX_LEN, :]
    vc = v_ref[0, 0:CTX_LEN, :]

    @pl.when(j == 0)
    def _():
        q = q_ref[0]
        outs = []
        for hh in range(2):
            qh = jnp.where(lo if hh == 0 else ~lo, q, jnp.zeros_like(q))
            s = _dot_nt(qh, kc) * scale
            outs.append(_softmax_pv([s], [vc]))
        o_ref[0] = jnp.where(lo, outs[0], outs[1]).astype(bf16)

    @pl.when(j > 0)
    def _():
        for rr in range(NA_ROWS_PER_TILE):
            r = (j - 1) * NA_ROWS_PER_TILE + rr
            rs = jnp.clip(r - WIN_ROWS // 2, 0, GRID_ROWS - WIN_ROWS)
            var = r - rs
            start = pl.multiple_of(CTX_LEN + rs * GRID_W, GRID_W)
            q = q_ref[0, rr * GRID_W:(rr + 1) * GRID_W, :]
            kw = k_ref[0, pl.ds(start, NA_WIN_KEYS), :]
            vw = v_ref[0, pl.ds(start, NA_WIN_KEYS), :]
            outs = []
            for hh in range(2):
                qh = jnp.where(lo if hh == 0 else ~lo, q, jnp.zeros_like(q))
                s_w = _dot_nt(qh, kw) * scale + bias_ref[hh, var]
                s_c = _dot_nt(qh, kc) * scale
                outs.append(_softmax_pv([s_w, s_c], [vw, vc]))
            o_ref[0, rr * GRID_W:(rr + 1) * GRID_W, :] = jnp.where(lo, outs[0], outs[1]).astype(bf16)


def _na_call(qn, kn, vn, bias):
    n_pairs = NA_WIDTH // LANES
    return pl.pallas_call(
        _na_kernel,
        grid=(BATCH, n_pairs, N_TILES),
        in_specs=[
            pl.BlockSpec((1, TM, LANES), lambda b, p, j: (b, j, p)),
            pl.BlockSpec((1, S_ALL, LANES), lambda b, p, j: (b, 0, p)),
            pl.BlockSpec((1, S_ALL, LANES), lambda b, p, j: (b, 0, p)),
            pl.BlockSpec((2, WIN_ROWS, GRID_W, NA_WIN_KEYS), lambda b, p, j: (p, 0, 0, 0)),
        ],
        out_specs=pl.BlockSpec((1, TM, LANES), lambda b, p, j: (b, j, p)),
        out_shape=jax.ShapeDtypeStruct((BATCH, S_ALL, NA_WIDTH), bf16),
        compiler_params=pltpu.CompilerParams(
            dimension_semantics=("arbitrary", "arbitrary", "arbitrary"), vmem_limit_bytes=VMEM_LIMIT),
        name="na_attention",
    )(qn, kn, vn, bias)


def _mla_kernel(q_ref, k_ref, v_ref, o_ref):
    j = pl.program_id(2)
    scale = MLA_QK ** -0.5
    lane = lax.broadcasted_iota(jnp.int32, (1, LANES), 1)
    lo = lane < MLA_V_DIM

    def attend(n_keys):
        outs = []
        v = v_ref[0, 0:n_keys, :]
        for hh in range(2):
            q = q_ref[0, :, hh * HEAD_PAD:(hh + 1) * HEAD_PAD]
            k = k_ref[0, 0:n_keys, hh * HEAD_PAD:(hh + 1) * HEAD_PAD]
            s = _dot_nt(q, k) * scale
            outs.append(_softmax_pv([s], [v]))
        o_ref[0] = jnp.where(lo, outs[0], outs[1]).astype(bf16)

    @pl.when(j == 0)
    def _():
        attend(CTX_LEN)

    @pl.when(j > 0)
    def _():
        attend(S_ALL)


def _mla_call(qm, km, vm):
    n_pairs = MLA_HEADS // 2
    return pl.pallas_call(
        _mla_kernel,
        grid=(BATCH, n_pairs, N_TILES),
        in_specs=[
            pl.BlockSpec((1, TM, 2 * HEAD_PAD), lambda b, p, j: (b, j, p)),
            pl.BlockSpec((1, S_ALL, 2 * HEAD_PAD), lambda b, p, j: (b, 0, p)),
            pl.BlockSpec((1, S_ALL, LANES), lambda b, p, j: (b, 0, p)),
        ],
        out_specs=pl.BlockSpec((1, TM, LANES), lambda b, p, j: (b, j, p)),
        out_shape=jax.ShapeDtypeStruct((BATCH, S_ALL, MLA_WIDTH), bf16),
        compiler_params=pltpu.CompilerParams(
            dimension_semantics=("arbitrary", "arbitrary", "arbitrary"), vmem_limit_bytes=VMEM_LIMIT),
        name="mla_attention",
    )(qm, km, vm)


def _s5_scan_kernel(u_ref, bmat_ref, cmat_ref, lam_ref, y_ref, h_ref, st_ref, *, reverse):
    i = pl.program_id(0)

    @pl.when(i == 0)
    def _():
        st_ref[...] = jnp.zeros_like(st_ref)

    h_ref[...] = _dot(u_ref[...].astype(bf16), bmat_ref[...])

    def step(t, carry):
        hr, hi = carry
        row = (S5_T - 1 - t) if reverse else t
        off = pl.multiple_of(row * BATCH, BATCH)
        lr = lam_ref[:, 0:S5_LANES]
        li = lam_ref[:, S5_LANES:]
        nr = (lr * hr - li * hi) + h_ref[pl.ds(off, BATCH), 0:S5_LANES]
        ni = (lr * hi + li * hr) + h_ref[pl.ds(off, BATCH), S5_LANES:]
        h_ref[pl.ds(off, BATCH), 0:S5_LANES] = nr
        h_ref[pl.ds(off, BATCH), S5_LANES:] = ni
        return nr, ni

    hr, hi = lax.fori_loop(0, S5_T, step, (st_ref[:, 0:S5_LANES], st_ref[:, S5_LANES:]))
    st_ref[:, 0:S5_LANES] = hr
    st_ref[:, S5_LANES:] = hi

    blk = 256
    for rb in range(S5_ROWS // blk):
        y_ref[rb * blk:(rb + 1) * blk, :] = _dot(
            h_ref[rb * blk:(rb + 1) * blk, :].astype(bf16), cmat_ref[...])


def _s5_scan_call(u_t, bmat, cmat, lam, reverse):
    if reverse:
        def chunk(i):
            return jnp.where(i < S5_CTX_CHUNKS, S5_CTX_CHUNKS - 1 - i, S5_CHUNKS + S5_CTX_CHUNKS - 1 - i)
    else:
        def chunk(i):
            return i
    return pl.pallas_call(
        functools.partial(_s5_scan_kernel, reverse=reverse),
        grid=(S5_CHUNKS,),
        in_specs=[
            pl.BlockSpec((S5_ROWS, S5_WIDTH), lambda i: (chunk(i), 0)),
            pl.BlockSpec((S5_WIDTH, 2 * S5_LANES), lambda i: (0, 0)),
            pl.BlockSpec((2 * S5_LANES, S5_WIDTH), lambda i: (0, 0)),
            pl.BlockSpec((BATCH, 2 * S5_LANES), lambda i: (0, 0)),
        ],
        out_specs=pl.BlockSpec((S5_ROWS, S5_WIDTH), lambda i: (chunk(i), 0)),
        out_shape=jax.ShapeDtypeStruct((S_ALL * BATCH, S5_WIDTH), f32),
        scratch_shapes=[pltpu.VMEM((S5_ROWS, 2 * S5_LANES), f32), pltpu.VMEM((BATCH, 2 * S5_LANES), f32)],
        compiler_params=pltpu.CompilerParams(
            dimension_semantics=("arbitrary",), vmem_limit_bytes=VMEM_LIMIT),
        name="s5_scan_bwd" if reverse else "s5_scan_fwd",
    )(u_t, bmat, cmat, lam)


def _s5_out_kernel(yf_ref, yb_ref, u_ref, d_ref, w_ref, b_ref, o_ref):
    y = (yf_ref[...] + yb_ref[...]) + d_ref[...] * u_ref[...]
    c0 = math.sqrt(2.0 / math.pi)
    gl = 0.5 * y * (1.0 + jnp.tanh(c0 * (y + 0.044715 * (y * y * y))))
    o = _dot(gl.astype(bf16), w_ref[...]) + b_ref[...]
    o_ref[...] = (o[:, :S5_WIDTH] * _sigmoid(o[:, S5_WIDTH:])).astype(bf16)


def _s5_out_call(yf, yb, u_t, d, w_glu, b_glu):
    rows = 1024
    row_spec = pl.BlockSpec((rows, S5_WIDTH), lambda i: (i, 0))
    return pl.pallas_call(
        _s5_out_kernel,
        grid=(S_ALL * BATCH // rows,),
        in_specs=[
            row_spec, row_spec, row_spec,
            pl.BlockSpec((1, S5_WIDTH), lambda i: (0, 0)),
            pl.BlockSpec((S5_WIDTH, 2 * S5_WIDTH), lambda i: (0, 0)),
            pl.BlockSpec((1, 2 * S5_WIDTH), lambda i: (0, 0)),
        ],
        out_specs=row_spec,
        out_shape=jax.ShapeDtypeStruct((S_ALL * BATCH, S5_WIDTH), bf16),
        compiler_params=pltpu.CompilerParams(
            dimension_semantics=("arbitrary",), vmem_limit_bytes=VMEM_LIMIT),
        name="s5_readout",
    )(yf, yb, u_t, d, w_glu, b_glu)


def _outproj_kernel(h_ref, mod_ref, a_ref, b_ref, s_ref, w_ref, o_ref):
    gate = mod_ref[0, 0, 5:6, :]
    o = _dot(a_ref[0], w_ref[0:NA_WIDTH, :])
    o += _dot(b_ref[0], w_ref[NA_WIDTH:NA_WIDTH + MLA_WIDTH, :])
    o += _dot(s_ref[0], w_ref[NA_WIDTH + MLA_WIDTH:, :])
    o_ref[0] = h_ref[0] + gate * o


def _outproj_call(h, mod, a, b, s, w_out):
    return pl.pallas_call(
        _outproj_kernel,
        grid=(BATCH, N_TILES),
        in_specs=[
            _tok_spec(D_MODEL), _mod_spec(), _tok_spec(NA_WIDTH), _tok_spec(MLA_WIDTH),
            _tok_spec(S5_WIDTH), _full_spec((D_MODEL, D_MODEL)),
        ],
        out_specs=_tok_spec(D_MODEL),
        out_shape=jax.ShapeDtypeStruct((BATCH, S_ALL, D_MODEL), f32),
        compiler_params=pltpu.CompilerParams(
            dimension_semantics=("arbitrary", "arbitrary"), vmem_limit_bytes=VMEM_LIMIT),
        name="mixer_out_proj",
    )(h, mod, a, b, s, w_out)


def _rope_perm():
    half = MLA_ROPE // 2
    quarter = half // 2
    idx, sign = [], []
    for j in range(MLA_ROPE):
        if (j % half) < quarter:
            idx.append(j + quarter)
            sign.append(-1.0)
        else:
            idx.append(j - quarter)
            sign.append(1.0)
    return jnp.array(idx, jnp.int32), jnp.array(sign, f32)


def _rope_tables():
    quarter = MLA_ROPE // 4
    t = jnp.arange(SEQ)
    row = (t // GRID_W).astype(f32)
    col = (t % GRID_W).astype(f32)
    inv = ROPE_THETA ** (-jnp.arange(quarter, dtype=f32) / quarter)
    ang_r = row[:, None] * inv
    ang_c = col[:, None] * inv
    cos32 = jnp.concatenate([jnp.cos(ang_r), jnp.cos(ang_r), jnp.cos(ang_c), jnp.cos(ang_c)], axis=-1)
    sin32 = jnp.concatenate([jnp.sin(ang_r), jnp.sin(ang_r), jnp.sin(ang_c), jnp.sin(ang_c)], axis=-1)
    cos32 = jnp.concatenate([jnp.ones((CTX_LEN, MLA_ROPE), f32), cos32], axis=0)
    sin32 = jnp.concatenate([jnp.zeros((CTX_LEN, MLA_ROPE), f32), sin32], axis=0)
    pad = jnp.zeros((S_ALL, HEAD_PAD - MLA_QK), f32)
    cos_t = jnp.concatenate([jnp.ones((S_ALL, MLA_NOPE), f32), cos32, pad], axis=-1)
    sin_t = jnp.concatenate([jnp.zeros((S_ALL, MLA_NOPE), f32), sin32, pad], axis=-1)
    return cos_t, sin_t


def _na_bias_table(rpb):
    var = jnp.arange(WIN_ROWS)
    i = jnp.arange(WIN_ROWS)
    cq = jnp.arange(GRID_W)
    kc = jnp.arange(GRID_W)
    d_row = i[None, :] - var[:, None] + WIN_ROWS - 1
    d_col = jnp.clip(kc[None, :] - cq[:, None], -(WIN_COLS - 1), WIN_COLS - 1) + WIN_COLS - 1
    col_start = jnp.clip(cq - WIN_COLS // 2, 0, GRID_W - WIN_COLS)
    in_win = (kc[None, :] >= col_start[:, None]) & (kc[None, :] < col_start[:, None] + WIN_COLS)
    rows = rpb[:, d_row, :]
    onehot = ((d_col[:, :, None] == jnp.arange(2 * WIN_COLS - 1)[None, None, :])
              & in_win[:, :, None]).astype(f32)
    tab = jnp.einsum('hvic,qkc->hvqik', rows.astype(f32), onehot, precision=lax.Precision.HIGHEST)
    tab = tab + jnp.where(in_win, 0.0, NEG_INF)[None, None, :, None, :]
    return tab.reshape(NA_HEADS, WIN_ROWS, GRID_W, NA_WIN_KEYS).astype(f32)


def _pad_heads(w, width):
    k = w.shape[0]
    w = w.reshape(k, MLA_HEADS, width)
    w = jnp.pad(w, ((0, 0), (0, 0), (0, HEAD_PAD - width)))
    return w.reshape(k, MLA_HEADS * HEAD_PAD)


def _layer_params(l, ffn_w_gu, ffn_w_down, w_in, w_out, na_qk_g, na_rpb, mla_cq_g, mla_ckv_g, mla_w_uq,
                  mla_w_ukv, mla_qk_g, s5_lam_re, s5_lam_im, s5_log_dt, s5_b_re, s5_b_im, s5_c_re,
                  s5_c_im, s5_d, s5_w_glu, s5_b_glu):
    p = {}
    perm_idx, perm_sign = _rope_perm()
    for k in range(2):
        wgu = ffn_w_gu[l, k]
        gate_w = wgu[:, :D_FF].reshape(D_MODEL, N_FF_CHUNKS, FF_CHUNK)
        up_w = wgu[:, D_FF:].reshape(D_MODEL, N_FF_CHUNKS, FF_CHUNK)
        p[f"wgu{k}"] = jnp.concatenate([gate_w, up_w], axis=-1).transpose(1, 0, 2).astype(bf16)
        p[f"wd{k}"] = ffn_w_down[l, k].reshape(N_FF_CHUNKS, FF_CHUNK, D_MODEL).astype(bf16)
    wi = w_in[l]
    o_cq = 3 * NA_WIDTH
    o_ckv = o_cq + Q_LORA
    o_kr = o_ckv + KV_LORA
    o_u = o_kr + MLA_ROPE
    w_kr = wi[:, o_kr:o_kr + MLA_ROPE]
    w_kr_perm = w_kr[:, perm_idx] * perm_sign[None, :]
    p["win"] = jnp.concatenate([
        wi[:, :o_cq], wi[:, o_cq:o_ckv], wi[:, o_ckv:o_kr], wi[:, o_u:o_u + S5_WIDTH],
        w_kr, w_kr_perm, jnp.zeros((D_MODEL, LANES - 2 * MLA_ROPE), f32)], axis=-1).astype(bf16)
    p["naq_g"] = jnp.tile(na_qk_g[l, 0], NA_HEADS)[None, :]
    p["nak_g"] = jnp.tile(na_qk_g[l, 1], NA_HEADS)[None, :]
    p["cq_g"] = mla_cq_g[l][None, :]
    p["ckv_g"] = mla_ckv_g[l][None, :]
    wuq = mla_w_uq[l].reshape(Q_LORA, MLA_HEADS, MLA_QK)
    rope_cols = wuq[:, :, MLA_NOPE:]
    partner = jnp.concatenate([jnp.zeros((Q_LORA, MLA_HEADS, MLA_NOPE), f32),
                               rope_cols[:, :, perm_idx] * perm_sign[None, None, :]], axis=-1)
    p["wq"] = jnp.concatenate([_pad_heads(wuq.reshape(Q_LORA, -1), MLA_QK),
                               _pad_heads(partner.reshape(Q_LORA, -1), MLA_QK)], axis=-1).astype(bf16)
    wukv = mla_w_ukv[l].reshape(KV_LORA, MLA_HEADS, MLA_NOPE + MLA_V_DIM)
    p["wkv"] = jnp.concatenate([_pad_heads(wukv[:, :, :MLA_NOPE].reshape(KV_LORA, -1), MLA_NOPE),
                                wukv[:, :, MLA_NOPE:].reshape(KV_LORA, -1)], axis=-1).astype(bf16)

    def pad_gain(g):
        tail = g[MLA_NOPE:]
        z = jnp.zeros((HEAD_PAD - MLA_QK,), f32)
        full = jnp.concatenate([g, z])[None, :]
        part = jnp.concatenate([jnp.zeros((MLA_NOPE,), f32), tail[perm_idx], z])[None, :]
        return full, part

    p["gq"], p["gqp"] = pad_gain(mla_qk_g[l, 0])
    p["gk"], p["gkp"] = pad_gain(mla_qk_g[l, 1])
    p["na_bias"] = _na_bias_table(na_rpb[l])
    eye_g = jnp.eye(S5_GROUPS, dtype=f32)
    for d in range(2):
        lre = s5_lam_re[l, d].astype(f32)
        lim = s5_lam_im[l, d].astype(f32)
        dt = jnp.exp(s5_log_dt[l, d].astype(f32))[:, None]
        mag = jnp.exp(lre * dt)
        bar_re = mag * jnp.cos(lim * dt)
        bar_im = mag * jnp.sin(lim * dt)
        den = lre * lre + lim * lim
        q_re = ((bar_re - 1.0) * lre + bar_im * lim) / den
        q_im = (bar_im * lre - (bar_re - 1.0) * lim) / den
        bre = s5_b_re[l, d].astype(f32)
        bim = s5_b_im[l, d].astype(f32)
        bbar_re = q_re[..., None] * bre - q_im[..., None] * bim
        bbar_im = q_re[..., None] * bim + q_im[..., None] * bre
        b_re = (bbar_re.transpose(0, 2, 1)[:, :, None, :] * eye_g[:, None, :, None]).reshape(S5_WIDTH, S5_LANES)
        b_im = (bbar_im.transpose(0, 2, 1)[:, :, None, :] * eye_g[:, None, :, None]).reshape(S5_WIDTH, S5_LANES)
        p[f"bmat{d}"] = jnp.concatenate([b_re, b_im], axis=-1).astype(bf16)
        c_re = (s5_c_re[l, d].astype(f32).transpose(0, 2, 1)[:, :, None, :]
                * eye_g[:, None, :, None]).reshape(S5_LANES, S5_WIDTH)
        c_im = (s5_c_im[l, d].astype(f32).transpose(0, 2, 1)[:, :, None, :]
                * eye_g[:, None, :, None]).reshape(S5_LANES, S5_WIDTH)
        p[f"cmat{d}"] = jnp.concatenate([c_re, -c_im], axis=0).astype(bf16)
        lam_row = jnp.concatenate([bar_re.reshape(-1), bar_im.reshape(-1)])
        p[f"lam{d}"] = jnp.broadcast_to(lam_row[None, :], (BATCH, 2 * S5_LANES)).astype(f32)
    p["s5_d"] = s5_d[l][None, :]
    p["w_glu"] = s5_w_glu[l].astype(bf16)
    p["b_glu"] = s5_b_glu[l][None, :]
    p["w_out"] = w_out[l].astype(bf16)
    return p


def kernel(x, c, ctx, c_ctx, w_mod, b_mod, norm_g, ffn_w_gu, ffn_w_down, w_in, w_out, na_qk_g, na_rpb, mla_cq_g, mla_ckv_g, mla_w_uq, mla_w_ukv, mla_qk_g, s5_lam_re, s5_lam_im, s5_log_dt, s5_b_re, s5_b_im, s5_c_re, s5_c_im, s5_d, s5_w_glu, s5_b_glu):
    assert x.shape == (BATCH, SEQ, D_MODEL) and ctx.shape == (BATCH, CTX_LEN, D_MODEL)
    mod_rows = 16
    cvec = jnp.concatenate([c, c_ctx[None, :], jnp.zeros((mod_rows - BATCH - 1, D_MODEL), f32)], axis=0)
    mod_all = _mod_call(cvec, w_mod, b_mod).reshape(DEPTH, mod_rows, N_MOD, D_MODEL)
    cos_t, sin_t = _rope_tables()
    h = jnp.concatenate([ctx, x], axis=1)
    for l in range(DEPTH):
        p = _layer_params(l, ffn_w_gu, ffn_w_down, w_in, w_out, na_qk_g, na_rpb, mla_cq_g, mla_ckv_g,
                          mla_w_uq, mla_w_ukv, mla_qk_g, s5_lam_re, s5_lam_im, s5_log_dt, s5_b_re,
                          s5_b_im, s5_c_re, s5_c_im, s5_d, s5_w_glu, s5_b_glu)
        mod_c = jnp.broadcast_to(mod_all[l, BATCH][None], (BATCH, N_MOD, D_MODEL))
        mod = jnp.stack([mod_c, mod_all[l, :BATCH]], axis=1)
        h = _ffn_call(h, mod, norm_g[l, 0][None, :], p["wgu0"], p["wd0"], 0)
        qn, kn, vn, qm, km, vm, u = _inproj_call(
            h, mod, norm_g[l, 1][None, :], p["win"], p["naq_g"], p["nak_g"], p["cq_g"], p["ckv_g"],
            p["wq"], p["wkv"], p["gq"], p["gqp"], p["gk"], p["gkp"], cos_t, sin_t)
        a = _na_call(qn, kn, vn, p["na_bias"])
        bm = _mla_call(qm, km, vm)
        u_t = u.transpose(1, 0, 2).reshape(S_ALL * BATCH, S5_WIDTH)
        yf = _s5_scan_call(u_t, p["bmat0"], p["cmat0"], p["lam0"], reverse=False)
        yb = _s5_scan_call(u_t, p["bmat1"], p["cmat1"], p["lam1"], reverse=True)
        s_t = _s5_out_call(yf, yb, u_t, p["s5_d"], p["w_glu"], p["b_glu"])
        s = s_t.reshape(S_ALL, BATCH, S5_WIDTH).transpose(1, 0, 2)
        h = _outproj_call(h, mod, a, bm, s, p["w_out"])
        h = _ffn_call(h, mod, norm_g[l, 2][None, :], p["wgu1"], p["wd1"], 6)
    return h[:, CTX_LEN:, :]
```

```python
import functools
import math

import jax
import jax.numpy as jnp
from jax import lax
from jax.experimental import pallas as pl
from jax.experimental.pallas import tpu as pltpu

D_MODEL = 1024
BATCH = 8
SEQ = 2048
DEPTH = 2
CTX_LEN = 256
S_ALL = CTX_LEN + SEQ
GRID_W = 64
GRID_ROWS = SEQ // GRID_W
NA_HEAD_DIM = 64
NA_WIDTH = 384
NA_HEADS = 6
WIN_ROWS = 8
WIN_COLS = 16
MLA_V_DIM = 64
MLA_WIDTH = 384
MLA_HEADS = 6
MLA_NOPE = 64
MLA_ROPE = 32
MLA_QK = 96
Q_LORA = 384
KV_LORA = 256
S5_WIDTH = 256
S5_GROUP = 16
S5_GROUPS = 16
S5_STATE = 64
S5_LANES = S5_GROUPS * S5_STATE
D_FF = 2816
ROPE_THETA = 10000.0
EPS = 1e-6
N_MOD = 9
NEG_INF = -1e30

LANES = 128
TM = 256
N_TILES = S_ALL // TM
FF_CHUNK = 1408
N_FF_CHUNKS = D_FF // FF_CHUNK
HEAD_PAD = 128
S5_T = 128
S5_ROWS = S5_T * BATCH
S5_CHUNKS = S_ALL // S5_T
S5_CTX_CHUNKS = CTX_LEN // S5_T
VMEM_LIMIT = 56 * 1024 * 1024

ZC_QA, ZC_KA, ZC_VA, ZC_CQ, ZC_CKV, ZC_U, ZC_KR = 0, 384, 768, 1152, 1536, 1792, 2048
Z_COLS = 2176

LOG2E = math.log2(math.e)
NA_Q_SCALE = NA_HEAD_DIM ** -0.5 * LOG2E
MLA_Q_SCALE = MLA_QK ** -0.5 * LOG2E

f32 = jnp.float32
bf16 = jnp.bfloat16


def _dot(a, b):
    return jnp.dot(a, b, preferred_element_type=f32)


def _sigmoid(x):
    return 1.0 / (1.0 + jnp.exp(-x))


def _rms(x, n):
    return lax.rsqrt(jnp.sum(x * x, axis=-1, keepdims=True) / n + EPS)


def _modulated(x, g, shift, scale):
    y = x * _rms(x, D_MODEL)
    return (y * g) * (1.0 + scale) + shift


def _mod_kernel(c_ref, w_ref, b_ref, o_ref):
    c = c_ref[...]
    a = (c * _sigmoid(c)).astype(bf16)
    o_ref[0] = _dot(a, w_ref[0].astype(bf16)) + b_ref[0]


def _mod_call(cvec, w_mod, b_mod):
    rows = cvec.shape[0]
    return pl.pallas_call(
        _mod_kernel,
        grid=(DEPTH, N_MOD),
        in_specs=[
            pl.BlockSpec((rows, D_MODEL), lambda l, j: (0, 0)),
            pl.BlockSpec((1, D_MODEL, D_MODEL), lambda l, j: (l, 0, j)),
            pl.BlockSpec((1, 1, D_MODEL), lambda l, j: (l, 0, j)),
        ],
        out_specs=pl.BlockSpec((1, rows, D_MODEL), lambda l, j: (l, 0, j)),
        out_shape=jax.ShapeDtypeStruct((DEPTH, rows, N_MOD * D_MODEL), f32),
        compiler_params=pltpu.CompilerParams(
            dimension_semantics=("arbitrary", "arbitrary"), vmem_limit_bytes=VMEM_LIMIT),
        name="adaln_mod",
    )(cvec, w_mod, b_mod.reshape(DEPTH, 1, N_MOD * D_MODEL))


def _mod_spec(first=0):
    return pl.BlockSpec((1, 1, N_MOD, D_MODEL), lambda b, j: (b, jnp.minimum(j + first, 1), 0, 0))


def _tok_spec(width, first=0):
    return pl.BlockSpec((1, TM, width), lambda b, j: (b, j + first, 0))


def _tok_spec_t(width):
    return pl.BlockSpec((1, width, TM), lambda b, j: (b, 0, j))


def _full_spec(shape):
    zeros = (0,) * len(shape)
    return pl.BlockSpec(shape, lambda b, j: zeros)


def _ffn_kernel(h_ref, mod_ref, g_ref, wgu_ref, wd_ref, o_ref, xm_ref, acc_ref, *, i0):
    x = h_ref[0]
    shift = mod_ref[0, 0, i0:i0 + 1, :]
    scale = mod_ref[0, 0, i0 + 1:i0 + 2, :]
    gate = mod_ref[0, 0, i0 + 2:i0 + 3, :]
    xm_ref[...] = _modulated(x, g_ref[...], shift, scale).astype(bf16)
    acc_ref[...] = jnp.zeros_like(acc_ref)

    def body(c, carry):
        gu = _dot(xm_ref[...], wgu_ref[c])
        gt = gu[:, :FF_CHUNK]
        up = gu[:, FF_CHUNK:]
        a = ((gt * _sigmoid(gt)) * up).astype(bf16)
        acc_ref[...] += _dot(a, wd_ref[c])
        return carry

    lax.fori_loop(0, N_FF_CHUNKS, body, 0)
    o_ref[0] = x + (0.5 * gate) * acc_ref[...]


def _ffn_call(h, mod, g, wgu, wd, i0):
    n_tiles = h.shape[1] // TM
    first = N_TILES - n_tiles
    return pl.pallas_call(
        functools.partial(_ffn_kernel, i0=i0),
        grid=(BATCH, n_tiles),
        in_specs=[
            _tok_spec(D_MODEL),
            _mod_spec(first),
            _full_spec((1, D_MODEL)),
            _full_spec((N_FF_CHUNKS, D_MODEL, 2 * FF_CHUNK)),
            _full_spec((N_FF_CHUNKS, FF_CHUNK, D_MODEL)),
        ],
        out_specs=_tok_spec(D_MODEL),
        out_shape=jax.ShapeDtypeStruct(h.shape, f32),
        scratch_shapes=[pltpu.VMEM((TM, D_MODEL), bf16), pltpu.VMEM((TM, D_MODEL), f32)],
        compiler_params=pltpu.CompilerParams(
            dimension_semantics=("arbitrary", "arbitrary"), vmem_limit_bytes=VMEM_LIMIT),
        name="swiglu_half_step",
    )(h, mod, g, wgu, wd)


def _pair_head_norm(x, g):
    lane = lax.broadcasted_iota(jnp.int32, (1, LANES), 1)
    lo = lane < NA_HEAD_DIM
    outs = []
    for p in range(NA_WIDTH // LANES):
        xb = x[:, p * LANES:(p + 1) * LANES]
        sq = xb * xb
        s_lo = jnp.sum(jnp.where(lo, sq, 0.0), axis=-1, keepdims=True)
        s_hi = jnp.sum(jnp.where(lo, 0.0, sq), axis=-1, keepdims=True)
        r = jnp.where(lo, lax.rsqrt(s_lo / NA_HEAD_DIM + EPS), lax.rsqrt(s_hi / NA_HEAD_DIM + EPS))
        outs.append((xb * r) * g[:, p * LANES:(p + 1) * LANES])
    return outs


def _inproj_kernel(h_ref, mod_ref, g_ref, win_ref, naq_g_ref, nak_g_ref, cq_g_ref, ckv_g_ref,
                   wq_ref, wkv_ref, gq_ref, gqp_ref, gk_ref, gkp_ref, cos_ref, sin_ref,
                   qn_ref, kn_ref, vn_ref, qm_ref, km_ref, vm_ref, u_ref):
    x = h_ref[0]
    shift = mod_ref[0, 0, 3:4, :]
    scale = mod_ref[0, 0, 4:5, :]
    xm = _modulated(x, g_ref[...], shift, scale).astype(bf16)
    z = _dot(xm, win_ref[...])

    qn = _pair_head_norm(z[:, ZC_QA:ZC_QA + NA_WIDTH], naq_g_ref[...])
    kn = _pair_head_norm(z[:, ZC_KA:ZC_KA + NA_WIDTH], nak_g_ref[...])
    for p in range(NA_WIDTH // LANES):
        qn_ref[0, p * LANES:(p + 1) * LANES, :] = (qn[p] * NA_Q_SCALE).T.astype(bf16)
        kn_ref[0, :, p * LANES:(p + 1) * LANES] = kn[p].astype(bf16)
        vn_ref[0, p * LANES:(p + 1) * LANES, :] = z[:, ZC_VA + p * LANES:ZC_VA + (p + 1) * LANES].T.astype(bf16)
    u_ref[0] = z[:, ZC_U:ZC_U + S5_WIDTH]

    cos_t = cos_ref[...]
    sin_t = sin_ref[...]

    cq = z[:, ZC_CQ:ZC_CQ + Q_LORA]
    ncq = ((cq * _rms(cq, Q_LORA)) * cq_g_ref[...]).astype(bf16)
    qq = _dot(ncq, wq_ref[...])
    gq = gq_ref[...]
    gqp = gqp_ref[...]
    for hd in range(MLA_HEADS):
        pre = qq[:, hd * HEAD_PAD:(hd + 1) * HEAD_PAD]
        perm = qq[:, (MLA_HEADS + hd) * HEAD_PAD:(MLA_HEADS + hd + 1) * HEAD_PAD]
        r = _rms(pre, MLA_QK)
        qm_ref[0, hd * HEAD_PAD:(hd + 1) * HEAD_PAD, :] = (
            ((pre * gq) * cos_t + (perm * gqp) * sin_t) * (r * MLA_Q_SCALE)).T.astype(bf16)

    ckv = z[:, ZC_CKV:ZC_CKV + KV_LORA]
    nkv = ((ckv * _rms(ckv, KV_LORA)) * ckv_g_ref[...]).astype(bf16)
    kv = _dot(nkv, wkv_ref[...])
    for p in range(MLA_WIDTH // LANES):
        lo_col = MLA_HEADS * HEAD_PAD + p * LANES
        vm_ref[0, p * LANES:(p + 1) * LANES, :] = kv[:, lo_col:lo_col + LANES].T.astype(bf16)
    krb = z[:, ZC_KR:ZC_KR + LANES]
    lane = lax.broadcasted_iota(jnp.int32, (1, LANES), 1)
    rope_lanes = (lane >= MLA_NOPE) & (lane < MLA_QK)
    kr_a = jnp.where(rope_lanes, pltpu.roll(krb, MLA_NOPE, axis=1), 0.0)
    kr_b = jnp.where(rope_lanes, pltpu.roll(krb, MLA_ROPE, axis=1), 0.0)
    gk = gk_ref[...]
    gkp = gkp_ref[...]
    rot_part = (kr_b * gkp) * sin_t
    for hd in range(MLA_HEADS):
        kfull = kv[:, hd * HEAD_PAD:(hd + 1) * HEAD_PAD] + kr_a
        r = _rms(kfull, MLA_QK)
        km_ref[0, :, hd * HEAD_PAD:(hd + 1) * HEAD_PAD] = (
            ((kfull * gk) * cos_t + rot_part) * r).astype(bf16)


def _inproj_call(h, mod, g, win, naq_g, nak_g, cq_g, ckv_g, wq, wkv, gq, gqp, gk, gkp, cos_t, sin_t):
    qk_w = MLA_HEADS * HEAD_PAD
    tab_spec = pl.BlockSpec((TM, LANES), lambda b, j: (j, 0))
    return pl.pallas_call(
        _inproj_kernel,
        grid=(BATCH, N_TILES),
        in_specs=[
            _tok_spec(D_MODEL), _mod_spec(), _full_spec((1, D_MODEL)),
            _full_spec((D_MODEL, Z_COLS)),
            _full_spec((1, NA_WIDTH)), _full_spec((1, NA_WIDTH)),
            _full_spec((1, Q_LORA)), _full_spec((1, KV_LORA)),
            _full_spec((Q_LORA, 2 * qk_w)), _full_spec((KV_LORA, qk_w + MLA_WIDTH)),
            _full_spec((1, LANES)), _full_spec((1, LANES)),
            _full_spec((1, LANES)), _full_spec((1, LANES)),
            tab_spec, tab_spec,
        ],
        out_specs=[
            _tok_spec_t(NA_WIDTH), _tok_spec(NA_WIDTH), _tok_spec_t(NA_WIDTH),
            _tok_spec_t(qk_w), _tok_spec(qk_w), _tok_spec_t(MLA_WIDTH), _tok_spec(S5_WIDTH),
        ],
        out_shape=[
            jax.ShapeDtypeStruct((BATCH, NA_WIDTH, S_ALL), bf16),
            jax.ShapeDtypeStruct((BATCH, S_ALL, NA_WIDTH), bf16),
            jax.ShapeDtypeStruct((BATCH, NA_WIDTH, S_ALL), bf16),
            jax.ShapeDtypeStruct((BATCH, qk_w, S_ALL), bf16),
            jax.ShapeDtypeStruct((BATCH, S_ALL, qk_w), bf16),
            jax.ShapeDtypeStruct((BATCH, MLA_WIDTH, S_ALL), bf16),
            jax.ShapeDtypeStruct((BATCH, S_ALL, S5_WIDTH), f32),
        ],
        compiler_params=pltpu.CompilerParams(
            dimension_semantics=("arbitrary", "arbitrary"), vmem_limit_bytes=VMEM_LIMIT),
        name="in_proj_heads",
    )(h, mod, g, win, naq_g, nak_g, cq_g, ckv_g, wq, wkv, gq, gqp, gk, gkp, cos_t, sin_t)


def _pair_block_diag(qt):
    d = qt.shape[0] // 2
    z = jnp.zeros((d, qt.shape[1]), qt.dtype)
    return jnp.concatenate([jnp.concatenate([qt[:d], z], axis=0),
                            jnp.concatenate([z, qt[d:]], axis=0)], axis=1)


def _attend_pair(key_blocks, vt_blocks, q_bd, bias_blocks):
    ex = jnp.exp2

    def scores(i):
        s = _dot(key_blocks[i], q_bd)
        return s if bias_blocks[i] is None else s + bias_blocks[i]

    m = den = acc = None
    s_next = scores(0)
    for i, vt in enumerate(vt_blocks):
        s = s_next
        if i + 1 < len(vt_blocks):
            s_next = scores(i + 1)
        mi = jnp.max(s, axis=0, keepdims=True)
        m_new = mi if m is None else jnp.maximum(m, mi)
        p = ex(s - m_new)
        li = jnp.sum(p, axis=0, keepdims=True)
        oi = _dot(vt, p.astype(bf16))
        if m is None:
            den, acc = li, oi
        else:
            alpha = ex(m - m_new)
            den = den * alpha + li
            acc = acc * alpha + oi
        m = m_new
    o = acc / den
    dv = o.shape[0] // 2
    return jnp.concatenate([o[:dv, :TM], o[dv:, TM:]], axis=0)


def _store_t(o_ref, ot):
    o_ref[0] = ot.T.astype(bf16)


NA_TILE_ROWS = TM // GRID_W
NA_UNION_ROWS = 12
NA_UNION_KEYS = NA_UNION_ROWS * GRID_W
NA_PATTERNS = 3
NA_KEY_BLOCK = 768


def _na_kernel(qt_ref, k_ref, vt_ref, bias_ref, o_ref, *, first_tile):
    j = pl.program_id(2) + first_tile
    kc = k_ref[0, 0:CTX_LEN, :]
    vtc = vt_ref[0, :, 0:CTX_LEN]
    q_bd = _pair_block_diag(qt_ref[0])

    @pl.when(j == 0)
    def _():
        _store_t(o_ref, _attend_pair([kc], [vtc], q_bd, [None]))

    @pl.when(j > 0)
    def _():
        r0 = (j - 1) * NA_TILE_ROWS
        base = jnp.clip(r0 - WIN_ROWS // 2, 0, GRID_ROWS - NA_UNION_ROWS)
        pat = (r0 - base) // NA_TILE_ROWS
        start = pl.multiple_of(CTX_LEN + base * GRID_W, NA_TILE_ROWS * GRID_W)
        keys, vts, biases = [kc], [vtc], [None]
        for s0 in range(0, NA_UNION_KEYS, NA_KEY_BLOCK):
            blk = pl.ds(pl.multiple_of(start + s0, LANES), NA_KEY_BLOCK)
            keys.append(k_ref[0, blk, :])
            vts.append(vt_ref[0, :, blk])
            biases.append(bias_ref[0, pat, s0:s0 + NA_KEY_BLOCK, :])
        _store_t(o_ref, _attend_pair(keys, vts, q_bd, biases))


def _na_call(qnt, kn, vnt, bias, with_ctx):
    n_pairs = NA_WIDTH // LANES
    first = 0 if with_ctx else 1
    return pl.pallas_call(
        functools.partial(_na_kernel, first_tile=first),
        grid=(n_pairs, BATCH, N_TILES - first),
        in_specs=[
            pl.BlockSpec((1, LANES, TM), lambda p, b, j: (b, p, j + first)),
            pl.BlockSpec((1, S_ALL, LANES), lambda p, b, j: (b, 0, p)),
            pl.BlockSpec((1, LANES, S_ALL), lambda p, b, j: (b, p, 0)),
            pl.BlockSpec((1, NA_PATTERNS, NA_UNION_KEYS, 2 * TM), lambda p, b, j: (p, 0, 0, 0)),
        ],
        out_specs=pl.BlockSpec((1, TM, LANES), lambda p, b, j: (b, j, p)),
        out_shape=jax.ShapeDtypeStruct((BATCH, S_ALL - first * TM, NA_WIDTH), bf16),
        compiler_params=pltpu.CompilerParams(
            dimension_semantics=("arbitrary", "arbitrary", "arbitrary"), vmem_limit_bytes=VMEM_LIMIT),
        name="na_attention",
    )(qnt, kn, vnt, bias)


MLA_KEY_BLOCK = 1152


def _mla_kernel(qt_ref, k_ref, vt_ref, o_ref, *, first_tile):
    j = pl.program_id(2) + first_tile

    q_bd = _pair_block_diag(qt_ref[0])

    def attend(n_keys, block):
        keys = [k_ref[0, s0:s0 + block, :] for s0 in range(0, n_keys, block)]
        vts = [vt_ref[0, :, s0:s0 + block] for s0 in range(0, n_keys, block)]
        _store_t(o_ref, _attend_pair(keys, vts, q_bd, [None] * len(keys)))

    @pl.when(j == 0)
    def _():
        attend(CTX_LEN, CTX_LEN)

    @pl.when(j > 0)
    def _():
        attend(S_ALL, MLA_KEY_BLOCK)


def _mla_call(qmt, km, vmt, with_ctx):
    n_pairs = MLA_HEADS // 2
    first = 0 if with_ctx else 1
    return pl.pallas_call(
        functools.partial(_mla_kernel, first_tile=first),
        grid=(BATCH, n_pairs, N_TILES - first),
        in_specs=[
            pl.BlockSpec((1, 2 * HEAD_PAD, TM), lambda b, p, j: (b, p, j + first)),
            pl.BlockSpec((1, S_ALL, 2 * HEAD_PAD), lambda b, p, j: (b, 0, p)),
            pl.BlockSpec((1, LANES, S_ALL), lambda b, p, j: (b, p, 0)),
        ],
        out_specs=pl.BlockSpec((1, TM, LANES), lambda b, p, j: (b, j, p)),
        out_shape=jax.ShapeDtypeStruct((BATCH, S_ALL - first * TM, MLA_WIDTH), bf16),
        compiler_params=pltpu.CompilerParams(
            dimension_semantics=("arbitrary", "arbitrary", "arbitrary"), vmem_limit_bytes=VMEM_LIMIT),
        name="mla_attention",
    )(qmt, km, vmt)


def _s5_scan_kernel(u_ref, bmat_ref, cmat_ref, lam_ref, y_ref, h_ref, st_ref, *, reverse):
    i = pl.program_id(0)

    @pl.when(i == 0)
    def _():
        st_ref[...] = jnp.zeros_like(st_ref)

    h_ref[...] = _dot(u_ref[...].astype(bf16), bmat_ref[...])

    def step(t, carry):
        hr, hi = carry
        row = (S5_T - 1 - t) if reverse else t
        off = pl.multiple_of(row * BATCH, BATCH)
        lr = lam_ref[:, 0:S5_LANES]
        li = lam_ref[:, S5_LANES:]
        nr = (lr * hr - li * hi) + h_ref[pl.ds(off, BATCH), 0:S5_LANES]
        ni = (lr * hi + li * hr) + h_ref[pl.ds(off, BATCH), S5_LANES:]
        h_ref[pl.ds(off, BATCH), 0:S5_LANES] = nr
        h_ref[pl.ds(off, BATCH), S5_LANES:] = ni
        return nr, ni

    hr, hi = lax.fori_loop(0, S5_T, step, (st_ref[:, 0:S5_LANES], st_ref[:, S5_LANES:]))
    st_ref[:, 0:S5_LANES] = hr
    st_ref[:, S5_LANES:] = hi

    blk = 256
    for rb in range(S5_ROWS // blk):
        y_ref[rb * blk:(rb + 1) * blk, :] = _dot(
            h_ref[rb * blk:(rb + 1) * blk, :].astype(bf16), cmat_ref[...])


def _s5_scan_call(u_t, bmat, cmat, lam, reverse):
    if reverse:
        def chunk(i):
            return jnp.where(i < S5_CTX_CHUNKS, S5_CTX_CHUNKS - 1 - i, S5_CHUNKS + S5_CTX_CHUNKS - 1 - i)
    else:
        def chunk(i):
            return i
    return pl.pallas_call(
        functools.partial(_s5_scan_kernel, reverse=reverse),
        grid=(S5_CHUNKS,),
        in_specs=[
            pl.BlockSpec((S5_ROWS, S5_WIDTH), lambda i: (chunk(i), 0)),
            pl.BlockSpec((S5_WIDTH, 2 * S5_LANES), lambda i: (0, 0)),
            pl.BlockSpec((2 * S5_LANES, S5_WIDTH), lambda i: (0, 0)),
            pl.BlockSpec((BATCH, 2 * S5_LANES), lambda i: (0, 0)),
        ],
        out_specs=pl.BlockSpec((S5_ROWS, S5_WIDTH), lambda i: (chunk(i), 0)),
        out_shape=jax.ShapeDtypeStruct((S_ALL * BATCH, S5_WIDTH), f32),
        scratch_shapes=[pltpu.VMEM((S5_ROWS, 2 * S5_LANES), f32), pltpu.VMEM((BATCH, 2 * S5_LANES), f32)],
        compiler_params=pltpu.CompilerParams(
            dimension_semantics=("arbitrary",), vmem_limit_bytes=VMEM_LIMIT),
        name="s5_scan_bwd" if reverse else "s5_scan_fwd",
    )(u_t, bmat, cmat, lam)


def _s5_out_kernel(yf_ref, yb_ref, u_ref, d_ref, w_ref, b_ref, o_ref):
    y = (yf_ref[...] + yb_ref[...]) + d_ref[...] * u_ref[...]
    c0 = math.sqrt(2.0 / math.pi)
    gl = 0.5 * y * (1.0 + jnp.tanh(c0 * (y + 0.044715 * (y * y * y))))
    o = _dot(gl.astype(bf16), w_ref[...]) + b_ref[...]
    o_ref[...] = (o[:, :S5_WIDTH] * _sigmoid(o[:, S5_WIDTH:])).astype(bf16)


def _s5_out_call(yf, yb, u_t, d, w_glu, b_glu, with_ctx):
    rows = 1024
    first = 0 if with_ctx else CTX_LEN * BATCH // rows
    in_spec = pl.BlockSpec((rows, S5_WIDTH), lambda i: (i + first, 0))
    row_spec = pl.BlockSpec((rows, S5_WIDTH), lambda i: (i, 0))
    return pl.pallas_call(
        _s5_out_kernel,
        grid=(S_ALL * BATCH // rows - first,),
        in_specs=[
            in_spec, in_spec, in_spec,
            pl.BlockSpec((1, S5_WIDTH), lambda i: (0, 0)),
            pl.BlockSpec((S5_WIDTH, 2 * S5_WIDTH), lambda i: (0, 0)),
            pl.BlockSpec((1, 2 * S5_WIDTH), lambda i: (0, 0)),
        ],
        out_specs=row_spec,
        out_shape=jax.ShapeDtypeStruct((S_ALL * BATCH - first * rows, S5_WIDTH), bf16),
        compiler_params=pltpu.CompilerParams(
            dimension_semantics=("arbitrary",), vmem_limit_bytes=VMEM_LIMIT),
        name="s5_readout",
    )(yf, yb, u_t, d, w_glu, b_glu)


def _outproj_kernel(h_ref, mod_ref, a_ref, b_ref, s_ref, w_ref, o_ref):
    gate = mod_ref[0, 0, 5:6, :]
    o = _dot(a_ref[0], w_ref[0:NA_WIDTH, :])
    o += _dot(b_ref[0], w_ref[NA_WIDTH:NA_WIDTH + MLA_WIDTH, :])
    o += _dot(s_ref[0], w_ref[NA_WIDTH + MLA_WIDTH:, :])
    o_ref[0] = h_ref[0] + gate * o


def _outproj_call(h, mod, a, b, s, w_out, with_ctx):
    first = 0 if with_ctx else 1
    return pl.pallas_call(
        _outproj_kernel,
        grid=(BATCH, N_TILES - first),
        in_specs=[
            _tok_spec(D_MODEL, first), _mod_spec(first), _tok_spec(NA_WIDTH), _tok_spec(MLA_WIDTH),
            _tok_spec(S5_WIDTH), _full_spec((D_MODEL, D_MODEL)),
        ],
        out_specs=_tok_spec(D_MODEL),
        out_shape=jax.ShapeDtypeStruct((BATCH, S_ALL - first * TM, D_MODEL), f32),
        compiler_params=pltpu.CompilerParams(
            dimension_semantics=("arbitrary", "arbitrary"), vmem_limit_bytes=VMEM_LIMIT),
        name="mixer_out_proj",
    )(h, mod, a, b, s, w_out)


def _rope_perm():
    half = MLA_ROPE // 2
    quarter = half // 2
    idx, sign = [], []
    for j in range(MLA_ROPE):
        if (j % half) < quarter:
            idx.append(j + quarter)
            sign.append(-1.0)
        else:
            idx.append(j - quarter)
            sign.append(1.0)
    return jnp.array(idx, jnp.int32), jnp.array(sign, f32)


def _rope_tables():
    quarter = MLA_ROPE // 4
    t = jnp.arange(SEQ)
    row = (t // GRID_W).astype(f32)
    col = (t % GRID_W).astype(f32)
    inv = ROPE_THETA ** (-jnp.arange(quarter, dtype=f32) / quarter)
    ang_r = row[:, None] * inv
    ang_c = col[:, None] * inv
    cos32 = jnp.concatenate([jnp.cos(ang_r), jnp.cos(ang_r), jnp.cos(ang_c), jnp.cos(ang_c)], axis=-1)
    sin32 = jnp.concatenate([jnp.sin(ang_r), jnp.sin(ang_r), jnp.sin(ang_c), jnp.sin(ang_c)], axis=-1)
    cos32 = jnp.concatenate([jnp.ones((CTX_LEN, MLA_ROPE), f32), cos32], axis=0)
    sin32 = jnp.concatenate([jnp.zeros((CTX_LEN, MLA_ROPE), f32), sin32], axis=0)
    pad = jnp.zeros((S_ALL, HEAD_PAD - MLA_QK), f32)
    cos_t = jnp.concatenate([jnp.ones((S_ALL, MLA_NOPE), f32), cos32, pad], axis=-1)
    sin_t = jnp.concatenate([jnp.zeros((S_ALL, MLA_NOPE), f32), sin32, pad], axis=-1)
    return cos_t, sin_t


def _na_bias_table(rpb):
    pat = jnp.arange(NA_PATTERNS)[:, None, None]
    ii = jnp.arange(NA_UNION_ROWS)[None, :, None]
    rr = jnp.arange(NA_TILE_ROWS)[None, None, :]
    win_lo = jnp.where(pat == 0, 0, jnp.where(pat == 1, rr, WIN_ROWS // 2))
    row_ok = (ii >= win_lo) & (ii < win_lo + WIN_ROWS)
    d_row = jnp.clip(ii - (NA_TILE_ROWS * pat + rr) + WIN_ROWS - 1, 0, 2 * WIN_ROWS - 2)
    cq = jnp.arange(GRID_W)
    kc = jnp.arange(GRID_W)
    d_col = jnp.clip(kc[None, :] - cq[:, None], -(WIN_COLS - 1), WIN_COLS - 1) + WIN_COLS - 1
    col_start = jnp.clip(cq - WIN_COLS // 2, 0, GRID_W - WIN_COLS)
    in_win = (kc[None, :] >= col_start[:, None]) & (kc[None, :] < col_start[:, None] + WIN_COLS)
    rows = rpb[:, d_row, :].astype(f32)
    onehot = (d_col[:, :, None] == jnp.arange(2 * WIN_COLS - 1)[None, None, :]).astype(f32)
    tab = jnp.einsum('hpirc,qkc->hpikrq', rows, onehot, precision=lax.Precision.HIGHEST)
    ok = row_ok[:, :, None, :, None] & in_win.T[None, None, :, None, :]
    tab = jnp.where(ok[None], tab * LOG2E, NEG_INF).reshape(NA_HEADS // 2, 2, NA_PATTERNS, NA_UNION_KEYS, TM)
    return tab.transpose(0, 2, 3, 1, 4).reshape(NA_HEADS // 2, NA_PATTERNS, NA_UNION_KEYS, 2 * TM)


def _pad_heads(w, width):
    k = w.shape[0]
    w = w.reshape(k, MLA_HEADS, width)
    w = jnp.pad(w, ((0, 0), (0, 0), (0, HEAD_PAD - width)))
    return w.reshape(k, MLA_HEADS * HEAD_PAD)


def _layer_params(l, ffn_w_gu, ffn_w_down, w_in, w_out, na_qk_g, na_rpb, mla_cq_g, mla_ckv_g, mla_w_uq,
                  mla_w_ukv, mla_qk_g, s5_lam_re, s5_lam_im, s5_log_dt, s5_b_re, s5_b_im, s5_c_re,
                  s5_c_im, s5_d, s5_w_glu, s5_b_glu):
    p = {}
    perm_idx, perm_sign = _rope_perm()
    for k in range(2):
        wgu = ffn_w_gu[l, k]
        gate_w = wgu[:, :D_FF].reshape(D_MODEL, N_FF_CHUNKS, FF_CHUNK)
        up_w = wgu[:, D_FF:].reshape(D_MODEL, N_FF_CHUNKS, FF_CHUNK)
        p[f"wgu{k}"] = jnp.concatenate([gate_w, up_w], axis=-1).transpose(1, 0, 2).astype(bf16)
        p[f"wd{k}"] = ffn_w_down[l, k].reshape(N_FF_CHUNKS, FF_CHUNK, D_MODEL).astype(bf16)
    wi = w_in[l]
    o_cq = 3 * NA_WIDTH
    o_ckv = o_cq + Q_LORA
    o_kr = o_ckv + KV_LORA
    o_u = o_kr + MLA_ROPE
    w_kr = wi[:, o_kr:o_kr + MLA_ROPE]
    w_kr_perm = w_kr[:, perm_idx] * perm_sign[None, :]
    p["win"] = jnp.concatenate([
        wi[:, :o_cq], wi[:, o_cq:o_ckv], wi[:, o_ckv:o_kr], wi[:, o_u:o_u + S5_WIDTH],
        w_kr, w_kr_perm, jnp.zeros((D_MODEL, LANES - 2 * MLA_ROPE), f32)], axis=-1).astype(bf16)
    p["naq_g"] = jnp.tile(na_qk_g[l, 0], NA_HEADS)[None, :]
    p["nak_g"] = jnp.tile(na_qk_g[l, 1], NA_HEADS)[None, :]
    p["cq_g"] = mla_cq_g[l][None, :]
    p["ckv_g"] = mla_ckv_g[l][None, :]
    wuq = mla_w_uq[l].reshape(Q_LORA, MLA_HEADS, MLA_QK)
    rope_cols = wuq[:, :, MLA_NOPE:]
    partner = jnp.concatenate([jnp.zeros((Q_LORA, MLA_HEADS, MLA_NOPE), f32),
                               rope_cols[:, :, perm_idx] * perm_sign[None, None, :]], axis=-1)
    p["wq"] = jnp.concatenate([_pad_heads(wuq.reshape(Q_LORA, -1), MLA_QK),
                               _pad_heads(partner.reshape(Q_LORA, -1), MLA_QK)], axis=-1).astype(bf16)
    wukv = mla_w_ukv[l].reshape(KV_LORA, MLA_HEADS, MLA_NOPE + MLA_V_DIM)
    p["wkv"] = jnp.concatenate([_pad_heads(wukv[:, :, :MLA_NOPE].reshape(KV_LORA, -1), MLA_NOPE),
                                wukv[:, :, MLA_NOPE:].reshape(KV_LORA, -1)], axis=-1).astype(bf16)

    def pad_gain(g):
        tail = g[MLA_NOPE:]
        z = jnp.zeros((HEAD_PAD - MLA_QK,), f32)
        full = jnp.concatenate([g, z])[None, :]
        part = jnp.concatenate([jnp.zeros((MLA_NOPE,), f32), tail[perm_idx], z])[None, :]
        return full, part

    p["gq"], p["gqp"] = pad_gain(mla_qk_g[l, 0])
    p["gk"], p["gkp"] = pad_gain(mla_qk_g[l, 1])
    p["na_bias"] = _na_bias_table(na_rpb[l])
    eye_g = jnp.eye(S5_GROUPS, dtype=f32)
    for d in range(2):
        lre = s5_lam_re[l, d].astype(f32)
        lim = s5_lam_im[l, d].astype(f32)
        dt = jnp.exp(s5_log_dt[l, d].astype(f32))[:, None]
        mag = jnp.exp(lre * dt)
        bar_re = mag * jnp.cos(lim * dt)
        bar_im = mag * jnp.sin(lim * dt)
        den = lre * lre + lim * lim
        q_re = ((bar_re - 1.0) * lre + bar_im * lim) / den
        q_im = (bar_im * lre - (bar_re - 1.0) * lim) / den
        bre = s5_b_re[l, d].astype(f32)
        bim = s5_b_im[l, d].astype(f32)
        bbar_re = q_re[..., None] * bre - q_im[..., None] * bim
        bbar_im = q_re[..., None] * bim + q_im[..., None] * bre
        b_re = (bbar_re.transpose(0, 2, 1)[:, :, None, :] * eye_g[:, None, :, None]).reshape(S5_WIDTH, S5_LANES)
        b_im = (bbar_im.transpose(0, 2, 1)[:, :, None, :] * eye_g[:, None, :, None]).reshape(S5_WIDTH, S5_LANES)
        p[f"bmat{d}"] = jnp.concatenate([b_re, b_im], axis=-1).astype(bf16)
        c_re = (s5_c_re[l, d].astype(f32).transpose(0, 2, 1)[:, :, None, :]
                * eye_g[:, None, :, None]).reshape(S5_LANES, S5_WIDTH)
        c_im = (s5_c_im[l, d].astype(f32).transpose(0, 2, 1)[:, :, None, :]
                * eye_g[:, None, :, None]).reshape(S5_LANES, S5_WIDTH)
        p[f"cmat{d}"] = jnp.concatenate([c_re, -c_im], axis=0).astype(bf16)
        lam_row = jnp.concatenate([bar_re.reshape(-1), bar_im.reshape(-1)])
        p[f"lam{d}"] = jnp.broadcast_to(lam_row[None, :], (BATCH, 2 * S5_LANES)).astype(f32)
    p["s5_d"] = s5_d[l][None, :]
    p["w_glu"] = s5_w_glu[l].astype(bf16)
    p["b_glu"] = s5_b_glu[l][None, :]
    p["w_out"] = w_out[l].astype(bf16)
    return p


def kernel(x, c, ctx, c_ctx, w_mod, b_mod, norm_g, ffn_w_gu, ffn_w_down, w_in, w_out, na_qk_g, na_rpb, mla_cq_g, mla_ckv_g, mla_w_uq, mla_w_ukv, mla_qk_g, s5_lam_re, s5_lam_im, s5_log_dt, s5_b_re, s5_b_im, s5_c_re, s5_c_im, s5_d, s5_w_glu, s5_b_glu):
    assert x.shape == (BATCH, SEQ, D_MODEL) and ctx.shape == (BATCH, CTX_LEN, D_MODEL)
    mod_rows = 16
    cvec = jnp.concatenate([c, c_ctx[None, :], jnp.zeros((mod_rows - BATCH - 1, D_MODEL), f32)], axis=0)
    mod_all = _mod_call(cvec, w_mod, b_mod).reshape(DEPTH, mod_rows, N_MOD, D_MODEL)
    cos_t, sin_t = _rope_tables()
    h = jnp.concatenate([ctx, x], axis=1)
    for l in range(DEPTH):
        p = _layer_params(l, ffn_w_gu, ffn_w_down, w_in, w_out, na_qk_g, na_rpb, mla_cq_g, mla_ckv_g,
                          mla_w_uq, mla_w_ukv, mla_qk_g, s5_lam_re, s5_lam_im, s5_log_dt, s5_b_re,
                          s5_b_im, s5_c_re, s5_c_im, s5_d, s5_w_glu, s5_b_glu)
        mod_c = jnp.broadcast_to(mod_all[l, BATCH][None], (BATCH, N_MOD, D_MODEL))
        mod = jnp.stack([mod_c, mod_all[l, :BATCH]], axis=1)
        need_ctx = l < DEPTH - 1
        h = _ffn_call(h, mod, norm_g[l, 0][None, :], p["wgu0"], p["wd0"], 0)
        qn, kn, vn, qm, km, vm, u = _inproj_call(
            h, mod, norm_g[l, 1][None, :], p["win"], p["naq_g"], p["nak_g"], p["cq_g"], p["ckv_g"],
            p["wq"], p["wkv"], p["gq"], p["gqp"], p["gk"], p["gkp"], cos_t, sin_t)
        a = _na_call(qn, kn, vn, p["na_bias"], need_ctx)
        bm = _mla_call(qm, km, vm, need_ctx)
        u_t = u.transpose(1, 0, 2).reshape(S_ALL * BATCH, S5_WIDTH)
        yf = _s5_scan_call(u_t, p["bmat0"], p["cmat0"], p["lam0"], reverse=False)
        yb = _s5_scan_call(u_t, p["bmat1"], p["cmat1"], p["lam1"], reverse=True)
        s_t = _s5_out_call(yf, yb, u_t, p["s5_d"], p["w_glu"], p["b_glu"], need_ctx)
        s = s_t.reshape(-1, BATCH, S5_WIDTH).transpose(1, 0, 2)
        h = _outproj_call(h, mod, a, bm, s, p["w_out"], need_ctx)
        h = _ffn_call(h, mod, norm_g[l, 2][None, :], p["wgu1"], p["wd1"], 6)
    return h
```

```python
import functools
import math

import jax
import jax.numpy as jnp
from jax import lax
from jax.experimental import pallas as pl
from jax.experimental.pallas import tpu as pltpu

D_MODEL = 1024
BATCH = 8
SEQ = 2048
DEPTH = 2
CTX_LEN = 256
S_ALL = CTX_LEN + SEQ
GRID_W = 64
GRID_ROWS = SEQ // GRID_W
NA_HEAD_DIM = 64
NA_WIDTH = 384
NA_HEADS = 6
WIN_ROWS = 8
WIN_COLS = 16
MLA_V_DIM = 64
MLA_WIDTH = 384
MLA_HEADS = 6
MLA_NOPE = 64
MLA_ROPE = 32
MLA_QK = 96
Q_LORA = 384
KV_LORA = 256
S5_WIDTH = 256
S5_GROUP = 16
S5_GROUPS = 16
S5_STATE = 64
S5_LANES = S5_GROUPS * S5_STATE
D_FF = 2816
ROPE_THETA = 10000.0
EPS = 1e-6
N_MOD = 9
NEG_INF = -1e30

LANES = 128
TM = 256
N_TILES = S_ALL // TM
FF_CHUNK = 1408
N_FF_CHUNKS = D_FF // FF_CHUNK
HEAD_PAD = 128
S5_T = 128
S5_ROWS = S5_T * BATCH
S5_CHUNKS = S_ALL // S5_T
S5_CTX_CHUNKS = CTX_LEN // S5_T
VMEM_LIMIT = 56 * 1024 * 1024

ZC_QA, ZC_KA, ZC_VA, ZC_CQ, ZC_CKV = 0, 384, 768, 1152, 1536
Z_MAIN = 1792
Z_TAIL = S5_WIDTH + LANES

LOG2E = math.log2(math.e)
NA_Q_SCALE = NA_HEAD_DIM ** -0.5 * LOG2E
MLA_Q_SCALE = MLA_QK ** -0.5 * LOG2E

f32 = jnp.float32
bf16 = jnp.bfloat16


def _dot(a, b):
    return jnp.dot(a, b, preferred_element_type=f32)


def _sigmoid(x):
    return 1.0 / (1.0 + jnp.exp(-x))


def _rms(x, n):
    return lax.rsqrt(jnp.sum(x * x, axis=-1, keepdims=True) / n + EPS)


def _modulated(x, g, shift, scale):
    y = x * _rms(x, D_MODEL)
    return (y * g) * (1.0 + scale) + shift


def _mod_kernel(c_ref, w_ref, b_ref, o_ref):
    c = c_ref[...]
    a = (c * _sigmoid(c)).astype(bf16)
    o_ref[0] = _dot(a, w_ref[0].astype(bf16)) + b_ref[0]


def _mod_call(cvec, w_mod, b_mod):
    rows = cvec.shape[0]
    return pl.pallas_call(
        _mod_kernel,
        grid=(DEPTH, N_MOD),
        in_specs=[
            pl.BlockSpec((rows, D_MODEL), lambda l, j: (0, 0)),
            pl.BlockSpec((1, D_MODEL, D_MODEL), lambda l, j: (l, 0, j)),
            pl.BlockSpec((1, 1, D_MODEL), lambda l, j: (l, 0, j)),
        ],
        out_specs=pl.BlockSpec((1, rows, D_MODEL), lambda l, j: (l, 0, j)),
        out_shape=jax.ShapeDtypeStruct((DEPTH, rows, N_MOD * D_MODEL), f32),
        compiler_params=pltpu.CompilerParams(
            dimension_semantics=("arbitrary", "arbitrary"), vmem_limit_bytes=VMEM_LIMIT),
        name="adaln_mod",
    )(cvec, w_mod, b_mod.reshape(DEPTH, 1, N_MOD * D_MODEL))


def _mod_spec(first=0):
    return pl.BlockSpec((1, 1, N_MOD, D_MODEL), lambda b, j: (b, jnp.minimum(j + first, 1), 0, 0))


def _tok_spec(width, first=0):
    return pl.BlockSpec((1, TM, width), lambda b, j: (b, j + first, 0))


def _tok_spec_t(width):
    return pl.BlockSpec((1, width, TM), lambda b, j: (b, 0, j))


def _param_spec(arr, *lead):
    tail = arr.shape[len(lead):]
    idx = tuple(lead) + (0,) * len(tail)
    return pl.BlockSpec((1,) * len(lead) + tail, lambda *_: idx)


def _ffn_kernel(*refs, i0, fuse_outproj, split_input):
    refs = list(refs)
    if split_input:
        ctx_ref, lat_ref = refs[:2]
        refs = refs[2:]
        x = jnp.where(pl.program_id(1) == 0, ctx_ref[0], lat_ref[0])
    else:
        x = refs[0][0]
        refs = refs[1:]
    mod_ref, g_ref, wgu_ref, wd_ref = refs[:4]
    refs = refs[4:]
    if fuse_outproj:
        a_ref, b_ref, s_ref, wo_ref = refs[:4]
        refs = refs[4:]
        o = _dot(a_ref[0], wo_ref[0, 0:NA_WIDTH, :])
        o += _dot(b_ref[0], wo_ref[0, NA_WIDTH:NA_WIDTH + MLA_WIDTH, :])
        o += _dot(s_ref[0], wo_ref[0, NA_WIDTH + MLA_WIDTH:, :])
        x = x + mod_ref[0, 0, 5:6, :] * o
    o_ref, x_scr, xm_scr = refs
    x_scr[...] = x
    shift = mod_ref[0, 0, i0:i0 + 1, :]
    scale = mod_ref[0, 0, i0 + 1:i0 + 2, :]
    gate = mod_ref[0, 0, i0 + 2:i0 + 3, :]
    xm_scr[...] = _modulated(x, g_ref[0, 0], shift, scale).astype(bf16)
    acc = None
    for c in range(N_FF_CHUNKS):
        lo = c * FF_CHUNK
        gt = _dot(xm_scr[...], wgu_ref[0, 0, :, lo:lo + FF_CHUNK])
        up = _dot(xm_scr[...], wgu_ref[0, 0, :, D_FF + lo:D_FF + lo + FF_CHUNK])
        a = ((gt * _sigmoid(gt)) * up).astype(bf16)
        part = _dot(a, wd_ref[0, 0, lo:lo + FF_CHUNK, :])
        acc = part if acc is None else acc + part
    o_ref[0] = x_scr[...] + (0.5 * gate) * acc


def _ffn_call(h, mod, norm_g, wgu, wd, l, k, *, mixers=None, w_out=None, out_ctx=True):
    split = isinstance(h, tuple)
    in_ctx = split or h.shape[1] == S_ALL
    first = 0 if (in_ctx and out_ctx) else 1
    n_tiles = N_TILES - first
    if split:
        ctx, lat = h
        ins = [ctx, lat]
        in_specs = [pl.BlockSpec((1, TM, D_MODEL), lambda b, j: (b, 0, 0)),
                    pl.BlockSpec((1, TM, D_MODEL), lambda b, j: (b, jnp.maximum(j - 1, 0), 0))]
    else:
        ins = [h]
        in_specs = [_tok_spec(D_MODEL, first if in_ctx else 0)]
    ins += [mod, norm_g, wgu, wd]
    in_specs += [_mod_spec(first), _param_spec(norm_g, l, 2 * k), _param_spec(wgu, l, k), _param_spec(wd, l, k)]
    if mixers is not None:
        ins += list(mixers) + [w_out]
        in_specs += [_tok_spec(NA_WIDTH), _tok_spec(MLA_WIDTH), _tok_spec(S5_WIDTH), _param_spec(w_out, l)]
    return pl.pallas_call(
        functools.partial(_ffn_kernel, i0=6 * k, fuse_outproj=mixers is not None, split_input=split),
        grid=(BATCH, n_tiles),
        in_specs=in_specs,
        out_specs=_tok_spec(D_MODEL),
        out_shape=jax.ShapeDtypeStruct((BATCH, n_tiles * TM, D_MODEL), f32),
        scratch_shapes=[pltpu.VMEM((TM, D_MODEL), f32), pltpu.VMEM((TM, D_MODEL), bf16)],
        compiler_params=pltpu.CompilerParams(
            dimension_semantics=("arbitrary", "arbitrary"), vmem_limit_bytes=VMEM_LIMIT),
        name="swiglu_half_step",
    )(*ins)


def _pair_head_norm(x, g):
    lane = lax.broadcasted_iota(jnp.int32, (1, LANES), 1)
    lo = lane < NA_HEAD_DIM
    outs = []
    for p in range(NA_WIDTH // LANES):
        xb = x[:, p * LANES:(p + 1) * LANES]
        sq = xb * xb
        s_lo = jnp.sum(jnp.where(lo, sq, 0.0), axis=-1, keepdims=True)
        s_hi = jnp.sum(jnp.where(lo, 0.0, sq), axis=-1, keepdims=True)
        r = jnp.where(lo, lax.rsqrt(s_lo / NA_HEAD_DIM + EPS), lax.rsqrt(s_hi / NA_HEAD_DIM + EPS))
        outs.append((xb * r) * g[:, p * LANES:(p + 1) * LANES])
    return outs


def _inproj_kernel(h_ref, mod_ref, g_ref, wmain_ref, wtail_ref, naq_g_ref, nak_g_ref, cq_g_ref, ckv_g_ref,
                   wq_ref, wkv_ref, gq_ref, gqp_ref, gk_ref, gkp_ref, cos_ref, sin_ref,
                   qn_ref, kn_ref, vn_ref, qm_ref, km_ref, vm_ref, u_ref):
    x = h_ref[0]
    shift = mod_ref[0, 0, 3:4, :]
    scale = mod_ref[0, 0, 4:5, :]
    xm = _modulated(x, g_ref[0, 0], shift, scale).astype(bf16)
    z = _dot(xm, wmain_ref[0])
    zt = _dot(xm, wtail_ref[0])

    qn = _pair_head_norm(z[:, ZC_QA:ZC_QA + NA_WIDTH], naq_g_ref[0])
    kn = _pair_head_norm(z[:, ZC_KA:ZC_KA + NA_WIDTH], nak_g_ref[0])
    for p in range(NA_WIDTH // LANES):
        qn_ref[0, p * LANES:(p + 1) * LANES, :] = (qn[p] * NA_Q_SCALE).T.astype(bf16)
        kn_ref[0, :, p * LANES:(p + 1) * LANES] = kn[p].astype(bf16)
        vn_ref[0, p * LANES:(p + 1) * LANES, :] = z[:, ZC_VA + p * LANES:ZC_VA + (p + 1) * LANES].T.astype(bf16)
    u_ref[0] = zt[:, 0:S5_WIDTH]

    cos_t = cos_ref[...]
    sin_t = sin_ref[...]

    cq = z[:, ZC_CQ:ZC_CQ + Q_LORA]
    ncq = ((cq * _rms(cq, Q_LORA)) * cq_g_ref[0]).astype(bf16)
    qq = _dot(ncq, wq_ref[0])
    gq = gq_ref[0]
    gqp = gqp_ref[0]
    for hd in range(MLA_HEADS):
        pre = qq[:, hd * HEAD_PAD:(hd + 1) * HEAD_PAD]
        perm = qq[:, (MLA_HEADS + hd) * HEAD_PAD:(MLA_HEADS + hd + 1) * HEAD_PAD]
        r = _rms(pre, MLA_QK)
        qm_ref[0, hd * HEAD_PAD:(hd + 1) * HEAD_PAD, :] = (
            ((pre * gq) * cos_t + (perm * gqp) * sin_t) * (r * MLA_Q_SCALE)).T.astype(bf16)

    ckv = z[:, ZC_CKV:ZC_CKV + KV_LORA]
    nkv = ((ckv * _rms(ckv, KV_LORA)) * ckv_g_ref[0]).astype(bf16)
    kv = _dot(nkv, wkv_ref[0])
    for p in range(MLA_WIDTH // LANES):
        lo_col = MLA_HEADS * HEAD_PAD + p * LANES
        vm_ref[0, p * LANES:(p + 1) * LANES, :] = kv[:, lo_col:lo_col + LANES].T.astype(bf16)
    krb = zt[:, S5_WIDTH:S5_WIDTH + LANES]
    lane = lax.broadcasted_iota(jnp.int32, (1, LANES), 1)
    rope_lanes = (lane >= MLA_NOPE) & (lane < MLA_QK)
    kr_a = jnp.where(rope_lanes, pltpu.roll(krb, MLA_NOPE, axis=1), 0.0)
    kr_b = jnp.where(rope_lanes, pltpu.roll(krb, MLA_ROPE, axis=1), 0.0)
    gk = gk_ref[0]
    gkp = gkp_ref[0]
    rot_part = (kr_b * gkp) * sin_t
    for hd in range(MLA_HEADS):
        kfull = kv[:, hd * HEAD_PAD:(hd + 1) * HEAD_PAD] + kr_a
        r = _rms(kfull, MLA_QK)
        km_ref[0, :, hd * HEAD_PAD:(hd + 1) * HEAD_PAD] = (
            ((kfull * gk) * cos_t + rot_part) * r).astype(bf16)


def _inproj_call(h, mod, norm_g, l, params, cos_t, sin_t):
    qk_w = MLA_HEADS * HEAD_PAD
    tab_spec = pl.BlockSpec((TM, LANES), lambda b, j: (j, 0))
    names = ["win_main", "win_tail", "naq_g", "nak_g", "cq_g", "ckv_g", "wq", "wkv", "gq", "gqp", "gk", "gkp"]
    return pl.pallas_call(
        _inproj_kernel,
        grid=(BATCH, N_TILES),
        in_specs=([_tok_spec(D_MODEL), _mod_spec(), _param_spec(norm_g, l, 1)]
                  + [_param_spec(params[n], l) for n in names] + [tab_spec, tab_spec]),
        out_specs=[
            _tok_spec_t(NA_WIDTH), _tok_spec(NA_WIDTH), _tok_spec_t(NA_WIDTH),
            _tok_spec_t(qk_w), _tok_spec(qk_w), _tok_spec_t(MLA_WIDTH), _tok_spec(S5_WIDTH),
        ],
        out_shape=[
            jax.ShapeDtypeStruct((BATCH, NA_WIDTH, S_ALL), bf16),
            jax.ShapeDtypeStruct((BATCH, S_ALL, NA_WIDTH), bf16),
            jax.ShapeDtypeStruct((BATCH, NA_WIDTH, S_ALL), bf16),
            jax.ShapeDtypeStruct((BATCH, qk_w, S_ALL), bf16),
            jax.ShapeDtypeStruct((BATCH, S_ALL, qk_w), bf16),
            jax.ShapeDtypeStruct((BATCH, MLA_WIDTH, S_ALL), bf16),
            jax.ShapeDtypeStruct((BATCH, S_ALL, S5_WIDTH), f32),
        ],
        compiler_params=pltpu.CompilerParams(
            dimension_semantics=("arbitrary", "arbitrary"), vmem_limit_bytes=VMEM_LIMIT),
        name="in_proj_heads",
    )(h, mod, norm_g, *[params[n] for n in names], cos_t, sin_t)


def _pair_block_diag(qt):
    d = qt.shape[0] // 2
    z = jnp.zeros((d, qt.shape[1]), qt.dtype)
    return jnp.concatenate([jnp.concatenate([qt[:d], z], axis=0),
                            jnp.concatenate([z, qt[d:]], axis=0)], axis=1)


def _attend_pair(key_blocks, vt_blocks, q_bd, bias_blocks):
    def scores(i):
        s = _dot(key_blocks[i], q_bd)
        return s if bias_blocks[i] is None else s + bias_blocks[i]

    m = den = acc = None
    s_next = scores(0)
    for i, vt in enumerate(vt_blocks):
        s = s_next
        if i + 1 < len(vt_blocks):
            s_next = scores(i + 1)
        mi = jnp.max(s, axis=0, keepdims=True)
        m_new = mi if m is None else jnp.maximum(m, mi)
        p = jnp.exp2(s - m_new)
        li = jnp.sum(p, axis=0, keepdims=True)
        oi = _dot(vt, p.astype(bf16))
        if m is None:
            den, acc = li, oi
        else:
            alpha = jnp.exp2(m - m_new)
            den = den * alpha + li
            acc = acc * alpha + oi
        m = m_new
    o = acc / den
    dv = o.shape[0] // 2
    return jnp.concatenate([o[:dv, :TM], o[dv:, TM:]], axis=0)


def _store_t(o_ref, ot):
    o_ref[0] = ot.T.astype(bf16)


NA_TILE_ROWS = TM // GRID_W
NA_UNION_ROWS = 12
NA_UNION_KEYS = NA_UNION_ROWS * GRID_W
NA_PATTERNS = 3
NA_BIAS_SHIFTS = 2 * WIN_ROWS


def _na_window_lo(pat, rr):
    return (-rr, -(WIN_ROWS // 2), -(WIN_ROWS // 2) - rr)[pat]


def _na_fill_bias(blk_ref, bias_scr):
    lane = lax.broadcasted_iota(jnp.int32, (1, LANES), 1)
    left = lane < GRID_W
    neg = jnp.full((GRID_W, LANES), NEG_INF, f32)
    for pat in range(NA_PATTERNS):
        for ii in range(NA_UNION_ROWS):
            for t in range(NA_TILE_ROWS // 2):
                d = ii - NA_TILE_ROWS * pat - 2 * t
                ok_l = 0 <= d - _na_window_lo(pat, 2 * t) < WIN_ROWS
                ok_r = 0 <= (d - 1) - _na_window_lo(pat, 2 * t + 1) < WIN_ROWS
                for hh in range(2):
                    if ok_l or ok_r:
                        blk = blk_ref[0, hh, d + WIN_ROWS - 1]
                        if not ok_l:
                            blk = jnp.where(left, NEG_INF, blk)
                        if not ok_r:
                            blk = jnp.where(left, blk, NEG_INF)
                    else:
                        blk = neg
                    c0 = hh * TM + t * LANES
                    bias_scr[pat, ii * GRID_W:(ii + 1) * GRID_W, c0:c0 + LANES] = blk


def _na_kernel(qt_ref, k_ref, vt_ref, blk_ref, o_ref, bias_scr, *, first_tile):
    j = pl.program_id(2) + first_tile

    @pl.when((pl.program_id(1) == 0) & (pl.program_id(2) == 0))
    def _():
        _na_fill_bias(blk_ref, bias_scr)

    kc = k_ref[0, 0:CTX_LEN, :]
    vtc = vt_ref[0, :, 0:CTX_LEN]
    q_bd = _pair_block_diag(qt_ref[0])

    @pl.when(j == 0)
    def _():
        _store_t(o_ref, _attend_pair([kc], [vtc], q_bd, [None]))

    @pl.when(j > 0)
    def _():
        r0 = (j - 1) * NA_TILE_ROWS
        base = jnp.clip(r0 - WIN_ROWS // 2, 0, GRID_ROWS - NA_UNION_ROWS)
        pat = (r0 - base) // NA_TILE_ROWS
        start = pl.multiple_of(CTX_LEN + base * GRID_W, NA_TILE_ROWS * GRID_W)
        kw = k_ref[0, pl.ds(start, NA_UNION_KEYS), :]
        vtw = vt_ref[0, :, pl.ds(start, NA_UNION_KEYS)]
        _store_t(o_ref, _attend_pair([kc, kw], [vtc, vtw], q_bd, [None, bias_scr[pat]]))


def _na_call(qnt, kn, vnt, bias_blocks, l, with_ctx):
    n_pairs = NA_WIDTH // LANES
    first = 0 if with_ctx else 1
    return pl.pallas_call(
        functools.partial(_na_kernel, first_tile=first),
        grid=(n_pairs, BATCH, N_TILES - first),
        in_specs=[
            pl.BlockSpec((1, LANES, TM), lambda p, b, j: (b, p, j + first)),
            pl.BlockSpec((1, S_ALL, LANES), lambda p, b, j: (b, 0, p)),
            pl.BlockSpec((1, LANES, S_ALL), lambda p, b, j: (b, p, 0)),
            pl.BlockSpec((1, 2, NA_BIAS_SHIFTS, GRID_W, LANES), lambda p, b, j: (l, p, 0, 0, 0)),
        ],
        out_specs=pl.BlockSpec((1, TM, LANES), lambda p, b, j: (b, j, p)),
        out_shape=jax.ShapeDtypeStruct((BATCH, S_ALL - first * TM, NA_WIDTH), bf16),
        scratch_shapes=[pltpu.VMEM((NA_PATTERNS, NA_UNION_KEYS, 2 * TM), f32)],
        compiler_params=pltpu.CompilerParams(
            dimension_semantics=("arbitrary", "arbitrary", "arbitrary"), vmem_limit_bytes=VMEM_LIMIT),
        name="na_attention",
    )(qnt, kn, vnt, bias_blocks)


MLA_KEY_BLOCK = 1152


def _mla_kernel(qt_ref, k_ref, vt_ref, o_ref, *, first_tile):
    j = pl.program_id(2) + first_tile
    q_bd = _pair_block_diag(qt_ref[0])

    def attend(n_keys, block):
        keys = [k_ref[0, s0:s0 + block, :] for s0 in range(0, n_keys, block)]
        vts = [vt_ref[0, :, s0:s0 + block] for s0 in range(0, n_keys, block)]
        _store_t(o_ref, _attend_pair(keys, vts, q_bd, [None] * len(keys)))

    @pl.when(j == 0)
    def _():
        attend(CTX_LEN, CTX_LEN)

    @pl.when(j > 0)
    def _():
        attend(S_ALL, MLA_KEY_BLOCK)


def _mla_call(qmt, km, vmt, with_ctx):
    n_pairs = MLA_HEADS // 2
    first = 0 if with_ctx else 1
    return pl.pallas_call(
        functools.partial(_mla_kernel, first_tile=first),
        grid=(BATCH, n_pairs, N_TILES - first),
        in_specs=[
            pl.BlockSpec((1, 2 * HEAD_PAD, TM), lambda b, p, j: (b, p, j + first)),
            pl.BlockSpec((1, S_ALL, 2 * HEAD_PAD), lambda b, p, j: (b, 0, p)),
            pl.BlockSpec((1, LANES, S_ALL), lambda b, p, j: (b, p, 0)),
        ],
        out_specs=pl.BlockSpec((1, TM, LANES), lambda b, p, j: (b, j, p)),
        out_shape=jax.ShapeDtypeStruct((BATCH, S_ALL - first * TM, MLA_WIDTH), bf16),
        compiler_params=pltpu.CompilerParams(
            dimension_semantics=("arbitrary", "arbitrary", "arbitrary"), vmem_limit_bytes=VMEM_LIMIT),
        name="mla_attention",
    )(qmt, km, vmt)


def _s5_scan_kernel(u_ref, bmat_ref, cmat_ref, lam_ref, y_ref, h_ref, st_ref, *, reverse):
    i = pl.program_id(0)

    @pl.when(i == 0)
    def _():
        st_ref[...] = jnp.zeros_like(st_ref)

    h_ref[...] = _dot(u_ref[...].astype(bf16), bmat_ref[0, 0])

    def step(t, carry):
        hr, hi = carry
        row = (S5_T - 1 - t) if reverse else t
        off = pl.multiple_of(row * BATCH, BATCH)
        lr = lam_ref[0, 0, :, 0:S5_LANES]
        li = lam_ref[0, 0, :, S5_LANES:]
        nr = (lr * hr - li * hi) + h_ref[pl.ds(off, BATCH), 0:S5_LANES]
        ni = (lr * hi + li * hr) + h_ref[pl.ds(off, BATCH), S5_LANES:]
        h_ref[pl.ds(off, BATCH), 0:S5_LANES] = nr
        h_ref[pl.ds(off, BATCH), S5_LANES:] = ni
        return nr, ni

    hr, hi = lax.fori_loop(0, S5_T, step, (st_ref[:, 0:S5_LANES], st_ref[:, S5_LANES:]))
    st_ref[:, 0:S5_LANES] = hr
    st_ref[:, S5_LANES:] = hi

    blk = 256
    for rb in range(S5_ROWS // blk):
        y_ref[rb * blk:(rb + 1) * blk, :] = _dot(
            h_ref[rb * blk:(rb + 1) * blk, :].astype(bf16), cmat_ref[0, 0])


def _s5_scan_call(u_t, bmat, cmat, lam, l, d):
    reverse = d == 1
    if reverse:
        def chunk(i):
            return jnp.where(i < S5_CTX_CHUNKS, S5_CTX_CHUNKS - 1 - i, S5_CHUNKS + S5_CTX_CHUNKS - 1 - i)
    else:
        def chunk(i):
            return i
    return pl.pallas_call(
        functools.partial(_s5_scan_kernel, reverse=reverse),
        grid=(S5_CHUNKS,),
        in_specs=[
            pl.BlockSpec((S5_ROWS, S5_WIDTH), lambda i: (chunk(i), 0)),
            _param_spec(bmat, l, d), _param_spec(cmat, l, d), _param_spec(lam, l, d),
        ],
        out_specs=pl.BlockSpec((S5_ROWS, S5_WIDTH), lambda i: (chunk(i), 0)),
        out_shape=jax.ShapeDtypeStruct((S_ALL * BATCH, S5_WIDTH), f32),
        scratch_shapes=[pltpu.VMEM((S5_ROWS, 2 * S5_LANES), f32), pltpu.VMEM((BATCH, 2 * S5_LANES), f32)],
        compiler_params=pltpu.CompilerParams(
            dimension_semantics=("arbitrary",), vmem_limit_bytes=VMEM_LIMIT),
        name="s5_scan_bwd" if reverse else "s5_scan_fwd",
    )(u_t, bmat, cmat, lam)


def _s5_out_kernel(yf_ref, yb_ref, u_ref, d_ref, w_ref, b_ref, o_ref):
    y = (yf_ref[...] + yb_ref[...]) + d_ref[0] * u_ref[...]
    c0 = math.sqrt(2.0 / math.pi)
    gl = 0.5 * y * (1.0 + jnp.tanh(c0 * (y + 0.044715 * (y * y * y))))
    o = _dot(gl.astype(bf16), w_ref[0]) + b_ref[0]
    o_ref[...] = (o[:, :S5_WIDTH] * _sigmoid(o[:, S5_WIDTH:])).astype(bf16)


def _s5_out_call(yf, yb, u_t, d_skip, w_glu, b_glu, l, with_ctx):
    rows = 1024
    first = 0 if with_ctx else CTX_LEN * BATCH // rows
    in_spec = pl.BlockSpec((rows, S5_WIDTH), lambda i: (i + first, 0))
    row_spec = pl.BlockSpec((rows, S5_WIDTH), lambda i: (i, 0))
    return pl.pallas_call(
        _s5_out_kernel,
        grid=(S_ALL * BATCH // rows - first,),
        in_specs=[in_spec, in_spec, in_spec,
                  _param_spec(d_skip, l), _param_spec(w_glu, l), _param_spec(b_glu, l)],
        out_specs=row_spec,
        out_shape=jax.ShapeDtypeStruct((S_ALL * BATCH - first * rows, S5_WIDTH), bf16),
        compiler_params=pltpu.CompilerParams(
            dimension_semantics=("arbitrary",), vmem_limit_bytes=VMEM_LIMIT),
        name="s5_readout",
    )(yf, yb, u_t, d_skip, w_glu, b_glu)


def _rope_perm():
    half = MLA_ROPE // 2
    quarter = half // 2
    idx, sign = [], []
    for j in range(MLA_ROPE):
        if (j % half) < quarter:
            idx.append(j + quarter)
            sign.append(-1.0)
        else:
            idx.append(j - quarter)
            sign.append(1.0)
    return jnp.array(idx, jnp.int32), jnp.array(sign, f32)


def _rope_tables():
    quarter = MLA_ROPE // 4
    t = jnp.arange(SEQ)
    row = (t // GRID_W).astype(f32)
    col = (t % GRID_W).astype(f32)
    inv = ROPE_THETA ** (-jnp.arange(quarter, dtype=f32) / quarter)
    ang = jnp.concatenate([row[:, None] * inv] * 2 + [col[:, None] * inv] * 2, axis=-1)
    ang = jnp.concatenate([jnp.zeros((CTX_LEN, MLA_ROPE), f32), ang], axis=0)
    ones = jnp.ones((S_ALL, MLA_NOPE), f32)
    pad = jnp.zeros((S_ALL, HEAD_PAD - MLA_QK), f32)
    cos_t = jnp.concatenate([ones, jnp.cos(ang), pad], axis=-1)
    sin_t = jnp.concatenate([0.0 * ones, jnp.sin(ang), pad], axis=-1)
    return cos_t, sin_t


def _na_bias_blocks(rpb):
    cq = jnp.arange(GRID_W)[None, :]
    kc = jnp.arange(GRID_W)[:, None]
    d_col = jnp.clip(kc - cq, -(WIN_COLS - 1), WIN_COLS - 1) + WIN_COLS - 1
    col_start = jnp.clip(cq - WIN_COLS // 2, 0, GRID_W - WIN_COLS)
    in_win = (kc >= col_start) & (kc < col_start + WIN_COLS)
    onehot = (d_col[:, :, None] == jnp.arange(2 * WIN_COLS - 1)[None, None, :]).astype(f32)
    blocks = jnp.einsum('lhdc,kqc->lhdkq', rpb.astype(f32), onehot, precision=lax.Precision.HIGHEST)
    blocks = jnp.where(in_win, blocks * LOG2E, NEG_INF)
    neg = jnp.full(blocks.shape[:2] + (1, GRID_W, GRID_W), NEG_INF, f32)
    padded = jnp.concatenate([neg, blocks, neg], axis=2)
    return jnp.concatenate([padded[:, :, 1:], padded[:, :, :-1]], axis=-1)


def _pad_heads(w, width):
    lead = w.shape[:-1]
    w = w.reshape(lead + (MLA_HEADS, width))
    w = jnp.pad(w, [(0, 0)] * len(lead) + [(0, 0), (0, HEAD_PAD - width)])
    return w.reshape(lead + (MLA_HEADS * HEAD_PAD,))


def _block_diag_groups(w):
    eye = jnp.eye(S5_GROUPS, dtype=f32)
    lead = w.shape[:-3]
    a, b = w.shape[-2:]
    full = w[..., :, :, None, :] * eye[:, None, :, None]
    return full.reshape(lead + (S5_GROUPS * a, S5_GROUPS * b))


def _prepare_params(ffn_w_gu, ffn_w_down, w_in, w_out, na_qk_g, na_rpb, mla_cq_g, mla_ckv_g, mla_w_uq,
                    mla_w_ukv, mla_qk_g, s5_lam_re, s5_lam_im, s5_log_dt, s5_b_re, s5_b_im, s5_c_re,
                    s5_c_im, s5_d, s5_w_glu, s5_b_glu):
    p = {}
    perm_idx, perm_sign = _rope_perm()
    p["wgu"] = ffn_w_gu.astype(bf16)
    p["wd"] = ffn_w_down.astype(bf16)
    p["w_out"] = w_out.astype(bf16)
    o_kr = Z_MAIN
    o_u = o_kr + MLA_ROPE
    w_kr = w_in[:, :, o_kr:o_kr + MLA_ROPE]
    p["win_main"] = w_in[:, :, :Z_MAIN].astype(bf16)
    p["win_tail"] = jnp.concatenate([
        w_in[:, :, o_u:o_u + S5_WIDTH], w_kr, w_kr[:, :, perm_idx] * perm_sign,
        jnp.zeros((DEPTH, D_MODEL, LANES - 2 * MLA_ROPE), f32)], axis=-1).astype(bf16)
    p["naq_g"] = jnp.tile(na_qk_g[:, 0], (1, NA_HEADS))[:, None, :]
    p["nak_g"] = jnp.tile(na_qk_g[:, 1], (1, NA_HEADS))[:, None, :]
    p["cq_g"] = mla_cq_g[:, None, :]
    p["ckv_g"] = mla_ckv_g[:, None, :]
    wuq = mla_w_uq.reshape(DEPTH, Q_LORA, MLA_HEADS, MLA_QK)
    partner = jnp.concatenate([jnp.zeros((DEPTH, Q_LORA, MLA_HEADS, MLA_NOPE), f32),
                               wuq[..., MLA_NOPE:][..., perm_idx] * perm_sign], axis=-1)
    p["wq"] = jnp.concatenate([_pad_heads(mla_w_uq, MLA_QK),
                               _pad_heads(partner.reshape(DEPTH, Q_LORA, -1), MLA_QK)], axis=-1).astype(bf16)
    wukv = mla_w_ukv.reshape(DEPTH, KV_LORA, MLA_HEADS, MLA_NOPE + MLA_V_DIM)
    p["wkv"] = jnp.concatenate([_pad_heads(wukv[..., :MLA_NOPE].reshape(DEPTH, KV_LORA, -1), MLA_NOPE),
                                wukv[..., MLA_NOPE:].reshape(DEPTH, KV_LORA, -1)], axis=-1).astype(bf16)

    def pad_gain(g):
        z = jnp.zeros((DEPTH, HEAD_PAD - MLA_QK), f32)
        full = jnp.concatenate([g, z], axis=-1)[:, None, :]
        part = jnp.concatenate([jnp.zeros((DEPTH, MLA_NOPE), f32), g[:, MLA_NOPE:][:, perm_idx], z],
                               axis=-1)[:, None, :]
        return full, part

    p["gq"], p["gqp"] = pad_gain(mla_qk_g[:, 0])
    p["gk"], p["gkp"] = pad_gain(mla_qk_g[:, 1])
    p["na_bias"] = _na_bias_blocks(na_rpb)
    lre = s5_lam_re.astype(f32)
    lim = s5_lam_im.astype(f32)
    dt = jnp.exp(s5_log_dt.astype(f32))[..., None]
    mag = jnp.exp(lre * dt)
    bar_re = mag * jnp.cos(lim * dt)
    bar_im = mag * jnp.sin(lim * dt)
    den = lre * lre + lim * lim
    q_re = ((bar_re - 1.0) * lre + bar_im * lim) / den
    q_im = (bar_im * lre - (bar_re - 1.0) * lim) / den
    bre = s5_b_re.astype(f32)
    bim = s5_b_im.astype(f32)
    bbar_re = q_re[..., None] * bre - q_im[..., None] * bim
    bbar_im = q_re[..., None] * bim + q_im[..., None] * bre
    p["bmat"] = jnp.concatenate([_block_diag_groups(jnp.swapaxes(bbar_re, -1, -2)),
                                 _block_diag_groups(jnp.swapaxes(bbar_im, -1, -2))], axis=-1).astype(bf16)
    p["cmat"] = jnp.concatenate([_block_diag_groups(jnp.swapaxes(s5_c_re.astype(f32), -1, -2)),
                                 -_block_diag_groups(jnp.swapaxes(s5_c_im.astype(f32), -1, -2))],
                                axis=-2).astype(bf16)
    lam_row = jnp.concatenate([bar_re.reshape(DEPTH, 2, S5_LANES), bar_im.reshape(DEPTH, 2, S5_LANES)], axis=-1)
    p["lam"] = jnp.broadcast_to(lam_row[:, :, None, :], (DEPTH, 2, BATCH, 2 * S5_LANES))
    p["s5_d"] = s5_d[:, None, :]
    p["w_glu"] = s5_w_glu.astype(bf16)
    p["b_glu"] = s5_b_glu[:, None, :]
    return p


def kernel(x, c, ctx, c_ctx, w_mod, b_mod, norm_g, ffn_w_gu, ffn_w_down, w_in, w_out, na_qk_g, na_rpb, mla_cq_g, mla_ckv_g, mla_w_uq, mla_w_ukv, mla_qk_g, s5_lam_re, s5_lam_im, s5_log_dt, s5_b_re, s5_b_im, s5_c_re, s5_c_im, s5_d, s5_w_glu, s5_b_glu):
    assert x.shape == (BATCH, SEQ, D_MODEL) and ctx.shape == (BATCH, CTX_LEN, D_MODEL)
    mod_rows = 16
    cvec = jnp.concatenate([c, c_ctx[None, :], jnp.zeros((mod_rows - BATCH - 1, D_MODEL), f32)], axis=0)
    mod_all = _mod_call(cvec, w_mod, b_mod).reshape(DEPTH, mod_rows, N_MOD, D_MODEL)
    cos_t, sin_t = _rope_tables()
    p = _prepare_params(ffn_w_gu, ffn_w_down, w_in, w_out, na_qk_g, na_rpb, mla_cq_g, mla_ckv_g,
                        mla_w_uq, mla_w_ukv, mla_qk_g, s5_lam_re, s5_lam_im, s5_log_dt, s5_b_re,
                        s5_b_im, s5_c_re, s5_c_im, s5_d, s5_w_glu, s5_b_glu)
    norm_g4 = norm_g[:, :, None, :]
    h = (ctx, x)
    for l in range(DEPTH):
        mod_c = jnp.broadcast_to(mod_all[l, BATCH][None], (BATCH, N_MOD, D_MODEL))
        mod = jnp.stack([mod_c, mod_all[l, :BATCH]], axis=1)
        need_ctx = l < DEPTH - 1
        h = _ffn_call(h, mod, norm_g4, p["wgu"], p["wd"], l, 0)
        qn, kn, vn, qm, km, vm, u = _inproj_call(h, mod, norm_g4, l, p, cos_t, sin_t)
        a = _na_call(qn, kn, vn, p["na_bias"], l, need_ctx)
        bm = _mla_call(qm, km, vm, need_ctx)
        u_t = u.transpose(1, 0, 2).reshape(S_ALL * BATCH, S5_WIDTH)
        yf = _s5_scan_call(u_t, p["bmat"], p["cmat"], p["lam"], l, 0)
        yb = _s5_scan_call(u_t, p["bmat"], p["cmat"], p["lam"], l, 1)
        s_t = _s5_out_call(yf, yb, u_t, p["s5_d"], p["w_glu"], p["b_glu"], l, need_ctx)
        s = s_t.reshape(-1, BATCH, S5_WIDTH).transpose(1, 0, 2)
        h = _ffn_call(h, mod, norm_g4, p["wgu"], p["wd"], l, 1, mixers=(a, bm, s), w_out=p["w_out"],
                      out_ctx=need_ctx)
    return h
```

```python
import functools
import math

import jax
import jax.numpy as jnp
from jax import lax
from jax.experimental import pallas as pl
from jax.experimental.pallas import tpu as pltpu

D_MODEL = 1024
BATCH = 8
SEQ = 2048
DEPTH = 2
CTX_LEN = 256
S_ALL = CTX_LEN + SEQ
GRID_W = 64
GRID_ROWS = SEQ // GRID_W
NA_HEAD_DIM = 64
NA_WIDTH = 384
NA_HEADS = 6
WIN_ROWS = 8
WIN_COLS = 16
MLA_V_DIM = 64
MLA_WIDTH = 384
MLA_HEADS = 6
MLA_NOPE = 64
MLA_ROPE = 32
MLA_QK = 96
Q_LORA = 384
KV_LORA = 256
S5_WIDTH = 256
S5_GROUP = 16
S5_GROUPS = 16
S5_STATE = 64
S5_LANES = S5_GROUPS * S5_STATE
D_FF = 2816
ROPE_THETA = 10000.0
EPS = 1e-6
N_MOD = 9
NEG_INF = -1e30

LANES = 128
TM = 256
N_TILES = S_ALL // TM
FF_CHUNK = 1408
N_FF_CHUNKS = D_FF // FF_CHUNK
HEAD_PAD = 128
S5_T = 128
S5_ROWS = S5_T * BATCH
S5_CHUNKS = S_ALL // S5_T
S5_CTX_CHUNKS = CTX_LEN // S5_T
VMEM_LIMIT = 56 * 1024 * 1024

ZC_QA, ZC_KA, ZC_VA, ZC_CQ, ZC_CKV = 0, 384, 768, 1152, 1536
Z_MAIN = 1792
Z_TAIL = S5_WIDTH + LANES

LOG2E = math.log2(math.e)
NA_Q_SCALE = NA_HEAD_DIM ** -0.5 * LOG2E
MLA_Q_SCALE = MLA_QK ** -0.5 * LOG2E

f32 = jnp.float32
bf16 = jnp.bfloat16


def _dot(a, b):
    return jnp.dot(a, b, preferred_element_type=f32)


def _sigmoid(x):
    return 1.0 / (1.0 + jnp.exp(-x))


def _rms(x, n):
    return lax.rsqrt(jnp.sum(x * x, axis=-1, keepdims=True) / n + EPS)


def _modulated(x, g, shift, scale):
    y = x * _rms(x, D_MODEL)
    return (y * g) * (1.0 + scale) + shift


def _mod_kernel(c_ref, w_ref, b_ref, o_ref):
    c = c_ref[...]
    a = (c * _sigmoid(c)).astype(bf16)
    o_ref[0] = _dot(a, w_ref[0].astype(bf16)) + b_ref[0]


def _mod_call(cvec, w_mod, b_mod):
    rows = cvec.shape[0]
    return pl.pallas_call(
        _mod_kernel,
        grid=(DEPTH, N_MOD),
        in_specs=[
            pl.BlockSpec((rows, D_MODEL), lambda l, j: (0, 0)),
            pl.BlockSpec((1, D_MODEL, D_MODEL), lambda l, j: (l, 0, j)),
            pl.BlockSpec((1, 1, D_MODEL), lambda l, j: (l, 0, j)),
        ],
        out_specs=pl.BlockSpec((1, rows, D_MODEL), lambda l, j: (l, 0, j)),
        out_shape=jax.ShapeDtypeStruct((DEPTH, rows, N_MOD * D_MODEL), f32),
        compiler_params=pltpu.CompilerParams(
            dimension_semantics=("arbitrary", "arbitrary"), vmem_limit_bytes=VMEM_LIMIT),
        name="adaln_mod",
    )(cvec, w_mod, b_mod.reshape(DEPTH, 1, N_MOD * D_MODEL))


def _mod_spec(first=0):
    return pl.BlockSpec((1, 1, N_MOD, D_MODEL), lambda b, j: (b, jnp.minimum(j + first, 1), 0, 0))


def _tok_spec(width, first=0):
    return pl.BlockSpec((1, TM, width), lambda b, j: (b, j + first, 0))


def _tok_spec_t(width):
    return pl.BlockSpec((1, width, TM), lambda b, j: (b, 0, j))


def _param_spec(arr, *lead):
    tail = arr.shape[len(lead):]
    idx = tuple(lead) + (0,) * len(tail)
    return pl.BlockSpec((1,) * len(lead) + tail, lambda *_: idx)


def _ffn_kernel(*refs, i0, fuse_outproj, split_input):
    refs = list(refs)
    if split_input:
        ctx_ref, lat_ref = refs[:2]
        refs = refs[2:]
        x = jnp.where(pl.program_id(1) == 0, ctx_ref[0], lat_ref[0])
    else:
        x = refs[0][0]
        refs = refs[1:]
    mod_ref, g_ref, wgu_ref, wd_ref = refs[:4]
    refs = refs[4:]
    if fuse_outproj:
        a_ref, b_ref, s_ref, wo_ref = refs[:4]
        refs = refs[4:]
        o = _dot(a_ref[0], wo_ref[0, 0:NA_WIDTH, :])
        o += _dot(b_ref[0], wo_ref[0, NA_WIDTH:NA_WIDTH + MLA_WIDTH, :])
        o += _dot(s_ref[0], wo_ref[0, NA_WIDTH + MLA_WIDTH:, :])
        x = x + mod_ref[0, 0, 5:6, :] * o
    o_ref, x_scr, xm_scr = refs
    x_scr[...] = x
    shift = mod_ref[0, 0, i0:i0 + 1, :]
    scale = mod_ref[0, 0, i0 + 1:i0 + 2, :]
    gate = mod_ref[0, 0, i0 + 2:i0 + 3, :]
    xm_scr[...] = _modulated(x, g_ref[0, 0], shift, scale).astype(bf16)
    acc = None
    for c in range(N_FF_CHUNKS):
        lo = c * FF_CHUNK
        gt = _dot(xm_scr[...], wgu_ref[0, 0, :, lo:lo + FF_CHUNK])
        up = _dot(xm_scr[...], wgu_ref[0, 0, :, D_FF + lo:D_FF + lo + FF_CHUNK])
        a = ((gt * _sigmoid(gt)) * up).astype(bf16)
        part = _dot(a, wd_ref[0, 0, lo:lo + FF_CHUNK, :])
        acc = part if acc is None else acc + part
    o_ref[0] = x_scr[...] + (0.5 * gate) * acc


def _ffn_call(h, mod, norm_g, wgu, wd, l, k, *, mixers=None, w_out=None, out_ctx=True):
    split = isinstance(h, tuple)
    in_ctx = split or h.shape[1] == S_ALL
    first = 0 if (in_ctx and out_ctx) else 1
    n_tiles = N_TILES - first
    if split:
        ctx, lat = h
        ins = [ctx, lat]
        in_specs = [pl.BlockSpec((1, TM, D_MODEL), lambda b, j: (b, 0, 0)),
                    pl.BlockSpec((1, TM, D_MODEL), lambda b, j: (b, jnp.maximum(j - 1, 0), 0))]
    else:
        ins = [h]
        in_specs = [_tok_spec(D_MODEL, first if in_ctx else 0)]
    ins += [mod, norm_g, wgu, wd]
    in_specs += [_mod_spec(first), _param_spec(norm_g, l, 2 * k), _param_spec(wgu, l, k), _param_spec(wd, l, k)]
    if mixers is not None:
        ins += list(mixers) + [w_out]
        in_specs += [_tok_spec(NA_WIDTH), _tok_spec(MLA_WIDTH), _tok_spec(S5_WIDTH), _param_spec(w_out, l)]
    return pl.pallas_call(
        functools.partial(_ffn_kernel, i0=6 * k, fuse_outproj=mixers is not None, split_input=split),
        grid=(BATCH, n_tiles),
        in_specs=in_specs,
        out_specs=_tok_spec(D_MODEL),
        out_shape=jax.ShapeDtypeStruct((BATCH, n_tiles * TM, D_MODEL), f32),
        scratch_shapes=[pltpu.VMEM((TM, D_MODEL), f32), pltpu.VMEM((TM, D_MODEL), bf16)],
        compiler_params=pltpu.CompilerParams(
            dimension_semantics=("arbitrary", "arbitrary"), vmem_limit_bytes=VMEM_LIMIT),
        name="swiglu_half_step",
    )(*ins)


def _pair_head_norm(x, g):
    lane = lax.broadcasted_iota(jnp.int32, (1, LANES), 1)
    lo = lane < NA_HEAD_DIM
    outs = []
    for p in range(NA_WIDTH // LANES):
        xb = x[:, p * LANES:(p + 1) * LANES]
        sq = xb * xb
        s_lo = jnp.sum(jnp.where(lo, sq, 0.0), axis=-1, keepdims=True)
        s_hi = jnp.sum(jnp.where(lo, 0.0, sq), axis=-1, keepdims=True)
        r = jnp.where(lo, lax.rsqrt(s_lo / NA_HEAD_DIM + EPS), lax.rsqrt(s_hi / NA_HEAD_DIM + EPS))
        outs.append((xb * r) * g[:, p * LANES:(p + 1) * LANES])
    return outs


def _inproj_kernel(h_ref, mod_ref, g_ref, wmain_ref, wtail_ref, naq_g_ref, nak_g_ref, cq_g_ref, ckv_g_ref,
                   wq_ref, wkv_ref, gq_ref, gqp_ref, gk_ref, gkp_ref, cos_ref, sin_ref,
                   qn_ref, kn_ref, vn_ref, qm_ref, km_ref, vm_ref, u_ref):
    x = h_ref[0]
    shift = mod_ref[0, 0, 3:4, :]
    scale = mod_ref[0, 0, 4:5, :]
    xm = _modulated(x, g_ref[0, 0], shift, scale).astype(bf16)
    z = _dot(xm, wmain_ref[0])
    zt = _dot(xm, wtail_ref[0])

    qn = _pair_head_norm(z[:, ZC_QA:ZC_QA + NA_WIDTH], naq_g_ref[0])
    kn = _pair_head_norm(z[:, ZC_KA:ZC_KA + NA_WIDTH], nak_g_ref[0])
    for p in range(NA_WIDTH // LANES):
        qn_ref[0, p * LANES:(p + 1) * LANES, :] = (qn[p] * NA_Q_SCALE).T.astype(bf16)
        kn_ref[0, :, p * LANES:(p + 1) * LANES] = kn[p].astype(bf16)
        vn_ref[0, p * LANES:(p + 1) * LANES, :] = z[:, ZC_VA + p * LANES:ZC_VA + (p + 1) * LANES].T.astype(bf16)
    u_ref[0] = zt[:, 0:S5_WIDTH]

    cos_t = cos_ref[...]
    sin_t = sin_ref[...]

    cq = z[:, ZC_CQ:ZC_CQ + Q_LORA]
    ncq = ((cq * _rms(cq, Q_LORA)) * cq_g_ref[0]).astype(bf16)
    qq = _dot(ncq, wq_ref[0])
    gq = gq_ref[0]
    gqp = gqp_ref[0]
    for hd in range(MLA_HEADS):
        pre = qq[:, hd * HEAD_PAD:(hd + 1) * HEAD_PAD]
        perm = qq[:, (MLA_HEADS + hd) * HEAD_PAD:(MLA_HEADS + hd + 1) * HEAD_PAD]
        r = _rms(pre, MLA_QK)
        qm_ref[0, hd * HEAD_PAD:(hd + 1) * HEAD_PAD, :] = (
            ((pre * gq) * cos_t + (perm * gqp) * sin_t) * (r * MLA_Q_SCALE)).T.astype(bf16)

    ckv = z[:, ZC_CKV:ZC_CKV + KV_LORA]
    nkv = ((ckv * _rms(ckv, KV_LORA)) * ckv_g_ref[0]).astype(bf16)
    kv = _dot(nkv, wkv_ref[0])
    for p in range(MLA_WIDTH // LANES):
        lo_col = MLA_HEADS * HEAD_PAD + p * LANES
        vm_ref[0, p * LANES:(p + 1) * LANES, :] = kv[:, lo_col:lo_col + LANES].T.astype(bf16)
    krb = zt[:, S5_WIDTH:S5_WIDTH + LANES]
    lane = lax.broadcasted_iota(jnp.int32, (1, LANES), 1)
    rope_lanes = (lane >= MLA_NOPE) & (lane < MLA_QK)
    kr_a = jnp.where(rope_lanes, pltpu.roll(krb, MLA_NOPE, axis=1), 0.0)
    kr_b = jnp.where(rope_lanes, pltpu.roll(krb, MLA_ROPE, axis=1), 0.0)
    gk = gk_ref[0]
    gkp = gkp_ref[0]
    rot_part = (kr_b * gkp) * sin_t
    for hd in range(MLA_HEADS):
        kfull = kv[:, hd * HEAD_PAD:(hd + 1) * HEAD_PAD] + kr_a
        r = _rms(kfull, MLA_QK)
        km_ref[0, :, hd * HEAD_PAD:(hd + 1) * HEAD_PAD] = (
            ((kfull * gk) * cos_t + rot_part) * r).astype(bf16)


def _inproj_call(h, mod, norm_g, l, params, cos_t, sin_t):
    qk_w = MLA_HEADS * HEAD_PAD
    tab_spec = pl.BlockSpec((TM, LANES), lambda b, j: (j, 0))
    names = ["win_main", "win_tail", "naq_g", "nak_g", "cq_g", "ckv_g", "wq", "wkv", "gq", "gqp", "gk", "gkp"]
    return pl.pallas_call(
        _inproj_kernel,
        grid=(BATCH, N_TILES),
        in_specs=([_tok_spec(D_MODEL), _mod_spec(), _param_spec(norm_g, l, 1)]
                  + [_param_spec(params[n], l) for n in names] + [tab_spec, tab_spec]),
        out_specs=[
            _tok_spec_t(NA_WIDTH), _tok_spec(NA_WIDTH), _tok_spec_t(NA_WIDTH),
            _tok_spec_t(qk_w), _tok_spec(qk_w), _tok_spec_t(MLA_WIDTH), _tok_spec(S5_WIDTH),
        ],
        out_shape=[
            jax.ShapeDtypeStruct((BATCH, NA_WIDTH, S_ALL), bf16),
            jax.ShapeDtypeStruct((BATCH, S_ALL, NA_WIDTH), bf16),
            jax.ShapeDtypeStruct((BATCH, NA_WIDTH, S_ALL), bf16),
            jax.ShapeDtypeStruct((BATCH, qk_w, S_ALL), bf16),
            jax.ShapeDtypeStruct((BATCH, S_ALL, qk_w), bf16),
            jax.ShapeDtypeStruct((BATCH, MLA_WIDTH, S_ALL), bf16),
            jax.ShapeDtypeStruct((BATCH, S_ALL, S5_WIDTH), f32),
        ],
        compiler_params=pltpu.CompilerParams(
            dimension_semantics=("arbitrary", "arbitrary"), vmem_limit_bytes=VMEM_LIMIT),
        name="in_proj_heads",
    )(h, mod, norm_g, *[params[n] for n in names], cos_t, sin_t)


def _pair_block_diag(qt):
    d = qt.shape[0] // 2
    z = jnp.zeros((d, qt.shape[1]), qt.dtype)
    return jnp.concatenate([jnp.concatenate([qt[:d], z], axis=0),
                            jnp.concatenate([z, qt[d:]], axis=0)], axis=1)


def _pair_scores(dst_ref, key_blocks, q_bd, bias_blocks):
    r = 0
    for kk, bias in zip(key_blocks, bias_blocks):
        s = _dot(kk, q_bd)
        dst_ref[r:r + kk.shape[0], :] = s if bias is None else s + bias
        r += kk.shape[0]


def _pair_softmax_pv(src_ref, vt_blocks):
    n_keys = sum(vt.shape[1] for vt in vt_blocks)
    m = jnp.max(src_ref[0:n_keys, :], axis=0, keepdims=True)
    den = acc = None
    r = 0
    for vt in vt_blocks:
        p = jnp.exp2(src_ref[r:r + vt.shape[1], :] - m)
        li = jnp.sum(p, axis=0, keepdims=True)
        oi = _dot(vt, p.astype(bf16))
        den = li if den is None else den + li
        acc = oi if acc is None else acc + oi
        r += vt.shape[1]
    o = acc / den
    dv = o.shape[0] // 2
    return jnp.concatenate([o[:dv, :TM], o[dv:, TM:]], axis=0).T.astype(bf16)


def _pipelined_tiles(scores, finish, s_a, s_b, first_tile, last_tile):
    scores(first_tile, s_a)

    def body(i, carry):
        t0 = first_tile + 2 * i
        scores(t0 + 1, s_b)
        finish(t0, s_a)
        scores(t0 + 2, s_a)
        finish(t0 + 1, s_b)
        return carry

    lax.fori_loop(0, (last_tile - first_tile + 1) // 2 - 1, body, 0)
    scores(last_tile, s_b)
    finish(last_tile - 1, s_a)
    finish(last_tile, s_b)


NA_TILE_ROWS = TM // GRID_W
NA_UNION_ROWS = 12
NA_UNION_KEYS = NA_UNION_ROWS * GRID_W
NA_PATTERNS = 3
NA_BIAS_SHIFTS = 2 * WIN_ROWS


def _na_window_lo(pat, rr):
    return (-rr, -(WIN_ROWS // 2), -(WIN_ROWS // 2) - rr)[pat]


def _na_fill_bias(blk_ref, bias_scr):
    lane = lax.broadcasted_iota(jnp.int32, (1, LANES), 1)
    left = lane < GRID_W
    neg = jnp.full((GRID_W, LANES), NEG_INF, f32)
    for pat in range(NA_PATTERNS):
        for ii in range(NA_UNION_ROWS):
            for t in range(NA_TILE_ROWS // 2):
                d = ii - NA_TILE_ROWS * pat - 2 * t
                ok_l = 0 <= d - _na_window_lo(pat, 2 * t) < WIN_ROWS
                ok_r = 0 <= (d - 1) - _na_window_lo(pat, 2 * t + 1) < WIN_ROWS
                for hh in range(2):
                    if ok_l or ok_r:
                        blk = blk_ref[0, hh, d + WIN_ROWS - 1]
                        if not ok_l:
                            blk = jnp.where(left, NEG_INF, blk)
                        if not ok_r:
                            blk = jnp.where(left, blk, NEG_INF)
                    else:
                        blk = neg
                    c0 = hh * TM + t * LANES
                    bias_scr[pat, ii * GRID_W:(ii + 1) * GRID_W, c0:c0 + LANES] = blk


def _query_tile(qt_ref, t):
    return _pair_block_diag(qt_ref[0, :, pl.ds(pl.multiple_of(t * TM, TM), TM)])


def _out_rows(t, with_ctx):
    return pl.ds(pl.multiple_of((t - (0 if with_ctx else 1)) * TM, TM), TM)


def _na_kernel(qt_ref, k_ref, vt_ref, blk_ref, o_ref, bias_scr, s_a, s_b, *, with_ctx):
    @pl.when(pl.program_id(1) == 0)
    def _():
        _na_fill_bias(blk_ref, bias_scr)

    kc = k_ref[0, 0:CTX_LEN, :]
    vtc = vt_ref[0, :, 0:CTX_LEN]

    def window(t):
        r0 = (t - 1) * NA_TILE_ROWS
        base = jnp.clip(r0 - WIN_ROWS // 2, 0, GRID_ROWS - NA_UNION_ROWS)
        pat = (r0 - base) // NA_TILE_ROWS
        start = pl.multiple_of(CTX_LEN + base * GRID_W, NA_TILE_ROWS * GRID_W)
        return pat, pl.ds(start, NA_UNION_KEYS)

    def scores(t, dst):
        pat, keys = window(t)
        _pair_scores(dst, [k_ref[0, keys, :], kc], _query_tile(qt_ref, t), [bias_scr[pat], None])

    def finish(t, src):
        _, keys = window(t)
        o_ref[0, _out_rows(t, with_ctx), :] = _pair_softmax_pv(src, [vt_ref[0, :, keys], vtc])

    if with_ctx:
        _pair_scores(s_b, [kc], _query_tile(qt_ref, 0), [None])
        o_ref[0, 0:TM, :] = _pair_softmax_pv(s_b, [vtc])
    _pipelined_tiles(scores, finish, s_a, s_b, 1, N_TILES - 1)


def _na_call(qnt, kn, vnt, bias_blocks, l, with_ctx):
    n_pairs = NA_WIDTH // LANES
    s_out = S_ALL if with_ctx else SEQ
    logits = pltpu.VMEM((NA_UNION_KEYS + CTX_LEN, 2 * TM), f32)
    return pl.pallas_call(
        functools.partial(_na_kernel, with_ctx=with_ctx),
        grid=(n_pairs, BATCH),
        in_specs=[
            pl.BlockSpec((1, LANES, S_ALL), lambda p, b: (b, p, 0)),
            pl.BlockSpec((1, S_ALL, LANES), lambda p, b: (b, 0, p)),
            pl.BlockSpec((1, LANES, S_ALL), lambda p, b: (b, p, 0)),
            pl.BlockSpec((1, 2, NA_BIAS_SHIFTS, GRID_W, LANES), lambda p, b: (l, p, 0, 0, 0)),
        ],
        out_specs=pl.BlockSpec((1, s_out, LANES), lambda p, b: (b, 0, p)),
        out_shape=jax.ShapeDtypeStruct((BATCH, s_out, NA_WIDTH), bf16),
        scratch_shapes=[pltpu.VMEM((NA_PATTERNS, NA_UNION_KEYS, 2 * TM), f32), logits, logits],
        compiler_params=pltpu.CompilerParams(
            dimension_semantics=("arbitrary", "arbitrary"), vmem_limit_bytes=VMEM_LIMIT),
        name="na_attention",
    )(qnt, kn, vnt, bias_blocks)


MLA_KEY_BLOCK = 768


def _mla_kernel(qt_ref, k_ref, vt_ref, o_ref, s_a, s_b, *, with_ctx):
    def scores(t, dst):
        _pair_scores(dst, [k_ref[0]], _query_tile(qt_ref, t), [None])

    def finish(t, src):
        vts = [vt_ref[0, :, s0:s0 + MLA_KEY_BLOCK] for s0 in range(0, S_ALL, MLA_KEY_BLOCK)]
        o_ref[0, _out_rows(t, with_ctx), :] = _pair_softmax_pv(src, vts)

    if with_ctx:
        _pair_scores(s_b, [k_ref[0, 0:CTX_LEN, :]], _query_tile(qt_ref, 0), [None])
        o_ref[0, 0:TM, :] = _pair_softmax_pv(s_b, [vt_ref[0, :, 0:CTX_LEN]])
    _pipelined_tiles(scores, finish, s_a, s_b, 1, N_TILES - 1)


def _mla_call(qmt, km, vmt, with_ctx):
    n_pairs = MLA_HEADS // 2
    s_out = S_ALL if with_ctx else SEQ
    logits = pltpu.VMEM((S_ALL, 2 * TM), f32)
    return pl.pallas_call(
        functools.partial(_mla_kernel, with_ctx=with_ctx),
        grid=(BATCH, n_pairs),
        in_specs=[
            pl.BlockSpec((1, 2 * HEAD_PAD, S_ALL), lambda b, p: (b, p, 0)),
            pl.BlockSpec((1, S_ALL, 2 * HEAD_PAD), lambda b, p: (b, 0, p)),
            pl.BlockSpec((1, LANES, S_ALL), lambda b, p: (b, p, 0)),
        ],
        out_specs=pl.BlockSpec((1, s_out, LANES), lambda b, p: (b, 0, p)),
        out_shape=jax.ShapeDtypeStruct((BATCH, s_out, MLA_WIDTH), bf16),
        scratch_shapes=[logits, logits],
        compiler_params=pltpu.CompilerParams(
            dimension_semantics=("arbitrary", "arbitrary"), vmem_limit_bytes=VMEM_LIMIT),
        name="mla_attention",
    )(qmt, km, vmt)


def _s5_scan_kernel(u_ref, bmat_ref, cmat_ref, lam_ref, y_ref, h_ref, st_ref, *, reverse):
    i = pl.program_id(0)

    @pl.when(i == 0)
    def _():
        st_ref[...] = jnp.zeros_like(st_ref)

    h_ref[...] = _dot(u_ref[...].astype(bf16), bmat_ref[0, 0])

    def step(t, carry):
        hr, hi = carry
        row = (S5_T - 1 - t) if reverse else t
        off = pl.multiple_of(row * BATCH, BATCH)
        lr = lam_ref[0, 0, :, 0:S5_LANES]
        li = lam_ref[0, 0, :, S5_LANES:]
        nr = (lr * hr - li * hi) + h_ref[pl.ds(off, BATCH), 0:S5_LANES]
        ni = (lr * hi + li * hr) + h_ref[pl.ds(off, BATCH), S5_LANES:]
        h_ref[pl.ds(off, BATCH), 0:S5_LANES] = nr
        h_ref[pl.ds(off, BATCH), S5_LANES:] = ni
        return nr, ni

    hr, hi = lax.fori_loop(0, S5_T, step, (st_ref[:, 0:S5_LANES], st_ref[:, S5_LANES:]))
    st_ref[:, 0:S5_LANES] = hr
    st_ref[:, S5_LANES:] = hi

    blk = 256
    for rb in range(S5_ROWS // blk):
        y_ref[rb * blk:(rb + 1) * blk, :] = _dot(
            h_ref[rb * blk:(rb + 1) * blk, :].astype(bf16), cmat_ref[0, 0])


def _s5_scan_call(u_t, bmat, cmat, lam, l, d):
    reverse = d == 1
    if reverse:
        def chunk(i):
            return jnp.where(i < S5_CTX_CHUNKS, S5_CTX_CHUNKS - 1 - i, S5_CHUNKS + S5_CTX_CHUNKS - 1 - i)
    else:
        def chunk(i):
            return i
    return pl.pallas_call(
        functools.partial(_s5_scan_kernel, reverse=reverse),
        grid=(S5_CHUNKS,),
        in_specs=[
            pl.BlockSpec((S5_ROWS, S5_WIDTH), lambda i: (chunk(i), 0)),
            _param_spec(bmat, l, d), _param_spec(cmat, l, d), _param_spec(lam, l, d),
        ],
        out_specs=pl.BlockSpec((S5_ROWS, S5_WIDTH), lambda i: (chunk(i), 0)),
        out_shape=jax.ShapeDtypeStruct((S_ALL * BATCH, S5_WIDTH), f32),
        scratch_shapes=[pltpu.VMEM((S5_ROWS, 2 * S5_LANES), f32), pltpu.VMEM((BATCH, 2 * S5_LANES), f32)],
        compiler_params=pltpu.CompilerParams(
            dimension_semantics=("arbitrary",), vmem_limit_bytes=VMEM_LIMIT),
        name="s5_scan_bwd" if reverse else "s5_scan_fwd",
    )(u_t, bmat, cmat, lam)


def _s5_out_kernel(yf_ref, yb_ref, u_ref, d_ref, w_ref, b_ref, o_ref):
    y = (yf_ref[...] + yb_ref[...]) + d_ref[0] * u_ref[...]
    c0 = math.sqrt(2.0 / math.pi)
    gl = 0.5 * y * (1.0 + jnp.tanh(c0 * (y + 0.044715 * (y * y * y))))
    o = _dot(gl.astype(bf16), w_ref[0]) + b_ref[0]
    o_ref[...] = (o[:, :S5_WIDTH] * _sigmoid(o[:, S5_WIDTH:])).astype(bf16)


def _s5_out_call(yf, yb, u_t, d_skip, w_glu, b_glu, l, with_ctx):
    rows = 1024
    first = 0 if with_ctx else CTX_LEN * BATCH // rows
    in_spec = pl.BlockSpec((rows, S5_WIDTH), lambda i: (i + first, 0))
    row_spec = pl.BlockSpec((rows, S5_WIDTH), lambda i: (i, 0))
    return pl.pallas_call(
        _s5_out_kernel,
        grid=(S_ALL * BATCH // rows - first,),
        in_specs=[in_spec, in_spec, in_spec,
                  _param_spec(d_skip, l), _param_spec(w_glu, l), _param_spec(b_glu, l)],
        out_specs=row_spec,
        out_shape=jax.ShapeDtypeStruct((S_ALL * BATCH - first * rows, S5_WIDTH), bf16),
        compiler_params=pltpu.CompilerParams(
            dimension_semantics=("arbitrary",), vmem_limit_bytes=VMEM_LIMIT),
        name="s5_readout",
    )(yf, yb, u_t, d_skip, w_glu, b_glu)


def _rope_perm():
    half = MLA_ROPE // 2
    quarter = half // 2
    idx, sign = [], []
    for j in range(MLA_ROPE):
        if (j % half) < quarter:
            idx.append(j + quarter)
            sign.append(-1.0)
        else:
            idx.append(j - quarter)
            sign.append(1.0)
    return jnp.array(idx, jnp.int32), jnp.array(sign, f32)


def _rope_tables():
    quarter = MLA_ROPE // 4
    t = jnp.arange(SEQ)
    row = (t // GRID_W).astype(f32)
    col = (t % GRID_W).astype(f32)
    inv = ROPE_THETA ** (-jnp.arange(quarter, dtype=f32) / quarter)
    ang = jnp.concatenate([row[:, None] * inv] * 2 + [col[:, None] * inv] * 2, axis=-1)
    ang = jnp.concatenate([jnp.zeros((CTX_LEN, MLA_ROPE), f32), ang], axis=0)
    ones = jnp.ones((S_ALL, MLA_NOPE), f32)
    pad = jnp.zeros((S_ALL, HEAD_PAD - MLA_QK), f32)
    cos_t = jnp.concatenate([ones, jnp.cos(ang), pad], axis=-1)
    sin_t = jnp.concatenate([0.0 * ones, jnp.sin(ang), pad], axis=-1)
    return cos_t, sin_t


def _na_bias_blocks(rpb):
    cq = jnp.arange(GRID_W)[None, :]
    kc = jnp.arange(GRID_W)[:, None]
    d_col = jnp.clip(kc - cq, -(WIN_COLS - 1), WIN_COLS - 1) + WIN_COLS - 1
    col_start = jnp.clip(cq - WIN_COLS // 2, 0, GRID_W - WIN_COLS)
    in_win = (kc >= col_start) & (kc < col_start + WIN_COLS)
    onehot = (d_col[:, :, None] == jnp.arange(2 * WIN_COLS - 1)[None, None, :]).astype(f32)
    blocks = jnp.einsum('lhdc,kqc->lhdkq', rpb.astype(f32), onehot, precision=lax.Precision.HIGHEST)
    blocks = jnp.where(in_win, blocks * LOG2E, NEG_INF)
    neg = jnp.full(blocks.shape[:2] + (1, GRID_W, GRID_W), NEG_INF, f32)
    padded = jnp.concatenate([neg, blocks, neg], axis=2)
    return jnp.concatenate([padded[:, :, 1:], padded[:, :, :-1]], axis=-1)


def _pad_heads(w, width):
    lead = w.shape[:-1]
    w = w.reshape(lead + (MLA_HEADS, width))
    w = jnp.pad(w, [(0, 0)] * len(lead) + [(0, 0), (0, HEAD_PAD - width)])
    return w.reshape(lead + (MLA_HEADS * HEAD_PAD,))


def _block_diag_groups(w):
    eye = jnp.eye(S5_GROUPS, dtype=f32)
    lead = w.shape[:-3]
    a, b = w.shape[-2:]
    full = w[..., :, :, None, :] * eye[:, None, :, None]
    return full.reshape(lead + (S5_GROUPS * a, S5_GROUPS * b))


def _prepare_params(ffn_w_gu, ffn_w_down, w_in, w_out, na_qk_g, na_rpb, mla_cq_g, mla_ckv_g, mla_w_uq,
                    mla_w_ukv, mla_qk_g, s5_lam_re, s5_lam_im, s5_log_dt, s5_b_re, s5_b_im, s5_c_re,
                    s5_c_im, s5_d, s5_w_glu, s5_b_glu):
    p = {}
    perm_idx, perm_sign = _rope_perm()
    p["wgu"] = ffn_w_gu.astype(bf16)
    p["wd"] = ffn_w_down.astype(bf16)
    p["w_out"] = w_out.astype(bf16)
    o_kr = Z_MAIN
    o_u = o_kr + MLA_ROPE
    w_kr = w_in[:, :, o_kr:o_kr + MLA_ROPE]
    p["win_main"] = w_in[:, :, :Z_MAIN].astype(bf16)
    p["win_tail"] = jnp.concatenate([
        w_in[:, :, o_u:o_u + S5_WIDTH], w_kr, w_kr[:, :, perm_idx] * perm_sign,
        jnp.zeros((DEPTH, D_MODEL, LANES - 2 * MLA_ROPE), f32)], axis=-1).astype(bf16)
    p["naq_g"] = jnp.tile(na_qk_g[:, 0], (1, NA_HEADS))[:, None, :]
    p["nak_g"] = jnp.tile(na_qk_g[:, 1], (1, NA_HEADS))[:, None, :]
    p["cq_g"] = mla_cq_g[:, None, :]
    p["ckv_g"] = mla_ckv_g[:, None, :]
    wuq = mla_w_uq.reshape(DEPTH, Q_LORA, MLA_HEADS, MLA_QK)
    partner = jnp.concatenate([jnp.zeros((DEPTH, Q_LORA, MLA_HEADS, MLA_NOPE), f32),
                               wuq[..., MLA_NOPE:][..., perm_idx] * perm_sign], axis=-1)
    p["wq"] = jnp.concatenate([_pad_heads(mla_w_uq, MLA_QK),
                               _pad_heads(partner.reshape(DEPTH, Q_LORA, -1), MLA_QK)], axis=-1).astype(bf16)
    wukv = mla_w_ukv.reshape(DEPTH, KV_LORA, MLA_HEADS, MLA_NOPE + MLA_V_DIM)
    p["wkv"] = jnp.concatenate([_pad_heads(wukv[..., :MLA_NOPE].reshape(DEPTH, KV_LORA, -1), MLA_NOPE),
                                wukv[..., MLA_NOPE:].reshape(DEPTH, KV_LORA, -1)], axis=-1).astype(bf16)

    def pad_gain(g):
        z = jnp.zeros((DEPTH, HEAD_PAD - MLA_QK), f32)
        full = jnp.concatenate([g, z], axis=-1)[:, None, :]
        part = jnp.concatenate([jnp.zeros((DEPTH, MLA_NOPE), f32), g[:, MLA_NOPE:][:, perm_idx], z],
                               axis=-1)[:, None, :]
        return full, part

    p["gq"], p["gqp"] = pad_gain(mla_qk_g[:, 0])
    p["gk"], p["gkp"] = pad_gain(mla_qk_g[:, 1])
    p["na_bias"] = _na_bias_blocks(na_rpb)
    lre = s5_lam_re.astype(f32)
    lim = s5_lam_im.astype(f32)
    dt = jnp.exp(s5_log_dt.astype(f32))[..., None]
    mag = jnp.exp(lre * dt)
    bar_re = mag * jnp.cos(lim * dt)
    bar_im = mag * jnp.sin(lim * dt)
    den = lre * lre + lim * lim
    q_re = ((bar_re - 1.0) * lre + bar_im * lim) / den
    q_im = (bar_im * lre - (bar_re - 1.0) * lim) / den
    bre = s5_b_re.astype(f32)
    bim = s5_b_im.astype(f32)
    bbar_re = q_re[..., None] * bre - q_im[..., None] * bim
    bbar_im = q_re[..., None] * bim + q_im[..., None] * bre
    p["bmat"] = jnp.concatenate([_block_diag_groups(jnp.swapaxes(bbar_re, -1, -2)),
                                 _block_diag_groups(jnp.swapaxes(bbar_im, -1, -2))], axis=-1).astype(bf16)
    p["cmat"] = jnp.concatenate([_block_diag_groups(jnp.swapaxes(s5_c_re.astype(f32), -1, -2)),
                                 -_block_diag_groups(jnp.swapaxes(s5_c_im.astype(f32), -1, -2))],
                                axis=-2).astype(bf16)
    lam_row = jnp.concatenate([bar_re.reshape(DEPTH, 2, S5_LANES), bar_im.reshape(DEPTH, 2, S5_LANES)], axis=-1)
    p["lam"] = jnp.broadcast_to(lam_row[:, :, None, :], (DEPTH, 2, BATCH, 2 * S5_LANES))
    p["s5_d"] = s5_d[:, None, :]
    p["w_glu"] = s5_w_glu.astype(bf16)
    p["b_glu"] = s5_b_glu[:, None, :]
    return p


def kernel(x, c, ctx, c_ctx, w_mod, b_mod, norm_g, ffn_w_gu, ffn_w_down, w_in, w_out, na_qk_g, na_rpb, mla_cq_g, mla_ckv_g, mla_w_uq, mla_w_ukv, mla_qk_g, s5_lam_re, s5_lam_im, s5_log_dt, s5_b_re, s5_b_im, s5_c_re, s5_c_im, s5_d, s5_w_glu, s5_b_glu):
    assert x.shape == (BATCH, SEQ, D_MODEL) and ctx.shape == (BATCH, CTX_LEN, D_MODEL)
    mod_rows = 16
    cvec = jnp.concatenate([c, c_ctx[None, :], jnp.zeros((mod_rows - BATCH - 1, D_MODEL), f32)], axis=0)
    mod_all = _mod_call(cvec, w_mod, b_mod).reshape(DEPTH, mod_rows, N_MOD, D_MODEL)
    cos_t, sin_t = _rope_tables()
    p = _prepare_params(ffn_w_gu, ffn_w_down, w_in, w_out, na_qk_g, na_rpb, mla_cq_g, mla_ckv_g,
                        mla_w_uq, mla_w_ukv, mla_qk_g, s5_lam_re, s5_lam_im, s5_log_dt, s5_b_re,
                        s5_b_im, s5_c_re, s5_c_im, s5_d, s5_w_glu, s5_b_glu)
    norm_g4 = norm_g[:, :, None, :]
    h = (ctx, x)
    for l in range(DEPTH):
        mod_c = jnp.broadcast_to(mod_all[l, BATCH][None], (BATCH, N_MOD, D_MODEL))
        mod = jnp.stack([mod_c, mod_all[l, :BATCH]], axis=1)
        need_ctx = l < DEPTH - 1
        h = _ffn_call(h, mod, norm_g4, p["wgu"], p["wd"], l, 0)
        qn, kn, vn, qm, km, vm, u = _inproj_call(h, mod, norm_g4, l, p, cos_t, sin_t)
        a = _na_call(qn, kn, vn, p["na_bias"], l, need_ctx)
        bm = _mla_call(qm, km, vm, need_ctx)
        u_t = u.transpose(1, 0, 2).reshape(S_ALL * BATCH, S5_WIDTH)
        yf = _s5_scan_call(u_t, p["bmat"], p["cmat"], p["lam"], l, 0)
        yb = _s5_scan_call(u_t, p["bmat"], p["cmat"], p["lam"], l, 1)
        s_t = _s5_out_call(yf, yb, u_t, p["s5_d"], p["w_glu"], p["b_glu"], l, need_ctx)
        s = s_t.reshape(-1, BATCH, S5_WIDTH).transpose(1, 0, 2)
        h = _ffn_call(h, mod, norm_g4, p["wgu"], p["wd"], l, 1, mixers=(a, bm, s), w_out=p["w_out"],
                      out_ctx=need_ctx)
    return h
```

```python
import functools
import math

import jax
import jax.numpy as jnp
from jax import lax
from jax.experimental import pallas as pl
from jax.experimental.pallas import tpu as pltpu

D_MODEL = 1024
BATCH = 8
SEQ = 2048
DEPTH = 2
CTX_LEN = 256
S_ALL = CTX_LEN + SEQ
GRID_W = 64
GRID_ROWS = SEQ // GRID_W
NA_HEAD_DIM = 64
NA_WIDTH = 384
NA_HEADS = 6
WIN_ROWS = 8
WIN_COLS = 16
MLA_V_DIM = 64
MLA_WIDTH = 384
MLA_HEADS = 6
MLA_NOPE = 64
MLA_ROPE = 32
MLA_QK = 96
Q_LORA = 384
KV_LORA = 256
S5_WIDTH = 256
S5_GROUP = 16
S5_GROUPS = 16
S5_STATE = 64
S5_LANES = S5_GROUPS * S5_STATE
D_FF = 2816
ROPE_THETA = 10000.0
EPS = 1e-6
N_MOD = 9
NEG_INF = -1e30

LANES = 128
TM = 256
N_TILES = S_ALL // TM
MXU_DIM = 256
FF_CHUNKS = ((0, 6 * MXU_DIM), (6 * MXU_DIM, D_FF))
HEAD_PAD = 128
S5_T = 128
S5_ROWS = S5_T * BATCH
S5_CHUNKS = S_ALL // S5_T
S5_CTX_CHUNKS = CTX_LEN // S5_T
VMEM_LIMIT = 56 * 1024 * 1024

ZC_QA, ZC_KA, ZC_VA, ZC_CQ, ZC_CKV = 0, 384, 768, 1152, 1536
Z_MAIN = 1792
Z_TAIL = S5_WIDTH + LANES

LOG2E = math.log2(math.e)
NA_Q_SCALE = NA_HEAD_DIM ** -0.5 * LOG2E
MLA_Q_SCALE = MLA_QK ** -0.5 * LOG2E

f32 = jnp.float32
bf16 = jnp.bfloat16


def _dot(a, b):
    return jnp.dot(a, b, preferred_element_type=f32)


def _sigmoid(x):
    return 1.0 / (1.0 + jnp.exp(-x))


def _rms(x, n):
    return lax.rsqrt(jnp.sum(x * x, axis=-1, keepdims=True) / n + EPS)


def _modulated(x, g, shift, scale):
    y = x * _rms(x, D_MODEL)
    return (y * g) * (1.0 + scale) + shift


def _mod_kernel(c_ref, w_ref, b_ref, o_ref):
    c = c_ref[...]
    a = (c * _sigmoid(c)).astype(bf16)
    o_ref[0] = _dot(a, w_ref[0].astype(bf16)) + b_ref[0]


def _mod_call(cvec, w_mod, b_mod):
    rows = cvec.shape[0]
    return pl.pallas_call(
        _mod_kernel,
        grid=(DEPTH, N_MOD),
        in_specs=[
            pl.BlockSpec((rows, D_MODEL), lambda l, j: (0, 0)),
            pl.BlockSpec((1, D_MODEL, D_MODEL), lambda l, j: (l, 0, j)),
            pl.BlockSpec((1, 1, D_MODEL), lambda l, j: (l, 0, j)),
        ],
        out_specs=pl.BlockSpec((1, rows, D_MODEL), lambda l, j: (l, 0, j)),
        out_shape=jax.ShapeDtypeStruct((DEPTH, rows, N_MOD * D_MODEL), f32),
        compiler_params=pltpu.CompilerParams(
            dimension_semantics=("arbitrary", "arbitrary"), vmem_limit_bytes=VMEM_LIMIT),
        name="adaln_mod",
    )(cvec, w_mod, b_mod.reshape(DEPTH, 1, N_MOD * D_MODEL))


def _mod_spec(first=0):
    return pl.BlockSpec((1, 1, N_MOD, D_MODEL), lambda b, j: (b, jnp.minimum(j + first, 1), 0, 0))


def _tok_spec(width, first=0):
    return pl.BlockSpec((1, TM, width), lambda b, j: (b, j + first, 0))


def _tok_spec_t(width):
    return pl.BlockSpec((1, width, TM), lambda b, j: (b, 0, j))


def _param_spec(arr, *lead, single=False):
    tail = arr.shape[len(lead):]
    idx = tuple(lead) + (0,) * len(tail)
    mode = pl.Buffered(1) if single else None
    return pl.BlockSpec((1,) * len(lead) + tail, lambda *_: idx, pipeline_mode=mode)


FFN_SUBTILES = 2


def _ffn_kernel(*refs, i0, fuse_outproj, split_input, tiles_per_batch, first):
    n_sub = FFN_SUBTILES
    refs = list(refs)
    n_src = 2 if split_input else 1
    srcs = [refs[g * n_src:(g + 1) * n_src] for g in range(n_sub)]
    refs = refs[n_sub * n_src:]
    mods = refs[:n_sub]
    g_ref, wgu_ref, wd_ref = refs[n_sub:n_sub + 3]
    refs = refs[n_sub + 3:]
    if fuse_outproj:
        mixers = [refs[3 * g:3 * g + 3] for g in range(n_sub)]
        wo_ref = refs[3 * n_sub]
        refs = refs[3 * n_sub + 1:]
    o_ref, x_scr, xm_scr = refs

    sels = []
    for g in range(n_sub):
        j = (pl.program_id(0) * n_sub + g) % tiles_per_batch + first
        sel = jnp.minimum(j, 1)
        sels.append(sel)
        rows = slice(g * TM, (g + 1) * TM)
        mod_ref = mods[g]
        if split_input:
            x = jnp.where(j == 0, srcs[g][0][0], srcs[g][1][0])
        else:
            x = srcs[g][0][0]
        if fuse_outproj:
            a_ref, b_ref, s_ref = mixers[g]
            o = _dot(a_ref[0], wo_ref[0, 0:NA_WIDTH, :])
            o += _dot(b_ref[0], wo_ref[0, NA_WIDTH:NA_WIDTH + MLA_WIDTH, :])
            o += _dot(s_ref[0], wo_ref[0, NA_WIDTH + MLA_WIDTH:, :])
            x = x + mod_ref[0, sel, 5:6, :] * o
        x_scr[rows, :] = x
        shift = mod_ref[0, sel, i0:i0 + 1, :]
        scale = mod_ref[0, sel, i0 + 1:i0 + 2, :]
        xm_scr[rows, :] = _modulated(x, g_ref[0, 0], shift, scale).astype(bf16)
    acc = None
    for lo, hi in FF_CHUNKS:
        gt = _dot(xm_scr[...], wgu_ref[0, 0, :, lo:hi])
        up = _dot(xm_scr[...], wgu_ref[0, 0, :, D_FF + lo:D_FF + hi])
        a = ((gt * _sigmoid(gt)) * up).astype(bf16)
        part = _dot(a, wd_ref[0, 0, lo:hi, :])
        acc = part if acc is None else acc + part
    for g in range(n_sub):
        rows = slice(g * TM, (g + 1) * TM)
        gate = mods[g][0, sels[g], i0 + 2:i0 + 3, :]
        o_ref[g] = x_scr[rows, :] + (0.5 * gate) * acc[rows, :]


def _ffn_call(h, mod, norm_g, wgu, wd, l, k, *, mixers=None, w_out=None, out_ctx=True):
    split = isinstance(h, tuple)
    first = 0 if out_ctx else 1
    tpb = N_TILES - first
    n_sub = FFN_SUBTILES

    def tile_of(i, g):
        t = i * n_sub + g
        return t // tpb, t % tpb

    ins, in_specs = [], []
    for g in range(n_sub):
        if split:
            ins += list(h)
            in_specs += [
                pl.BlockSpec((1, TM, D_MODEL), lambda i, g=g: (tile_of(i, g)[0], 0, 0)),
                pl.BlockSpec((1, TM, D_MODEL),
                             lambda i, g=g: (tile_of(i, g)[0], jnp.maximum(tile_of(i, g)[1] + first - 1, 0), 0))]
        else:
            ins.append(h)
            in_specs.append(pl.BlockSpec(
                (1, TM, D_MODEL), lambda i, g=g: (tile_of(i, g)[0], tile_of(i, g)[1] + first, 0)))
    for g in range(n_sub):
        ins.append(mod)
        in_specs.append(pl.BlockSpec((1, 2, N_MOD, D_MODEL), lambda i, g=g: (tile_of(i, g)[0], 0, 0, 0)))
    ins += [norm_g, wgu, wd]
    in_specs += [_param_spec(norm_g, l, 2 * k), _param_spec(wgu, l, k, single=True),
                 _param_spec(wd, l, k, single=True)]
    if mixers is not None:
        for g in range(n_sub):
            for arr in mixers:
                ins.append(arr)
                in_specs.append(pl.BlockSpec((1, TM, arr.shape[-1]), lambda i, g=g: tile_of(i, g) + (0,)))
        ins.append(w_out)
        in_specs.append(_param_spec(w_out, l, single=True))
    rows = n_sub * TM
    out = pl.pallas_call(
        functools.partial(_ffn_kernel, i0=6 * k, fuse_outproj=mixers is not None, split_input=split,
                          tiles_per_batch=tpb, first=first),
        grid=(BATCH * tpb // n_sub,),
        in_specs=in_specs,
        out_specs=pl.BlockSpec((n_sub, TM, D_MODEL), lambda i: (i, 0, 0)),
        out_shape=jax.ShapeDtypeStruct((BATCH * tpb, TM, D_MODEL), f32),
        scratch_shapes=[pltpu.VMEM((rows, D_MODEL), f32), pltpu.VMEM((rows, D_MODEL), bf16)],
        compiler_params=pltpu.CompilerParams(
            dimension_semantics=("arbitrary",), vmem_limit_bytes=VMEM_LIMIT),
        name="swiglu_half_step",
    )(*ins)
    return out.reshape(BATCH, tpb * TM, D_MODEL)


def _pair_head_norm(x, g):
    lane = lax.broadcasted_iota(jnp.int32, (1, LANES), 1)
    lo = lane < NA_HEAD_DIM
    outs = []
    for p in range(NA_WIDTH // LANES):
        xb = x[:, p * LANES:(p + 1) * LANES]
        sq = xb * xb
        s_lo = jnp.sum(jnp.where(lo, sq, 0.0), axis=-1, keepdims=True)
        s_hi = jnp.sum(jnp.where(lo, 0.0, sq), axis=-1, keepdims=True)
        r = jnp.where(lo, lax.rsqrt(s_lo / NA_HEAD_DIM + EPS), lax.rsqrt(s_hi / NA_HEAD_DIM + EPS))
        outs.append((xb * r) * g[:, p * LANES:(p + 1) * LANES])
    return outs


def _inproj_kernel(h_ref, mod_ref, g_ref, wmain_ref, wtail_ref, naq_g_ref, nak_g_ref, cq_g_ref, ckv_g_ref,
                   wq_ref, wkv_ref, gq_ref, gqp_ref, gk_ref, gkp_ref, cos_ref, sin_ref,
                   qn_ref, kn_ref, vn_ref, qm_ref, km_ref, vm_ref, u_ref):
    x = h_ref[0]
    shift = mod_ref[0, 0, 3:4, :]
    scale = mod_ref[0, 0, 4:5, :]
    xm = _modulated(x, g_ref[0, 0], shift, scale).astype(bf16)
    z = _dot(xm, wmain_ref[0])
    zt = _dot(xm, wtail_ref[0])

    qn = _pair_head_norm(z[:, ZC_QA:ZC_QA + NA_WIDTH], naq_g_ref[0])
    kn = _pair_head_norm(z[:, ZC_KA:ZC_KA + NA_WIDTH], nak_g_ref[0])
    for p in range(NA_WIDTH // LANES):
        qn_ref[0, p * LANES:(p + 1) * LANES, :] = (qn[p] * NA_Q_SCALE).T.astype(bf16)
        kn_ref[0, :, p * LANES:(p + 1) * LANES] = kn[p].astype(bf16)
        vn_ref[0, p * LANES:(p + 1) * LANES, :] = z[:, ZC_VA + p * LANES:ZC_VA + (p + 1) * LANES].T.astype(bf16)
    u_ref[0] = zt[:, 0:S5_WIDTH]

    cos_t = cos_ref[...]
    sin_t = sin_ref[...]

    cq = z[:, ZC_CQ:ZC_CQ + Q_LORA]
    ncq = ((cq * _rms(cq, Q_LORA)) * cq_g_ref[0]).astype(bf16)
    qq = _dot(ncq, wq_ref[0])
    gq = gq_ref[0]
    gqp = gqp_ref[0]
    for hd in range(MLA_HEADS):
        pre = qq[:, hd * HEAD_PAD:(hd + 1) * HEAD_PAD]
        perm = qq[:, (MLA_HEADS + hd) * HEAD_PAD:(MLA_HEADS + hd + 1) * HEAD_PAD]
        r = _rms(pre, MLA_QK)
        qm_ref[0, hd * HEAD_PAD:(hd + 1) * HEAD_PAD, :] = (
            ((pre * gq) * cos_t + (perm * gqp) * sin_t) * (r * MLA_Q_SCALE)).T.astype(bf16)

    ckv = z[:, ZC_CKV:ZC_CKV + KV_LORA]
    nkv = ((ckv * _rms(ckv, KV_LORA)) * ckv_g_ref[0]).astype(bf16)
    kv = _dot(nkv, wkv_ref[0])
    for p in range(MLA_WIDTH // LANES):
        lo_col = MLA_HEADS * HEAD_PAD + p * LANES
        vm_ref[0, p * LANES:(p + 1) * LANES, :] = kv[:, lo_col:lo_col + LANES].T.astype(bf16)
    krb = zt[:, S5_WIDTH:S5_WIDTH + LANES]
    lane = lax.broadcasted_iota(jnp.int32, (1, LANES), 1)
    rope_lanes = (lane >= MLA_NOPE) & (lane < MLA_QK)
    kr_a = jnp.where(rope_lanes, pltpu.roll(krb, MLA_NOPE, axis=1), 0.0)
    kr_b = jnp.where(rope_lanes, pltpu.roll(krb, MLA_ROPE, axis=1), 0.0)
    gk = gk_ref[0]
    gkp = gkp_ref[0]
    rot_part = (kr_b * gkp) * sin_t
    for hd in range(MLA_HEADS):
        kfull = kv[:, hd * HEAD_PAD:(hd + 1) * HEAD_PAD] + kr_a
        r = _rms(kfull, MLA_QK)
        km_ref[0, :, hd * HEAD_PAD:(hd + 1) * HEAD_PAD] = (
            ((kfull * gk) * cos_t + rot_part) * r).astype(bf16)


def _inproj_call(h, mod, norm_g, l, params, cos_t, sin_t):
    qk_w = MLA_HEADS * HEAD_PAD
    tab_spec = pl.BlockSpec((TM, LANES), lambda b, j: (j, 0))
    names = ["win_main", "win_tail", "naq_g", "nak_g", "cq_g", "ckv_g", "wq", "wkv", "gq", "gqp", "gk", "gkp"]
    return pl.pallas_call(
        _inproj_kernel,
        grid=(BATCH, N_TILES),
        in_specs=([_tok_spec(D_MODEL), _mod_spec(), _param_spec(norm_g, l, 1)]
                  + [_param_spec(params[n], l) for n in names] + [tab_spec, tab_spec]),
        out_specs=[
            _tok_spec_t(NA_WIDTH), _tok_spec(NA_WIDTH), _tok_spec_t(NA_WIDTH),
            _tok_spec_t(qk_w), _tok_spec(qk_w), _tok_spec_t(MLA_WIDTH), _tok_spec(S5_WIDTH),
        ],
        out_shape=[
            jax.ShapeDtypeStruct((BATCH, NA_WIDTH, S_ALL), bf16),
            jax.ShapeDtypeStruct((BATCH, S_ALL, NA_WIDTH), bf16),
            jax.ShapeDtypeStruct((BATCH, NA_WIDTH, S_ALL), bf16),
            jax.ShapeDtypeStruct((BATCH, qk_w, S_ALL), bf16),
            jax.ShapeDtypeStruct((BATCH, S_ALL, qk_w), bf16),
            jax.ShapeDtypeStruct((BATCH, MLA_WIDTH, S_ALL), bf16),
            jax.ShapeDtypeStruct((BATCH, S_ALL, S5_WIDTH), f32),
        ],
        compiler_params=pltpu.CompilerParams(
            dimension_semantics=("arbitrary", "arbitrary"), vmem_limit_bytes=VMEM_LIMIT),
        name="in_proj_heads",
    )(h, mod, norm_g, *[params[n] for n in names], cos_t, sin_t)


def _pair_block_diag(qt):
    d = qt.shape[0] // 2
    z = jnp.zeros((d, qt.shape[1]), qt.dtype)
    return jnp.concatenate([jnp.concatenate([qt[:d], z], axis=0),
                            jnp.concatenate([z, qt[d:]], axis=0)], axis=1)


def _pair_scores(dst_ref, key_blocks, q_bd, bias_blocks):
    r = 0
    for kk, bias in zip(key_blocks, bias_blocks):
        s = _dot(kk, q_bd)
        dst_ref[r:r + kk.shape[0], :] = s if bias is None else s + bias
        r += kk.shape[0]


def _pair_softmax_pv(src_ref, vt_blocks):
    n_keys = sum(vt.shape[1] for vt in vt_blocks)
    m = jnp.max(src_ref[0:n_keys, :], axis=0, keepdims=True)
    den = acc = None
    r = 0
    for vt in vt_blocks:
        p = jnp.exp2(src_ref[r:r + vt.shape[1], :] - m)
        li = jnp.sum(p, axis=0, keepdims=True)
        oi = _dot(vt, p.astype(bf16))
        den = li if den is None else den + li
        acc = oi if acc is None else acc + oi
        r += vt.shape[1]
    o = acc / den
    dv = o.shape[0] // 2
    return jnp.concatenate([o[:dv, :TM], o[dv:, TM:]], axis=0).T.astype(bf16)


def _pipelined_tiles(scores, finish, s_a, s_b, first_tile, last_tile):
    scores(first_tile, s_a)

    def body(i, carry):
        t0 = first_tile + 2 * i
        scores(t0 + 1, s_b)
        finish(t0, s_a)
        scores(t0 + 2, s_a)
        finish(t0 + 1, s_b)
        return carry

    lax.fori_loop(0, (last_tile - first_tile + 1) // 2 - 1, body, 0)
    scores(last_tile, s_b)
    finish(last_tile - 1, s_a)
    finish(last_tile, s_b)


NA_TILE_ROWS = TM // GRID_W
NA_UNION_ROWS = 12
NA_UNION_KEYS = NA_UNION_ROWS * GRID_W
NA_PATTERNS = 3
NA_BIAS_SHIFTS = 2 * WIN_ROWS


def _na_window_lo(pat, rr):
    return (-rr, -(WIN_ROWS // 2), -(WIN_ROWS // 2) - rr)[pat]


def _na_fill_bias(blk_ref, bias_scr):
    lane = lax.broadcasted_iota(jnp.int32, (1, LANES), 1)
    left = lane < GRID_W
    neg = jnp.full((GRID_W, LANES), NEG_INF, f32)
    for pat in range(NA_PATTERNS):
        for ii in range(NA_UNION_ROWS):
            for t in range(NA_TILE_ROWS // 2):
                d = ii - NA_TILE_ROWS * pat - 2 * t
                ok_l = 0 <= d - _na_window_lo(pat, 2 * t) < WIN_ROWS
                ok_r = 0 <= (d - 1) - _na_window_lo(pat, 2 * t + 1) < WIN_ROWS
                for hh in range(2):
                    if ok_l or ok_r:
                        blk = blk_ref[0, hh, d + WIN_ROWS - 1]
                        if not ok_l:
                            blk = jnp.where(left, NEG_INF, blk)
                        if not ok_r:
                            blk = jnp.where(left, blk, NEG_INF)
                    else:
                        blk = neg
                    c0 = hh * TM + t * LANES
                    bias_scr[pat, ii * GRID_W:(ii + 1) * GRID_W, c0:c0 + LANES] = blk


def _query_tile(qt_ref, t):
    return _pair_block_diag(qt_ref[0, :, pl.ds(pl.multiple_of(t * TM, TM), TM)])


def _out_rows(t, with_ctx):
    return pl.ds(pl.multiple_of((t - (0 if with_ctx else 1)) * TM, TM), TM)


def _na_kernel(qt_ref, k_ref, vt_ref, blk_ref, o_ref, bias_scr, s_a, s_b, *, with_ctx):
    @pl.when(pl.program_id(1) == 0)
    def _():
        _na_fill_bias(blk_ref, bias_scr)

    kc = k_ref[0, 0:CTX_LEN, :]
    vtc = vt_ref[0, :, 0:CTX_LEN]

    def window(t):
        r0 = (t - 1) * NA_TILE_ROWS
        base = jnp.clip(r0 - WIN_ROWS // 2, 0, GRID_ROWS - NA_UNION_ROWS)
        pat = (r0 - base) // NA_TILE_ROWS
        start = pl.multiple_of(CTX_LEN + base * GRID_W, NA_TILE_ROWS * GRID_W)
        return pat, pl.ds(start, NA_UNION_KEYS)

    def scores(t, dst):
        pat, keys = window(t)
        _pair_scores(dst, [k_ref[0, keys, :], kc], _query_tile(qt_ref, t), [bias_scr[pat], None])

    def finish(t, src):
        _, keys = window(t)
        o_ref[0, _out_rows(t, with_ctx), :] = _pair_softmax_pv(src, [vt_ref[0, :, keys], vtc])

    if with_ctx:
        _pair_scores(s_b, [kc], _query_tile(qt_ref, 0), [None])
        o_ref[0, 0:TM, :] = _pair_softmax_pv(s_b, [vtc])
    _pipelined_tiles(scores, finish, s_a, s_b, 1, N_TILES - 1)


def _na_call(qnt, kn, vnt, bias_blocks, l, with_ctx):
    n_pairs = NA_WIDTH // LANES
    s_out = S_ALL if with_ctx else SEQ
    logits = pltpu.VMEM((NA_UNION_KEYS + CTX_LEN, 2 * TM), f32)
    return pl.pallas_call(
        functools.partial(_na_kernel, with_ctx=with_ctx),
        grid=(n_pairs, BATCH),
        in_specs=[
            pl.BlockSpec((1, LANES, S_ALL), lambda p, b: (b, p, 0)),
            pl.BlockSpec((1, S_ALL, LANES), lambda p, b: (b, 0, p)),
            pl.BlockSpec((1, LANES, S_ALL), lambda p, b: (b, p, 0)),
            pl.BlockSpec((1, 2, NA_BIAS_SHIFTS, GRID_W, LANES), lambda p, b: (l, p, 0, 0, 0)),
        ],
        out_specs=pl.BlockSpec((1, s_out, LANES), lambda p, b: (b, 0, p)),
        out_shape=jax.ShapeDtypeStruct((BATCH, s_out, NA_WIDTH), bf16),
        scratch_shapes=[pltpu.VMEM((NA_PATTERNS, NA_UNION_KEYS, 2 * TM), f32), logits, logits],
        compiler_params=pltpu.CompilerParams(
            dimension_semantics=("arbitrary", "arbitrary"), vmem_limit_bytes=VMEM_LIMIT),
        name="na_attention",
    )(qnt, kn, vnt, bias_blocks)


MLA_KEY_BLOCK = 768


def _mla_kernel(qt_ref, k_ref, vt_ref, o_ref, s_a, s_b, *, with_ctx):
    def scores(t, dst):
        _pair_scores(dst, [k_ref[0]], _query_tile(qt_ref, t), [None])

    def finish(t, src):
        vts = [vt_ref[0, :, s0:s0 + MLA_KEY_BLOCK] for s0 in range(0, S_ALL, MLA_KEY_BLOCK)]
        o_ref[0, _out_rows(t, with_ctx), :] = _pair_softmax_pv(src, vts)

    if with_ctx:
        _pair_scores(s_b, [k_ref[0, 0:CTX_LEN, :]], _query_tile(qt_ref, 0), [None])
        o_ref[0, 0:TM, :] = _pair_softmax_pv(s_b, [vt_ref[0, :, 0:CTX_LEN]])
    _pipelined_tiles(scores, finish, s_a, s_b, 1, N_TILES - 1)


def _mla_call(qmt, km, vmt, with_ctx):
    n_pairs = MLA_HEADS // 2
    s_out = S_ALL if with_ctx else SEQ
    logits = pltpu.VMEM((S_ALL, 2 * TM), f32)
    return pl.pallas_call(
        functools.partial(_mla_kernel, with_ctx=with_ctx),
        grid=(BATCH, n_pairs),
        in_specs=[
            pl.BlockSpec((1, 2 * HEAD_PAD, S_ALL), lambda b, p: (b, p, 0)),
            pl.BlockSpec((1, S_ALL, 2 * HEAD_PAD), lambda b, p: (b, 0, p)),
            pl.BlockSpec((1, LANES, S_ALL), lambda b, p: (b, p, 0)),
        ],
        out_specs=pl.BlockSpec((1, s_out, LANES), lambda b, p: (b, 0, p)),
        out_shape=jax.ShapeDtypeStruct((BATCH, s_out, MLA_WIDTH), bf16),
        scratch_shapes=[logits, logits],
        compiler_params=pltpu.CompilerParams(
            dimension_semantics=("arbitrary", "arbitrary"), vmem_limit_bytes=VMEM_LIMIT),
        name="mla_attention",
    )(qmt, km, vmt)


def _s5_scan_kernel(u_ref, bmat_ref, cmat_ref, lam_ref, y_ref, h_ref, st_ref, *, reverse):
    i = pl.program_id(0)

    @pl.when(i == 0)
    def _():
        st_ref[...] = jnp.zeros_like(st_ref)

    h_ref[...] = _dot(u_ref[...].astype(bf16), bmat_ref[0, 0])

    def step(t, carry):
        hr, hi = carry
        row = (S5_T - 1 - t) if reverse else t
        off = pl.multiple_of(row * BATCH, BATCH)
        lr = lam_ref[0, 0, :, 0:S5_LANES]
        li = lam_ref[0, 0, :, S5_LANES:]
        nr = (lr * hr - li * hi) + h_ref[pl.ds(off, BATCH), 0:S5_LANES]
        ni = (lr * hi + li * hr) + h_ref[pl.ds(off, BATCH), S5_LANES:]
        h_ref[pl.ds(off, BATCH), 0:S5_LANES] = nr
        h_ref[pl.ds(off, BATCH), S5_LANES:] = ni
        return nr, ni

    hr, hi = lax.fori_loop(0, S5_T, step, (st_ref[:, 0:S5_LANES], st_ref[:, S5_LANES:]))
    st_ref[:, 0:S5_LANES] = hr
    st_ref[:, S5_LANES:] = hi

    blk = 256
    for rb in range(S5_ROWS // blk):
        y_ref[rb * blk:(rb + 1) * blk, :] = _dot(
            h_ref[rb * blk:(rb + 1) * blk, :].astype(bf16), cmat_ref[0, 0])


def _s5_scan_call(u_t, bmat, cmat, lam, l, d):
    reverse = d == 1
    if reverse:
        def chunk(i):
            return jnp.where(i < S5_CTX_CHUNKS, S5_CTX_CHUNKS - 1 - i, S5_CHUNKS + S5_CTX_CHUNKS - 1 - i)
    else:
        def chunk(i):
            return i
    return pl.pallas_call(
        functools.partial(_s5_scan_kernel, reverse=reverse),
        grid=(S5_CHUNKS,),
        in_specs=[
            pl.BlockSpec((S5_ROWS, S5_WIDTH), lambda i: (chunk(i), 0)),
            _param_spec(bmat, l, d), _param_spec(cmat, l, d), _param_spec(lam, l, d),
        ],
        out_specs=pl.BlockSpec((S5_ROWS, S5_WIDTH), lambda i: (chunk(i), 0)),
        out_shape=jax.ShapeDtypeStruct((S_ALL * BATCH, S5_WIDTH), f32),
        scratch_shapes=[pltpu.VMEM((S5_ROWS, 2 * S5_LANES), f32), pltpu.VMEM((BATCH, 2 * S5_LANES), f32)],
        compiler_params=pltpu.CompilerParams(
            dimension_semantics=("arbitrary",), vmem_limit_bytes=VMEM_LIMIT),
        name="s5_scan_bwd" if reverse else "s5_scan_fwd",
    )(u_t, bmat, cmat, lam)


def _s5_out_kernel(yf_ref, yb_ref, u_ref, d_ref, w_ref, b_ref, o_ref):
    y = (yf_ref[...] + yb_ref[...]) + d_ref[0] * u_ref[...]
    c0 = math.sqrt(2.0 / math.pi)
    gl = 0.5 * y * (1.0 + jnp.tanh(c0 * (y + 0.044715 * (y * y * y))))
    o = _dot(gl.astype(bf16), w_ref[0]) + b_ref[0]
    o_ref[...] = (o[:, :S5_WIDTH] * _sigmoid(o[:, S5_WIDTH:])).astype(bf16)


def _s5_out_call(yf, yb, u_t, d_skip, w_glu, b_glu, l, with_ctx):
    rows = 1024
    first = 0 if with_ctx else CTX_LEN * BATCH // rows
    in_spec = pl.BlockSpec((rows, S5_WIDTH), lambda i: (i + first, 0))
    row_spec = pl.BlockSpec((rows, S5_WIDTH), lambda i: (i, 0))
    return pl.pallas_call(
        _s5_out_kernel,
        grid=(S_ALL * BATCH // rows - first,),
        in_specs=[in_spec, in_spec, in_spec,
                  _param_spec(d_skip, l), _param_spec(w_glu, l), _param_spec(b_glu, l)],
        out_specs=row_spec,
        out_shape=jax.ShapeDtypeStruct((S_ALL * BATCH - first * rows, S5_WIDTH), bf16),
        compiler_params=pltpu.CompilerParams(
            dimension_semantics=("arbitrary",), vmem_limit_bytes=VMEM_LIMIT),
        name="s5_readout",
    )(yf, yb, u_t, d_skip, w_glu, b_glu)


def _rope_perm():
    half = MLA_ROPE // 2
    quarter = half // 2
    idx, sign = [], []
    for j in range(MLA_ROPE):
        if (j % half) < quarter:
            idx.append(j + quarter)
            sign.append(-1.0)
        else:
            idx.append(j - quarter)
            sign.append(1.0)
    return jnp.array(idx, jnp.int32), jnp.array(sign, f32)


def _rope_tables():
    quarter = MLA_ROPE // 4
    t = jnp.arange(SEQ)
    row = (t // GRID_W).astype(f32)
    col = (t % GRID_W).astype(f32)
    inv = ROPE_THETA ** (-jnp.arange(quarter, dtype=f32) / quarter)
    ang = jnp.concatenate([row[:, None] * inv] * 2 + [col[:, None] * inv] * 2, axis=-1)
    ang = jnp.concatenate([jnp.zeros((CTX_LEN, MLA_ROPE), f32), ang], axis=0)
    ones = jnp.ones((S_ALL, MLA_NOPE), f32)
    pad = jnp.zeros((S_ALL, HEAD_PAD - MLA_QK), f32)
    cos_t = jnp.concatenate([ones, jnp.cos(ang), pad], axis=-1)
    sin_t = jnp.concatenate([0.0 * ones, jnp.sin(ang), pad], axis=-1)
    return cos_t, sin_t


def _na_bias_blocks(rpb):
    cq = jnp.arange(GRID_W)[None, :]
    kc = jnp.arange(GRID_W)[:, None]
    d_col = jnp.clip(kc - cq, -(WIN_COLS - 1), WIN_COLS - 1) + WIN_COLS - 1
    col_start = jnp.clip(cq - WIN_COLS // 2, 0, GRID_W - WIN_COLS)
    in_win = (kc >= col_start) & (kc < col_start + WIN_COLS)
    onehot = (d_col[:, :, None] == jnp.arange(2 * WIN_COLS - 1)[None, None, :]).astype(f32)
    blocks = jnp.einsum('lhdc,kqc->lhdkq', rpb.astype(f32), onehot, precision=lax.Precision.HIGHEST)
    blocks = jnp.where(in_win, blocks * LOG2E, NEG_INF)
    neg = jnp.full(blocks.shape[:2] + (1, GRID_W, GRID_W), NEG_INF, f32)
    padded = jnp.concatenate([neg, blocks, neg], axis=2)
    return jnp.concatenate([padded[:, :, 1:], padded[:, :, :-1]], axis=-1)


def _pad_heads(w, width):
    lead = w.shape[:-1]
    w = w.reshape(lead + (MLA_HEADS, width))
    w = jnp.pad(w, [(0, 0)] * len(lead) + [(0, 0), (0, HEAD_PAD - width)])
    return w.reshape(lead + (MLA_HEADS * HEAD_PAD,))


def _block_diag_groups(w):
    eye = jnp.eye(S5_GROUPS, dtype=f32)
    lead = w.shape[:-3]
    a, b = w.shape[-2:]
    full = w[..., :, :, None, :] * eye[:, None, :, None]
    return full.reshape(lead + (S5_GROUPS * a, S5_GROUPS * b))


def _prepare_params(ffn_w_gu, ffn_w_down, w_in, w_out, na_qk_g, na_rpb, mla_cq_g, mla_ckv_g, mla_w_uq,
                    mla_w_ukv, mla_qk_g, s5_lam_re, s5_lam_im, s5_log_dt, s5_b_re, s5_b_im, s5_c_re,
                    s5_c_im, s5_d, s5_w_glu, s5_b_glu):
    p = {}
    perm_idx, perm_sign = _rope_perm()
    p["wgu"] = ffn_w_gu.astype(bf16)
    p["wd"] = ffn_w_down.astype(bf16)
    p["w_out"] = w_out.astype(bf16)
    o_kr = Z_MAIN
    o_u = o_kr + MLA_ROPE
    w_kr = w_in[:, :, o_kr:o_kr + MLA_ROPE]
    p["win_main"] = w_in[:, :, :Z_MAIN].astype(bf16)
    p["win_tail"] = jnp.concatenate([
        w_in[:, :, o_u:o_u + S5_WIDTH], w_kr, w_kr[:, :, perm_idx] * perm_sign,
        jnp.zeros((DEPTH, D_MODEL, LANES - 2 * MLA_ROPE), f32)], axis=-1).astype(bf16)
    p["naq_g"] = jnp.tile(na_qk_g[:, 0], (1, NA_HEADS))[:, None, :]
    p["nak_g"] = jnp.tile(na_qk_g[:, 1], (1, NA_HEADS))[:, None, :]
    p["cq_g"] = mla_cq_g[:, None, :]
    p["ckv_g"] = mla_ckv_g[:, None, :]
    wuq = mla_w_uq.reshape(DEPTH, Q_LORA, MLA_HEADS, MLA_QK)
    partner = jnp.concatenate([jnp.zeros((DEPTH, Q_LORA, MLA_HEADS, MLA_NOPE), f32),
                               wuq[..., MLA_NOPE:][..., perm_idx] * perm_sign], axis=-1)
    p["wq"] = jnp.concatenate([_pad_heads(mla_w_uq, MLA_QK),
                               _pad_heads(partner.reshape(DEPTH, Q_LORA, -1), MLA_QK)], axis=-1).astype(bf16)
    wukv = mla_w_ukv.reshape(DEPTH, KV_LORA, MLA_HEADS, MLA_NOPE + MLA_V_DIM)
    p["wkv"] = jnp.concatenate([_pad_heads(wukv[..., :MLA_NOPE].reshape(DEPTH, KV_LORA, -1), MLA_NOPE),
                                wukv[..., MLA_NOPE:].reshape(DEPTH, KV_LORA, -1)], axis=-1).astype(bf16)

    def pad_gain(g):
        z = jnp.zeros((DEPTH, HEAD_PAD - MLA_QK), f32)
        full = jnp.concatenate([g, z], axis=-1)[:, None, :]
        part = jnp.concatenate([jnp.zeros((DEPTH, MLA_NOPE), f32), g[:, MLA_NOPE:][:, perm_idx], z],
                               axis=-1)[:, None, :]
        return full, part

    p["gq"], p["gqp"] = pad_gain(mla_qk_g[:, 0])
    p["gk"], p["gkp"] = pad_gain(mla_qk_g[:, 1])
    p["na_bias"] = _na_bias_blocks(na_rpb)
    lre = s5_lam_re.astype(f32)
    lim = s5_lam_im.astype(f32)
    dt = jnp.exp(s5_log_dt.astype(f32))[..., None]
    mag = jnp.exp(lre * dt)
    bar_re = mag * jnp.cos(lim * dt)
    bar_im = mag * jnp.sin(lim * dt)
    den = lre * lre + lim * lim
    q_re = ((bar_re - 1.0) * lre + bar_im * lim) / den
    q_im = (bar_im * lre - (bar_re - 1.0) * lim) / den
    bre = s5_b_re.astype(f32)
    bim = s5_b_im.astype(f32)
    bbar_re = q_re[..., None] * bre - q_im[..., None] * bim
    bbar_im = q_re[..., None] * bim + q_im[..., None] * bre
    p["bmat"] = jnp.concatenate([_block_diag_groups(jnp.swapaxes(bbar_re, -1, -2)),
                                 _block_diag_groups(jnp.swapaxes(bbar_im, -1, -2))], axis=-1).astype(bf16)
    p["cmat"] = jnp.concatenate([_block_diag_groups(jnp.swapaxes(s5_c_re.astype(f32), -1, -2)),
                                 -_block_diag_groups(jnp.swapaxes(s5_c_im.astype(f32), -1, -2))],
                                axis=-2).astype(bf16)
    lam_row = jnp.concatenate([bar_re.reshape(DEPTH, 2, S5_LANES), bar_im.reshape(DEPTH, 2, S5_LANES)], axis=-1)
    p["lam"] = jnp.broadcast_to(lam_row[:, :, None, :], (DEPTH, 2, BATCH, 2 * S5_LANES))
    p["s5_d"] = s5_d[:, None, :]
    p["w_glu"] = s5_w_glu.astype(bf16)
    p["b_glu"] = s5_b_glu[:, None, :]
    return p


def kernel(x, c, ctx, c_ctx, w_mod, b_mod, norm_g, ffn_w_gu, ffn_w_down, w_in, w_out, na_qk_g, na_rpb, mla_cq_g, mla_ckv_g, mla_w_uq, mla_w_ukv, mla_qk_g, s5_lam_re, s5_lam_im, s5_log_dt, s5_b_re, s5_b_im, s5_c_re, s5_c_im, s5_d, s5_w_glu, s5_b_glu):
    assert x.shape == (BATCH, SEQ, D_MODEL) and ctx.shape == (BATCH, CTX_LEN, D_MODEL)
    mod_rows = 16
    cvec = jnp.concatenate([c, c_ctx[None, :], jnp.zeros((mod_rows - BATCH - 1, D_MODEL), f32)], axis=0)
    mod_all = _mod_call(cvec, w_mod, b_mod).reshape(DEPTH, mod_rows, N_MOD, D_MODEL)
    cos_t, sin_t = _rope_tables()
    p = _prepare_params(ffn_w_gu, ffn_w_down, w_in, w_out, na_qk_g, na_rpb, mla_cq_g, mla_ckv_g,
                        mla_w_uq, mla_w_ukv, mla_qk_g, s5_lam_re, s5_lam_im, s5_log_dt, s5_b_re,
                        s5_b_im, s5_c_re, s5_c_im, s5_d, s5_w_glu, s5_b_glu)
    norm_g4 = norm_g[:, :, None, :]
    h = (ctx, x)
    for l in range(DEPTH):
        mod_c = jnp.broadcast_to(mod_all[l, BATCH][None], (BATCH, N_MOD, D_MODEL))
        mod = jnp.stack([mod_c, mod_all[l, :BATCH]], axis=1)
        need_ctx = l < DEPTH - 1
        h = _ffn_call(h, mod, norm_g4, p["wgu"], p["wd"], l, 0)
        qn, kn, vn, qm, km, vm, u = _inproj_call(h, mod, norm_g4, l, p, cos_t, sin_t)
        a = _na_call(qn, kn, vn, p["na_bias"], l, need_ctx)
        bm = _mla_call(qm, km, vm, need_ctx)
        u_t = u.transpose(1, 0, 2).reshape(S_ALL * BATCH, S5_WIDTH)
        yf = _s5_scan_call(u_t, p["bmat"], p["cmat"], p["lam"], l, 0)
        yb = _s5_scan_call(u_t, p["bmat"], p["cmat"], p["lam"], l, 1)
        s_t = _s5_out_call(yf, yb, u_t, p["s5_d"], p["w_glu"], p["b_glu"], l, need_ctx)
        s = s_t.reshape(-1, BATCH, S5_WIDTH).transpose(1, 0, 2)
        h = _ffn_call(h, mod, norm_g4, p["wgu"], p["wd"], l, 1, mixers=(a, bm, s), w_out=p["w_out"],
                      out_ctx=need_ctx)
    return h
```

```python
import functools
import math

import jax
import jax.numpy as jnp
from jax import lax
from jax.experimental import pallas as pl
from jax.experimental.pallas import tpu as pltpu

D_MODEL = 1024
BATCH = 8
SEQ = 2048
DEPTH = 2
CTX_LEN = 256
S_ALL = CTX_LEN + SEQ
GRID_W = 64
GRID_ROWS = SEQ // GRID_W
NA_HEAD_DIM = 64
NA_WIDTH = 384
NA_HEADS = 6
WIN_ROWS = 8
WIN_COLS = 16
MLA_V_DIM = 64
MLA_WIDTH = 384
MLA_HEADS = 6
MLA_NOPE = 64
MLA_ROPE = 32
MLA_QK = 96
Q_LORA = 384
KV_LORA = 256
S5_WIDTH = 256
S5_GROUP = 16
S5_GROUPS = 16
S5_STATE = 64
S5_LANES = S5_GROUPS * S5_STATE
D_FF = 2816
ROPE_THETA = 10000.0
EPS = 1e-6
N_MOD = 9
NEG_INF = -1e30

LANES = 128
TM = 256
N_TILES = S_ALL // TM
MXU_DIM = 256
FF_CHUNKS = ((0, 6 * MXU_DIM), (6 * MXU_DIM, D_FF))
HEAD_PAD = 128
S5_T = 128
S5_ROWS = S5_T * BATCH
S5_CHUNKS = S_ALL // S5_T
S5_CTX_CHUNKS = CTX_LEN // S5_T
VMEM_LIMIT = 56 * 1024 * 1024

ZC_QA, ZC_KA, ZC_VA, ZC_CQ, ZC_CKV = 0, 384, 768, 1152, 1536
Z_MAIN = 1792
Z_TAIL = S5_WIDTH + LANES

LOG2E = math.log2(math.e)
NA_Q_SCALE = NA_HEAD_DIM ** -0.5 * LOG2E
MLA_Q_SCALE = MLA_QK ** -0.5 * LOG2E

f32 = jnp.float32
bf16 = jnp.bfloat16


def _dot(a, b):
    return jnp.dot(a, b, preferred_element_type=f32)


def _sigmoid(x):
    return 1.0 / (1.0 + jnp.exp(-x))


def _rms(x, n):
    return lax.rsqrt(jnp.sum(x * x, axis=-1, keepdims=True) / n + EPS)


def _modulated(x, g, shift, scale):
    y = x * _rms(x, D_MODEL)
    return (y * g) * (1.0 + scale) + shift


def _mod_kernel(c_ref, w_ref, b_ref, o_ref):
    c = c_ref[...]
    a = (c * _sigmoid(c)).astype(bf16)
    o_ref[0] = _dot(a, w_ref[0].astype(bf16)) + b_ref[0]


def _mod_call(cvec, w_mod, b_mod):
    rows = cvec.shape[0]
    return pl.pallas_call(
        _mod_kernel,
        grid=(DEPTH, N_MOD),
        in_specs=[
            pl.BlockSpec((rows, D_MODEL), lambda l, j: (0, 0)),
            pl.BlockSpec((1, D_MODEL, D_MODEL), lambda l, j: (l, 0, j)),
            pl.BlockSpec((1, 1, D_MODEL), lambda l, j: (l, 0, j)),
        ],
        out_specs=pl.BlockSpec((1, rows, D_MODEL), lambda l, j: (l, 0, j)),
        out_shape=jax.ShapeDtypeStruct((DEPTH, rows, N_MOD * D_MODEL), f32),
        compiler_params=pltpu.CompilerParams(
            dimension_semantics=("arbitrary", "arbitrary"), vmem_limit_bytes=VMEM_LIMIT),
        name="adaln_mod",
    )(cvec, w_mod, b_mod.reshape(DEPTH, 1, N_MOD * D_MODEL))


def _param_spec(arr, *lead, single=False):
    tail = arr.shape[len(lead):]
    idx = tuple(lead) + (0,) * len(tail)
    mode = pl.Buffered(1) if single else None
    return pl.BlockSpec((1,) * len(lead) + tail, lambda *_: idx, pipeline_mode=mode)


FFN_SUBTILES = 2


def _ffn_kernel(*refs, i0, fuse_outproj, split_input, tiles_per_batch, first):
    n_sub = FFN_SUBTILES
    refs = list(refs)
    n_src = 2 if split_input else 1
    srcs = [refs[g * n_src:(g + 1) * n_src] for g in range(n_sub)]
    refs = refs[n_sub * n_src:]
    mods = refs[:n_sub]
    g_ref, wgu_ref, wd_ref = refs[n_sub:n_sub + 3]
    refs = refs[n_sub + 3:]
    if fuse_outproj:
        mixers = [refs[3 * g:3 * g + 3] for g in range(n_sub)]
        wo_ref = refs[3 * n_sub]
        refs = refs[3 * n_sub + 1:]
    o_ref, x_scr, xm_scr = refs

    sels = []
    for g in range(n_sub):
        j = (pl.program_id(0) * n_sub + g) % tiles_per_batch + first
        sel = jnp.minimum(j, 1)
        sels.append(sel)
        rows = slice(g * TM, (g + 1) * TM)
        mod_ref = mods[g]
        if split_input:
            x = jnp.where(j == 0, srcs[g][0][0], srcs[g][1][0])
        else:
            x = srcs[g][0][0]
        if fuse_outproj:
            a_ref, b_ref, s_ref = mixers[g]
            o = _dot(a_ref[0], wo_ref[0, 0:NA_WIDTH, :])
            o += _dot(b_ref[0], wo_ref[0, NA_WIDTH:NA_WIDTH + MLA_WIDTH, :])
            o += _dot(s_ref[0], wo_ref[0, NA_WIDTH + MLA_WIDTH:, :])
            x = x + mod_ref[0, sel, 5:6, :] * o
        x_scr[rows, :] = x
        shift = mod_ref[0, sel, i0:i0 + 1, :]
        scale = mod_ref[0, sel, i0 + 1:i0 + 2, :]
        xm_scr[rows, :] = _modulated(x, g_ref[0, 0], shift, scale).astype(bf16)
    acc = None
    for lo, hi in FF_CHUNKS:
        gt = _dot(xm_scr[...], wgu_ref[0, 0, :, lo:hi])
        up = _dot(xm_scr[...], wgu_ref[0, 0, :, D_FF + lo:D_FF + hi])
        a = ((gt * _sigmoid(gt)) * up).astype(bf16)
        part = _dot(a, wd_ref[0, 0, lo:hi, :])
        acc = part if acc is None else acc + part
    for g in range(n_sub):
        rows = slice(g * TM, (g + 1) * TM)
        gate = mods[g][0, sels[g], i0 + 2:i0 + 3, :]
        o_ref[g] = x_scr[rows, :] + (0.5 * gate) * acc[rows, :]


def _ffn_call(h, mod, norm_g, wgu, wd, l, k, *, mixers=None, w_out=None, out_ctx=True):
    split = isinstance(h, tuple)
    first = 0 if out_ctx else 1
    tpb = N_TILES - first
    n_sub = FFN_SUBTILES

    def tile_of(i, g):
        t = i * n_sub + g
        return t // tpb, t % tpb

    ins, in_specs = [], []
    for g in range(n_sub):
        if split:
            ins += list(h)
            in_specs += [
                pl.BlockSpec((1, TM, D_MODEL), lambda i, g=g: (tile_of(i, g)[0], 0, 0)),
                pl.BlockSpec((1, TM, D_MODEL),
                             lambda i, g=g: (tile_of(i, g)[0], jnp.maximum(tile_of(i, g)[1] + first - 1, 0), 0))]
        else:
            ins.append(h)
            in_specs.append(pl.BlockSpec(
                (1, TM, D_MODEL), lambda i, g=g: (tile_of(i, g)[0], tile_of(i, g)[1] + first, 0)))
    for g in range(n_sub):
        ins.append(mod)
        in_specs.append(pl.BlockSpec((1, 2, N_MOD, D_MODEL), lambda i, g=g: (tile_of(i, g)[0], 0, 0, 0)))
    ins += [norm_g, wgu, wd]
    in_specs += [_param_spec(norm_g, l, 2 * k), _param_spec(wgu, l, k, single=True),
                 _param_spec(wd, l, k, single=True)]
    if mixers is not None:
        for g in range(n_sub):
            for arr in mixers:
                ins.append(arr)
                in_specs.append(pl.BlockSpec((1, TM, arr.shape[-1]), lambda i, g=g: tile_of(i, g) + (0,)))
        ins.append(w_out)
        in_specs.append(_param_spec(w_out, l, single=True))
    rows = n_sub * TM
    out = pl.pallas_call(
        functools.partial(_ffn_kernel, i0=6 * k, fuse_outproj=mixers is not None, split_input=split,
                          tiles_per_batch=tpb, first=first),
        grid=(BATCH * tpb // n_sub,),
        in_specs=in_specs,
        out_specs=pl.BlockSpec((n_sub, TM, D_MODEL), lambda i: (i, 0, 0)),
        out_shape=jax.ShapeDtypeStruct((BATCH * tpb, TM, D_MODEL), f32),
        scratch_shapes=[pltpu.VMEM((rows, D_MODEL), f32), pltpu.VMEM((rows, D_MODEL), bf16)],
        compiler_params=pltpu.CompilerParams(
            dimension_semantics=("arbitrary",), vmem_limit_bytes=VMEM_LIMIT),
        name="swiglu_half_step",
    )(*ins)
    return out.reshape(BATCH, tpb * TM, D_MODEL)


INPROJ_SUBTILES = 3


def _pair_head_norm(x, g):
    lane = lax.broadcasted_iota(jnp.int32, (1, LANES), 1)
    lo = lane < NA_HEAD_DIM
    outs = []
    for p in range(NA_WIDTH // LANES):
        xb = x[:, p * LANES:(p + 1) * LANES]
        sq = xb * xb
        s_lo = jnp.sum(jnp.where(lo, sq, 0.0), axis=-1, keepdims=True)
        s_hi = jnp.sum(jnp.where(lo, 0.0, sq), axis=-1, keepdims=True)
        r = jnp.where(lo, lax.rsqrt(s_lo / NA_HEAD_DIM + EPS), lax.rsqrt(s_hi / NA_HEAD_DIM + EPS))
        outs.append((xb * r) * g[:, p * LANES:(p + 1) * LANES])
    return outs


def _inproj_kernel(h_ref, mod_ref, g_ref, wmain_ref, wtail_ref, naq_g_ref, nak_g_ref, cq_g_ref, ckv_g_ref,
                   wq_ref, wkv_ref, gq_ref, gqp_ref, gk_ref, gkp_ref, cos_ref, sin_ref,
                   qn_ref, kn_ref, vn_ref, qm_ref, km_ref, vm_ref, u_ref, z_scr, zt_scr):
    def project(g):
        rows = slice(g * TM, (g + 1) * TM)
        sel = jnp.minimum(pl.program_id(1) * INPROJ_SUBTILES + g, 1)
        xm = _modulated(h_ref[0, rows, :], g_ref[0, 0], mod_ref[0, sel, 3:4, :],
                        mod_ref[0, sel, 4:5, :]).astype(bf16)
        z_scr[rows, :] = _dot(xm, wmain_ref[0])
        zt_scr[rows, :] = _dot(xm, wtail_ref[0])

    def heads(g):
        rows = slice(g * TM, (g + 1) * TM)
        _inproj_heads(rows, z_scr, zt_scr, naq_g_ref, nak_g_ref, cq_g_ref, ckv_g_ref, wq_ref, wkv_ref,
                      gq_ref, gqp_ref, gk_ref, gkp_ref, cos_ref, sin_ref,
                      qn_ref, kn_ref, vn_ref, qm_ref, km_ref, vm_ref, u_ref)

    project(0)
    for g in range(INPROJ_SUBTILES):
        if g + 1 < INPROJ_SUBTILES:
            project(g + 1)
        heads(g)


def _inproj_heads(rows, z_scr, zt_scr, naq_g_ref, nak_g_ref, cq_g_ref, ckv_g_ref, wq_ref, wkv_ref,
                  gq_ref, gqp_ref, gk_ref, gkp_ref, cos_ref, sin_ref,
                  qn_ref, kn_ref, vn_ref, qm_ref, km_ref, vm_ref, u_ref):
    z = z_scr[rows, :]
    zt = zt_scr[rows, :]

    qn = _pair_head_norm(z[:, ZC_QA:ZC_QA + NA_WIDTH], naq_g_ref[0])
    kn = _pair_head_norm(z[:, ZC_KA:ZC_KA + NA_WIDTH], nak_g_ref[0])
    for p in range(NA_WIDTH // LANES):
        qn_ref[0, p * LANES:(p + 1) * LANES, rows] = (qn[p] * NA_Q_SCALE).T.astype(bf16)
        kn_ref[0, rows, p * LANES:(p + 1) * LANES] = kn[p].astype(bf16)
        vn_ref[0, p * LANES:(p + 1) * LANES, rows] = z[:, ZC_VA + p * LANES:ZC_VA + (p + 1) * LANES].T.astype(bf16)
    u_ref[0, rows, :] = zt[:, 0:S5_WIDTH]

    cos_t = cos_ref[rows, :]
    sin_t = sin_ref[rows, :]

    cq = z[:, ZC_CQ:ZC_CQ + Q_LORA]
    ncq = ((cq * _rms(cq, Q_LORA)) * cq_g_ref[0]).astype(bf16)
    qq = _dot(ncq, wq_ref[0])
    gq = gq_ref[0]
    gqp = gqp_ref[0]
    for hd in range(MLA_HEADS):
        pre = qq[:, hd * HEAD_PAD:(hd + 1) * HEAD_PAD]
        perm = qq[:, (MLA_HEADS + hd) * HEAD_PAD:(MLA_HEADS + hd + 1) * HEAD_PAD]
        r = _rms(pre, MLA_QK)
        qm_ref[0, hd * HEAD_PAD:(hd + 1) * HEAD_PAD, rows] = (
            ((pre * gq) * cos_t + (perm * gqp) * sin_t) * (r * MLA_Q_SCALE)).T.astype(bf16)

    ckv = z[:, ZC_CKV:ZC_CKV + KV_LORA]
    nkv = ((ckv * _rms(ckv, KV_LORA)) * ckv_g_ref[0]).astype(bf16)
    kv = _dot(nkv, wkv_ref[0])
    for p in range(MLA_WIDTH // LANES):
        lo_col = MLA_HEADS * HEAD_PAD + p * LANES
        vm_ref[0, p * LANES:(p + 1) * LANES, rows] = kv[:, lo_col:lo_col + LANES].T.astype(bf16)
    krb = zt[:, S5_WIDTH:S5_WIDTH + LANES]
    lane = lax.broadcasted_iota(jnp.int32, (1, LANES), 1)
    rope_lanes = (lane >= MLA_NOPE) & (lane < MLA_QK)
    kr_a = jnp.where(rope_lanes, pltpu.roll(krb, MLA_NOPE, axis=1), 0.0)
    kr_b = jnp.where(rope_lanes, pltpu.roll(krb, MLA_ROPE, axis=1), 0.0)
    gk = gk_ref[0]
    gkp = gkp_ref[0]
    rot_part = (kr_b * gkp) * sin_t
    for hd in range(MLA_HEADS):
        kfull = kv[:, hd * HEAD_PAD:(hd + 1) * HEAD_PAD] + kr_a
        r = _rms(kfull, MLA_QK)
        km_ref[0, rows, hd * HEAD_PAD:(hd + 1) * HEAD_PAD] = (
            ((kfull * gk) * cos_t + rot_part) * r).astype(bf16)


def _inproj_call(h, mod, norm_g, l, params, cos_t, sin_t):
    qk_w = MLA_HEADS * HEAD_PAD
    rows = INPROJ_SUBTILES * TM

    def tok(width):
        return pl.BlockSpec((1, rows, width), lambda b, j: (b, j, 0))

    def tok_t(width):
        return pl.BlockSpec((1, width, rows), lambda b, j: (b, 0, j))

    tab_spec = pl.BlockSpec((rows, LANES), lambda b, j: (j, 0))
    mod_spec = pl.BlockSpec((1, 2, N_MOD, D_MODEL), lambda b, j: (b, 0, 0, 0))
    names = ["win_main", "win_tail", "naq_g", "nak_g", "cq_g", "ckv_g", "wq", "wkv", "gq", "gqp", "gk", "gkp"]
    return pl.pallas_call(
        _inproj_kernel,
        grid=(BATCH, N_TILES // INPROJ_SUBTILES),
        in_specs=([tok(D_MODEL), mod_spec, _param_spec(norm_g, l, 1)]
                  + [_param_spec(params[n], l) for n in names] + [tab_spec, tab_spec]),
        out_specs=[tok_t(NA_WIDTH), tok(NA_WIDTH), tok_t(NA_WIDTH),
                   tok_t(qk_w), tok(qk_w), tok_t(MLA_WIDTH), tok(S5_WIDTH)],
        out_shape=[
            jax.ShapeDtypeStruct((BATCH, NA_WIDTH, S_ALL), bf16),
            jax.ShapeDtypeStruct((BATCH, S_ALL, NA_WIDTH), bf16),
            jax.ShapeDtypeStruct((BATCH, NA_WIDTH, S_ALL), bf16),
            jax.ShapeDtypeStruct((BATCH, qk_w, S_ALL), bf16),
            jax.ShapeDtypeStruct((BATCH, S_ALL, qk_w), bf16),
            jax.ShapeDtypeStruct((BATCH, MLA_WIDTH, S_ALL), bf16),
            jax.ShapeDtypeStruct((BATCH, S_ALL, S5_WIDTH), f32),
        ],
        scratch_shapes=[pltpu.VMEM((rows, Z_MAIN), f32), pltpu.VMEM((rows, Z_TAIL), f32)],
        compiler_params=pltpu.CompilerParams(
            dimension_semantics=("arbitrary", "arbitrary"), vmem_limit_bytes=VMEM_LIMIT),
        name="in_proj_heads",
    )(h, mod, norm_g, *[params[n] for n in names], cos_t, sin_t)


def _pair_block_diag(qt):
    d = qt.shape[0] // 2
    z = jnp.zeros((d, qt.shape[1]), qt.dtype)
    return jnp.concatenate([jnp.concatenate([qt[:d], z], axis=0),
                            jnp.concatenate([z, qt[d:]], axis=0)], axis=1)


def _pair_scores(dst_ref, key_blocks, q_bd, bias_blocks):
    r = 0
    for kk, bias in zip(key_blocks, bias_blocks):
        s = _dot(kk, q_bd)
        dst_ref[r:r + kk.shape[0], :] = s if bias is None else s + bias
        r += kk.shape[0]


def _pair_softmax_pv(src_ref, vt_blocks):
    n_keys = sum(vt.shape[1] for vt in vt_blocks)
    m = jnp.max(src_ref[0:n_keys, :], axis=0, keepdims=True)
    den = acc = None
    r = 0
    for vt in vt_blocks:
        p = jnp.exp2(src_ref[r:r + vt.shape[1], :] - m)
        li = jnp.sum(p, axis=0, keepdims=True)
        oi = _dot(vt, p.astype(bf16))
        den = li if den is None else den + li
        acc = oi if acc is None else acc + oi
        r += vt.shape[1]
    o = acc / den
    dv = o.shape[0] // 2
    return jnp.concatenate([o[:dv, :TM], o[dv:, TM:]], axis=0).T.astype(bf16)


def _pipelined_tiles(scores, finish, s_a, s_b, first_tile, last_tile):
    scores(first_tile, s_a)

    def body(i, carry):
        t0 = first_tile + 2 * i
        scores(t0 + 1, s_b)
        finish(t0, s_a)
        scores(t0 + 2, s_a)
        finish(t0 + 1, s_b)
        return carry

    lax.fori_loop(0, (last_tile - first_tile + 1) // 2 - 1, body, 0)
    scores(last_tile, s_b)
    finish(last_tile - 1, s_a)
    finish(last_tile, s_b)


NA_TILE_ROWS = TM // GRID_W
NA_UNION_ROWS = 12
NA_UNION_KEYS = NA_UNION_ROWS * GRID_W
NA_PATTERNS = 3
NA_BIAS_SHIFTS = 2 * WIN_ROWS


def _na_window_lo(pat, rr):
    return (-rr, -(WIN_ROWS // 2), -(WIN_ROWS // 2) - rr)[pat]


def _na_fill_bias(blk_ref, bias_scr):
    lane = lax.broadcasted_iota(jnp.int32, (1, LANES), 1)
    left = lane < GRID_W
    neg = jnp.full((GRID_W, LANES), NEG_INF, f32)
    for pat in range(NA_PATTERNS):
        for ii in range(NA_UNION_ROWS):
            for t in range(NA_TILE_ROWS // 2):
                d = ii - NA_TILE_ROWS * pat - 2 * t
                ok_l = 0 <= d - _na_window_lo(pat, 2 * t) < WIN_ROWS
                ok_r = 0 <= (d - 1) - _na_window_lo(pat, 2 * t + 1) < WIN_ROWS
                for hh in range(2):
                    if ok_l or ok_r:
                        blk = blk_ref[0, hh, d + WIN_ROWS - 1]
                        if not ok_l:
                            blk = jnp.where(left, NEG_INF, blk)
                        if not ok_r:
                            blk = jnp.where(left, blk, NEG_INF)
                    else:
                        blk = neg
                    c0 = hh * TM + t * LANES
                    bias_scr[pat, ii * GRID_W:(ii + 1) * GRID_W, c0:c0 + LANES] = blk


def _query_tile(qt_ref, t):
    return _pair_block_diag(qt_ref[0, :, pl.ds(pl.multiple_of(t * TM, TM), TM)])


def _out_rows(t, with_ctx):
    return pl.ds(pl.multiple_of((t - (0 if with_ctx else 1)) * TM, TM), TM)


def _na_kernel(qt_ref, k_ref, vt_ref, blk_ref, o_ref, bias_scr, s_a, s_b, *, with_ctx):
    @pl.when(pl.program_id(1) == 0)
    def _():
        _na_fill_bias(blk_ref, bias_scr)

    kc = k_ref[0, 0:CTX_LEN, :]
    vtc = vt_ref[0, :, 0:CTX_LEN]

    def window(t):
        r0 = (t - 1) * NA_TILE_ROWS
        base = jnp.clip(r0 - WIN_ROWS // 2, 0, GRID_ROWS - NA_UNION_ROWS)
        pat = (r0 - base) // NA_TILE_ROWS
        start = pl.multiple_of(CTX_LEN + base * GRID_W, NA_TILE_ROWS * GRID_W)
        return pat, pl.ds(start, NA_UNION_KEYS)

    def scores(t, dst):
        pat, keys = window(t)
        _pair_scores(dst, [k_ref[0, keys, :], kc], _query_tile(qt_ref, t), [bias_scr[pat], None])

    def finish(t, src):
        _, keys = window(t)
        o_ref[0, _out_rows(t, with_ctx), :] = _pair_softmax_pv(src, [vt_ref[0, :, keys], vtc])

    if with_ctx:
        _pair_scores(s_b, [kc], _query_tile(qt_ref, 0), [None])
        o_ref[0, 0:TM, :] = _pair_softmax_pv(s_b, [vtc])
    _pipelined_tiles(scores, finish, s_a, s_b, 1, N_TILES - 1)


def _na_call(qnt, kn, vnt, bias_blocks, l, with_ctx):
    n_pairs = NA_WIDTH // LANES
    s_out = S_ALL if with_ctx else SEQ
    logits = pltpu.VMEM((NA_UNION_KEYS + CTX_LEN, 2 * TM), f32)
    return pl.pallas_call(
        functools.partial(_na_kernel, with_ctx=with_ctx),
        grid=(n_pairs, BATCH),
        in_specs=[
            pl.BlockSpec((1, LANES, S_ALL), lambda p, b: (b, p, 0)),
            pl.BlockSpec((1, S_ALL, LANES), lambda p, b: (b, 0, p)),
            pl.BlockSpec((1, LANES, S_ALL), lambda p, b: (b, p, 0)),
            pl.BlockSpec((1, 2, NA_BIAS_SHIFTS, GRID_W, LANES), lambda p, b: (l, p, 0, 0, 0)),
        ],
        out_specs=pl.BlockSpec((1, s_out, LANES), lambda p, b: (b, 0, p)),
        out_shape=jax.ShapeDtypeStruct((BATCH, s_out, NA_WIDTH), bf16),
        scratch_shapes=[pltpu.VMEM((NA_PATTERNS, NA_UNION_KEYS, 2 * TM), f32), logits, logits],
        compiler_params=pltpu.CompilerParams(
            dimension_semantics=("arbitrary", "arbitrary"), vmem_limit_bytes=VMEM_LIMIT),
        name="na_attention",
    )(qnt, kn, vnt, bias_blocks)


MLA_KEY_BLOCK = 768


def _mla_kernel(qt_ref, k_ref, vt_ref, o_ref, s_a, s_b, *, with_ctx):
    def scores(t, dst):
        _pair_scores(dst, [k_ref[0]], _query_tile(qt_ref, t), [None])

    def finish(t, src):
        vts = [vt_ref[0, :, s0:s0 + MLA_KEY_BLOCK] for s0 in range(0, S_ALL, MLA_KEY_BLOCK)]
        o_ref[0, _out_rows(t, with_ctx), :] = _pair_softmax_pv(src, vts)

    if with_ctx:
        _pair_scores(s_b, [k_ref[0, 0:CTX_LEN, :]], _query_tile(qt_ref, 0), [None])
        o_ref[0, 0:TM, :] = _pair_softmax_pv(s_b, [vt_ref[0, :, 0:CTX_LEN]])
    _pipelined_tiles(scores, finish, s_a, s_b, 1, N_TILES - 1)


def _mla_call(qmt, km, vmt, with_ctx):
    n_pairs = MLA_HEADS // 2
    s_out = S_ALL if with_ctx else SEQ
    logits = pltpu.VMEM((S_ALL, 2 * TM), f32)
    return pl.pallas_call(
        functools.partial(_mla_kernel, with_ctx=with_ctx),
        grid=(BATCH, n_pairs),
        in_specs=[
            pl.BlockSpec((1, 2 * HEAD_PAD, S_ALL), lambda b, p: (b, p, 0)),
            pl.BlockSpec((1, S_ALL, 2 * HEAD_PAD), lambda b, p: (b, 0, p)),
            pl.BlockSpec((1, LANES, S_ALL), lambda b, p: (b, p, 0)),
        ],
        out_specs=pl.BlockSpec((1, s_out, LANES), lambda b, p: (b, 0, p)),
        out_shape=jax.ShapeDtypeStruct((BATCH, s_out, MLA_WIDTH), bf16),
        scratch_shapes=[logits, logits],
        compiler_params=pltpu.CompilerParams(
            dimension_semantics=("arbitrary", "arbitrary"), vmem_limit_bytes=VMEM_LIMIT),
        name="mla_attention",
    )(qmt, km, vmt)


S5_BLOCK_T = 16
S5_BLOCK_ROWS = S5_BLOCK_T * BATCH


def _s5_scan_kernel(u_ref, bmat_ref, cmat_ref, lam_ref, y_ref, bu_ref, h_ref, st_ref, *, reverse):
    @pl.when(pl.program_id(0) == 0)
    def _():
        st_ref[...] = jnp.zeros_like(st_ref)

    def rows(k):
        return slice(k * S5_BLOCK_ROWS, (k + 1) * S5_BLOCK_ROWS)

    def project(k):
        bu_ref[rows(k), :] = _dot(u_ref[rows(k), :].astype(bf16), bmat_ref[0, 0])

    def readout(k):
        y_ref[rows(k), :] = _dot(h_ref[rows(k), :].astype(bf16), cmat_ref[0, 0])

    n_blocks = S5_T // S5_BLOCK_T
    order = list(range(n_blocks))[::-1] if reverse else list(range(n_blocks))
    steps = list(range(S5_BLOCK_T))[::-1] if reverse else list(range(S5_BLOCK_T))
    hr = st_ref[:, 0:S5_LANES]
    hi = st_ref[:, S5_LANES:]
    project(order[0])
    for n, k in enumerate(order):
        if n + 1 < n_blocks:
            project(order[n + 1])
        for t in steps:
            r0 = k * S5_BLOCK_ROWS + t * BATCH
            lr = lam_ref[0, 0, :, 0:S5_LANES]
            li = lam_ref[0, 0, :, S5_LANES:]
            nr = (lr * hr - li * hi) + bu_ref[r0:r0 + BATCH, 0:S5_LANES]
            ni = (lr * hi + li * hr) + bu_ref[r0:r0 + BATCH, S5_LANES:]
            h_ref[r0:r0 + BATCH, 0:S5_LANES] = nr
            h_ref[r0:r0 + BATCH, S5_LANES:] = ni
            hr, hi = nr, ni
        if n >= 1:
            readout(order[n - 1])
    readout(order[-1])
    st_ref[:, 0:S5_LANES] = hr
    st_ref[:, S5_LANES:] = hi


def _s5_scan_call(u_t, bmat, cmat, lam, l, d):
    reverse = d == 1
    if reverse:
        def chunk(i):
            return jnp.where(i < S5_CTX_CHUNKS, S5_CTX_CHUNKS - 1 - i, S5_CHUNKS + S5_CTX_CHUNKS - 1 - i)
    else:
        def chunk(i):
            return i
    return pl.pallas_call(
        functools.partial(_s5_scan_kernel, reverse=reverse),
        grid=(S5_CHUNKS,),
        in_specs=[
            pl.BlockSpec((S5_ROWS, S5_WIDTH), lambda i: (chunk(i), 0)),
            _param_spec(bmat, l, d), _param_spec(cmat, l, d), _param_spec(lam, l, d),
        ],
        out_specs=pl.BlockSpec((S5_ROWS, S5_WIDTH), lambda i: (chunk(i), 0)),
        out_shape=jax.ShapeDtypeStruct((S_ALL * BATCH, S5_WIDTH), f32),
        scratch_shapes=[pltpu.VMEM((S5_ROWS, 2 * S5_LANES), f32), pltpu.VMEM((S5_ROWS, 2 * S5_LANES), f32),
                        pltpu.VMEM((BATCH, 2 * S5_LANES), f32)],
        compiler_params=pltpu.CompilerParams(
            dimension_semantics=("arbitrary",), vmem_limit_bytes=VMEM_LIMIT),
        name="s5_scan_bwd" if reverse else "s5_scan_fwd",
    )(u_t, bmat, cmat, lam)


def _s5_out_kernel(yf_ref, yb_ref, u_ref, d_ref, w_ref, b_ref, o_ref):
    y = (yf_ref[...] + yb_ref[...]) + d_ref[0] * u_ref[...]
    c0 = math.sqrt(2.0 / math.pi)
    gl = 0.5 * y * (1.0 + jnp.tanh(c0 * (y + 0.044715 * (y * y * y))))
    o = _dot(gl.astype(bf16), w_ref[0]) + b_ref[0]
    o_ref[...] = (o[:, :S5_WIDTH] * _sigmoid(o[:, S5_WIDTH:])).astype(bf16)


def _s5_out_call(yf, yb, u_t, d_skip, w_glu, b_glu, l, with_ctx):
    rows = 1024
    first = 0 if with_ctx else CTX_LEN * BATCH // rows
    in_spec = pl.BlockSpec((rows, S5_WIDTH), lambda i: (i + first, 0))
    row_spec = pl.BlockSpec((rows, S5_WIDTH), lambda i: (i, 0))
    return pl.pallas_call(
        _s5_out_kernel,
        grid=(S_ALL * BATCH // rows - first,),
        in_specs=[in_spec, in_spec, in_spec,
                  _param_spec(d_skip, l), _param_spec(w_glu, l), _param_spec(b_glu, l)],
        out_specs=row_spec,
        out_shape=jax.ShapeDtypeStruct((S_ALL * BATCH - first * rows, S5_WIDTH), bf16),
        compiler_params=pltpu.CompilerParams(
            dimension_semantics=("arbitrary",), vmem_limit_bytes=VMEM_LIMIT),
        name="s5_readout",
    )(yf, yb, u_t, d_skip, w_glu, b_glu)


def _rope_perm():
    half = MLA_ROPE // 2
    quarter = half // 2
    idx, sign = [], []
    for j in range(MLA_ROPE):
        if (j % half) < quarter:
            idx.append(j + quarter)
            sign.append(-1.0)
        else:
            idx.append(j - quarter)
            sign.append(1.0)
    return jnp.array(idx, jnp.int32), jnp.array(sign, f32)


def _rope_tables():
    quarter = MLA_ROPE // 4
    t = jnp.arange(SEQ)
    row = (t // GRID_W).astype(f32)
    col = (t % GRID_W).astype(f32)
    inv = ROPE_THETA ** (-jnp.arange(quarter, dtype=f32) / quarter)
    ang = jnp.concatenate([row[:, None] * inv] * 2 + [col[:, None] * inv] * 2, axis=-1)
    ang = jnp.concatenate([jnp.zeros((CTX_LEN, MLA_ROPE), f32), ang], axis=0)
    ones = jnp.ones((S_ALL, MLA_NOPE), f32)
    pad = jnp.zeros((S_ALL, HEAD_PAD - MLA_QK), f32)
    cos_t = jnp.concatenate([ones, jnp.cos(ang), pad], axis=-1)
    sin_t = jnp.concatenate([0.0 * ones, jnp.sin(ang), pad], axis=-1)
    return cos_t, sin_t


def _na_bias_blocks(rpb):
    cq = jnp.arange(GRID_W)[None, :]
    kc = jnp.arange(GRID_W)[:, None]
    d_col = jnp.clip(kc - cq, -(WIN_COLS - 1), WIN_COLS - 1) + WIN_COLS - 1
    col_start = jnp.clip(cq - WIN_COLS // 2, 0, GRID_W - WIN_COLS)
    in_win = (kc >= col_start) & (kc < col_start + WIN_COLS)
    onehot = (d_col[:, :, None] == jnp.arange(2 * WIN_COLS - 1)[None, None, :]).astype(f32)
    blocks = jnp.einsum('lhdc,kqc->lhdkq', rpb.astype(f32), onehot, precision=lax.Precision.HIGHEST)
    blocks = jnp.where(in_win, blocks * LOG2E, NEG_INF)
    neg = jnp.full(blocks.shape[:2] + (1, GRID_W, GRID_W), NEG_INF, f32)
    padded = jnp.concatenate([neg, blocks, neg], axis=2)
    return jnp.concatenate([padded[:, :, 1:], padded[:, :, :-1]], axis=-1)


def _pad_heads(w, width):
    lead = w.shape[:-1]
    w = w.reshape(lead + (MLA_HEADS, width))
    w = jnp.pad(w, [(0, 0)] * len(lead) + [(0, 0), (0, HEAD_PAD - width)])
    return w.reshape(lead + (MLA_HEADS * HEAD_PAD,))


def _block_diag_groups(w):
    eye = jnp.eye(S5_GROUPS, dtype=f32)
    lead = w.shape[:-3]
    a, b = w.shape[-2:]
    full = w[..., :, :, None, :] * eye[:, None, :, None]
    return full.reshape(lead + (S5_GROUPS * a, S5_GROUPS * b))


def _prepare_params(ffn_w_gu, ffn_w_down, w_in, w_out, na_qk_g, na_rpb, mla_cq_g, mla_ckv_g, mla_w_uq,
                    mla_w_ukv, mla_qk_g, s5_lam_re, s5_lam_im, s5_log_dt, s5_b_re, s5_b_im, s5_c_re,
                    s5_c_im, s5_d, s5_w_glu, s5_b_glu):
    p = {}
    perm_idx, perm_sign = _rope_perm()
    p["wgu"] = ffn_w_gu.astype(bf16)
    p["wd"] = ffn_w_down.astype(bf16)
    p["w_out"] = w_out.astype(bf16)
    o_kr = Z_MAIN
    o_u = o_kr + MLA_ROPE
    w_kr = w_in[:, :, o_kr:o_kr + MLA_ROPE]
    p["win_main"] = w_in[:, :, :Z_MAIN].astype(bf16)
    p["win_tail"] = jnp.concatenate([
        w_in[:, :, o_u:o_u + S5_WIDTH], w_kr, w_kr[:, :, perm_idx] * perm_sign,
        jnp.zeros((DEPTH, D_MODEL, LANES - 2 * MLA_ROPE), f32)], axis=-1).astype(bf16)
    p["naq_g"] = jnp.tile(na_qk_g[:, 0], (1, NA_HEADS))[:, None, :]
    p["nak_g"] = jnp.tile(na_qk_g[:, 1], (1, NA_HEADS))[:, None, :]
    p["cq_g"] = mla_cq_g[:, None, :]
    p["ckv_g"] = mla_ckv_g[:, None, :]
    wuq = mla_w_uq.reshape(DEPTH, Q_LORA, MLA_HEADS, MLA_QK)
    partner = jnp.concatenate([jnp.zeros((DEPTH, Q_LORA, MLA_HEADS, MLA_NOPE), f32),
                               wuq[..., MLA_NOPE:][..., perm_idx] * perm_sign], axis=-1)
    p["wq"] = jnp.concatenate([_pad_heads(mla_w_uq, MLA_QK),
                               _pad_heads(partner.reshape(DEPTH, Q_LORA, -1), MLA_QK)], axis=-1).astype(bf16)
    wukv = mla_w_ukv.reshape(DEPTH, KV_LORA, MLA_HEADS, MLA_NOPE + MLA_V_DIM)
    p["wkv"] = jnp.concatenate([_pad_heads(wukv[..., :MLA_NOPE].reshape(DEPTH, KV_LORA, -1), MLA_NOPE),
                                wukv[..., MLA_NOPE:].reshape(DEPTH, KV_LORA, -1)], axis=-1).astype(bf16)

    def pad_gain(g):
        z = jnp.zeros((DEPTH, HEAD_PAD - MLA_QK), f32)
        full = jnp.concatenate([g, z], axis=-1)[:, None, :]
        part = jnp.concatenate([jnp.zeros((DEPTH, MLA_NOPE), f32), g[:, MLA_NOPE:][:, perm_idx], z],
                               axis=-1)[:, None, :]
        return full, part

    p["gq"], p["gqp"] = pad_gain(mla_qk_g[:, 0])
    p["gk"], p["gkp"] = pad_gain(mla_qk_g[:, 1])
    p["na_bias"] = _na_bias_blocks(na_rpb)
    lre = s5_lam_re.astype(f32)
    lim = s5_lam_im.astype(f32)
    dt = jnp.exp(s5_log_dt.astype(f32))[..., None]
    mag = jnp.exp(lre * dt)
    bar_re = mag * jnp.cos(lim * dt)
    bar_im = mag * jnp.sin(lim * dt)
    den = lre * lre + lim * lim
    q_re = ((bar_re - 1.0) * lre + bar_im * lim) / den
    q_im = (bar_im * lre - (bar_re - 1.0) * lim) / den
    bre = s5_b_re.astype(f32)
    bim = s5_b_im.astype(f32)
    bbar_re = q_re[..., None] * bre - q_im[..., None] * bim
    bbar_im = q_re[..., None] * bim + q_im[..., None] * bre
    p["bmat"] = jnp.concatenate([_block_diag_groups(jnp.swapaxes(bbar_re, -1, -2)),
                                 _block_diag_groups(jnp.swapaxes(bbar_im, -1, -2))], axis=-1).astype(bf16)
    p["cmat"] = jnp.concatenate([_block_diag_groups(jnp.swapaxes(s5_c_re.astype(f32), -1, -2)),
                                 -_block_diag_groups(jnp.swapaxes(s5_c_im.astype(f32), -1, -2))],
                                axis=-2).astype(bf16)
    lam_row = jnp.concatenate([bar_re.reshape(DEPTH, 2, S5_LANES), bar_im.reshape(DEPTH, 2, S5_LANES)], axis=-1)
    p["lam"] = jnp.broadcast_to(lam_row[:, :, None, :], (DEPTH, 2, BATCH, 2 * S5_LANES))
    p["s5_d"] = s5_d[:, None, :]
    p["w_glu"] = s5_w_glu.astype(bf16)
    p["b_glu"] = s5_b_glu[:, None, :]
    return p


def kernel(x, c, ctx, c_ctx, w_mod, b_mod, norm_g, ffn_w_gu, ffn_w_down, w_in, w_out, na_qk_g, na_rpb, mla_cq_g, mla_ckv_g, mla_w_uq, mla_w_ukv, mla_qk_g, s5_lam_re, s5_lam_im, s5_log_dt, s5_b_re, s5_b_im, s5_c_re, s5_c_im, s5_d, s5_w_glu, s5_b_glu):
    assert x.shape == (BATCH, SEQ, D_MODEL) and ctx.shape == (BATCH, CTX_LEN, D_MODEL)
    mod_rows = 16
    cvec = jnp.concatenate([c, c_ctx[None, :], jnp.zeros((mod_rows - BATCH - 1, D_MODEL), f32)], axis=0)
    mod_all = _mod_call(cvec, w_mod, b_mod).reshape(DEPTH, mod_rows, N_MOD, D_MODEL)
    cos_t, sin_t = _rope_tables()
    p = _prepare_params(ffn_w_gu, ffn_w_down, w_in, w_out, na_qk_g, na_rpb, mla_cq_g, mla_ckv_g,
                        mla_w_uq, mla_w_ukv, mla_qk_g, s5_lam_re, s5_lam_im, s5_log_dt, s5_b_re,
                        s5_b_im, s5_c_re, s5_c_im, s5_d, s5_w_glu, s5_b_glu)
    norm_g4 = norm_g[:, :, None, :]
    h = (ctx, x)
    for l in range(DEPTH):
        mod_c = jnp.broadcast_to(mod_all[l, BATCH][None], (BATCH, N_MOD, D_MODEL))
        mod = jnp.stack([mod_c, mod_all[l, :BATCH]], axis=1)
        need_ctx = l < DEPTH - 1
        h = _ffn_call(h, mod, norm_g4, p["wgu"], p["wd"], l, 0)
        qn, kn, vn, qm, km, vm, u = _inproj_call(h, mod, norm_g4, l, p, cos_t, sin_t)
        a = _na_call(qn, kn, vn, p["na_bias"], l, need_ctx)
        bm = _mla_call(qm, km, vm, need_ctx)
        u_t = u.transpose(1, 0, 2).reshape(S_ALL * BATCH, S5_WIDTH)
        yf = _s5_scan_call(u_t, p["bmat"], p["cmat"], p["lam"], l, 0)
        yb = _s5_scan_call(u_t, p["bmat"], p["cmat"], p["lam"], l, 1)
        s_t = _s5_out_call(yf, yb, u_t, p["s5_d"], p["w_glu"], p["b_glu"], l, need_ctx)
        s = s_t.reshape(-1, BATCH, S5_WIDTH).transpose(1, 0, 2)
        h = _ffn_call(h, mod, norm_g4, p["wgu"], p["wd"], l, 1, mixers=(a, bm, s), w_out=p["w_out"],
                      out_ctx=need_ctx)
    return h
```

```python
import functools
import math

import jax
import jax.numpy as jnp
from jax import lax
from jax.experimental import pallas as pl
from jax.experimental.pallas import tpu as pltpu

D_MODEL = 1024
BATCH = 8
SEQ = 2048
DEPTH = 2
CTX_LEN = 256
S_ALL = CTX_LEN + SEQ
GRID_W = 64
GRID_ROWS = SEQ // GRID_W
NA_HEAD_DIM = 64
NA_WIDTH = 384
NA_HEADS = 6
WIN_ROWS = 8
WIN_COLS = 16
MLA_V_DIM = 64
MLA_WIDTH = 384
MLA_HEADS = 6
MLA_NOPE = 64
MLA_ROPE = 32
MLA_QK = 96
Q_LORA = 384
KV_LORA = 256
S5_WIDTH = 256
S5_GROUP = 16
S5_GROUPS = 16
S5_STATE = 64
S5_LANES = S5_GROUPS * S5_STATE
D_FF = 2816
ROPE_THETA = 10000.0
EPS = 1e-6
N_MOD = 9
NEG_INF = -1e30

LANES = 128
TM = 256
N_TILES = S_ALL // TM
MXU_DIM = 256
FF_CHUNKS = ((0, 6 * MXU_DIM), (6 * MXU_DIM, D_FF))
HEAD_PAD = 128
S5_T = 128
S5_ROWS = S5_T * BATCH
S5_CHUNKS = S_ALL // S5_T
S5_CTX_CHUNKS = CTX_LEN // S5_T
VMEM_LIMIT = 56 * 1024 * 1024

ZC_QA, ZC_KA, ZC_VA, ZC_CQ, ZC_CKV = 0, 384, 768, 1152, 1536
Z_MAIN = 1792
Z_TAIL = S5_WIDTH + LANES

LOG2E = math.log2(math.e)
NA_Q_SCALE = NA_HEAD_DIM ** -0.5 * LOG2E
MLA_Q_SCALE = MLA_QK ** -0.5 * LOG2E

f32 = jnp.float32
bf16 = jnp.bfloat16


def _dot(a, b):
    return jnp.dot(a, b, preferred_element_type=f32)


def _sigmoid(x):
    return 1.0 / (1.0 + jnp.exp(-x))


def _rms(x, n):
    return lax.rsqrt(jnp.sum(x * x, axis=-1, keepdims=True) / n + EPS)


def _modulated(x, g, shift, scale):
    y = x * _rms(x, D_MODEL)
    return (y * g) * (1.0 + scale) + shift


def _mod_kernel(c_ref, w_ref, b_ref, o_ref):
    c = c_ref[...]
    a = (c * _sigmoid(c)).astype(bf16)
    o_ref[0] = _dot(a, w_ref[0].astype(bf16)) + b_ref[0]


def _mod_call(cvec, w_mod, b_mod):
    rows = cvec.shape[0]
    return pl.pallas_call(
        _mod_kernel,
        grid=(DEPTH, N_MOD),
        in_specs=[
            pl.BlockSpec((rows, D_MODEL), lambda l, j: (0, 0)),
            pl.BlockSpec((1, D_MODEL, D_MODEL), lambda l, j: (l, 0, j)),
            pl.BlockSpec((1, 1, D_MODEL), lambda l, j: (l, 0, j)),
        ],
        out_specs=pl.BlockSpec((1, rows, D_MODEL), lambda l, j: (l, 0, j)),
        out_shape=jax.ShapeDtypeStruct((DEPTH, rows, N_MOD * D_MODEL), f32),
        compiler_params=pltpu.CompilerParams(
            dimension_semantics=("arbitrary", "arbitrary"), vmem_limit_bytes=VMEM_LIMIT),
        name="adaln_mod",
    )(cvec, w_mod, b_mod.reshape(DEPTH, 1, N_MOD * D_MODEL))


def _param_spec(arr, *lead, single=False):
    tail = arr.shape[len(lead):]
    idx = tuple(lead) + (0,) * len(tail)
    mode = pl.Buffered(1) if single else None
    return pl.BlockSpec((1,) * len(lead) + tail, lambda *_: idx, pipeline_mode=mode)


FFN_SUBTILES = 2


def _ffn_kernel(*refs, i0, fuse_outproj, split_input, tiles_per_batch, first):
    n_sub = FFN_SUBTILES
    refs = list(refs)
    n_src = 2 if split_input else 1
    srcs = [refs[g * n_src:(g + 1) * n_src] for g in range(n_sub)]
    refs = refs[n_sub * n_src:]
    mods = refs[:n_sub]
    g_ref, wgu_ref, wd_ref = refs[n_sub:n_sub + 3]
    refs = refs[n_sub + 3:]
    if fuse_outproj:
        mixers = [refs[3 * g:3 * g + 3] for g in range(n_sub)]
        wo_ref = refs[3 * n_sub]
        refs = refs[3 * n_sub + 1:]
    o_ref, x_scr, xm_scr = refs

    def mod_set(g):
        j = (pl.program_id(0) * n_sub + g) % tiles_per_batch + first
        return j, jnp.minimum(j, 1)

    def prologue(g):
        j, sel = mod_set(g)
        rows = slice(g * TM, (g + 1) * TM)
        mod_ref = mods[g]
        if split_input:
            x = jnp.where(j == 0, srcs[g][0][0], srcs[g][1][0])
        else:
            x = srcs[g][0][0]
        if fuse_outproj:
            a_ref, b_ref, s_ref = mixers[g]
            mixed = jnp.concatenate([a_ref[0], b_ref[0], s_ref[0]], axis=-1)
            x = x + mod_ref[0, sel, 5:6, :] * _dot(mixed, wo_ref[0])
        x_scr[rows, :] = x
        shift = mod_ref[0, sel, i0:i0 + 1, :]
        scale = mod_ref[0, sel, i0 + 1:i0 + 2, :]
        xm_scr[rows, :] = _modulated(x, g_ref[0, 0], shift, scale).astype(bf16)

    def chunk(g, c, acc):
        rows = slice(g * TM, (g + 1) * TM)
        lo, hi = FF_CHUNKS[c]
        gt = _dot(xm_scr[rows, :], wgu_ref[0, 0, :, lo:hi])
        up = _dot(xm_scr[rows, :], wgu_ref[0, 0, :, D_FF + lo:D_FF + hi])
        a = ((gt * _sigmoid(gt)) * up).astype(bf16)
        part = _dot(a, wd_ref[0, 0, lo:hi, :])
        return part if acc is None else acc + part

    def epilogue(g, acc):
        _, sel = mod_set(g)
        gate = mods[g][0, sel, i0 + 2:i0 + 3, :]
        o_ref[g] = x_scr[g * TM:(g + 1) * TM, :] + (0.5 * gate) * acc

    assert len(FF_CHUNKS) == 2
    prologue(0)
    acc_prev = chunk(0, 0, None)
    for g in range(1, n_sub):
        prologue(g)
        acc_prev = chunk(g - 1, 1, acc_prev)
        acc_cur = chunk(g, 0, None)
        epilogue(g - 1, acc_prev)
        acc_prev = acc_cur
    acc_prev = chunk(n_sub - 1, 1, acc_prev)
    epilogue(n_sub - 1, acc_prev)


def _ffn_call(h, mod, norm_g, wgu, wd, l, k, *, mixers=None, w_out=None, out_ctx=True):
    split = isinstance(h, tuple)
    first = 0 if out_ctx else 1
    tpb = N_TILES - first
    n_sub = FFN_SUBTILES

    def tile_of(i, g):
        t = i * n_sub + g
        return t // tpb, t % tpb

    ins, in_specs = [], []
    for g in range(n_sub):
        if split:
            ins += list(h)
            in_specs += [
                pl.BlockSpec((1, TM, D_MODEL), lambda i, g=g: (tile_of(i, g)[0], 0, 0)),
                pl.BlockSpec((1, TM, D_MODEL),
                             lambda i, g=g: (tile_of(i, g)[0], jnp.maximum(tile_of(i, g)[1] + first - 1, 0), 0))]
        else:
            ins.append(h)
            in_specs.append(pl.BlockSpec(
                (1, TM, D_MODEL), lambda i, g=g: (tile_of(i, g)[0], tile_of(i, g)[1] + first, 0)))
    for g in range(n_sub):
        ins.append(mod)
        in_specs.append(pl.BlockSpec((1, 2, N_MOD, D_MODEL), lambda i, g=g: (tile_of(i, g)[0], 0, 0, 0)))
    ins += [norm_g, wgu, wd]
    in_specs += [_param_spec(norm_g, l, 2 * k), _param_spec(wgu, l, k, single=True),
                 _param_spec(wd, l, k, single=True)]
    if mixers is not None:
        for g in range(n_sub):
            for arr in mixers:
                ins.append(arr)
                in_specs.append(pl.BlockSpec((1, TM, arr.shape[-1]), lambda i, g=g: tile_of(i, g) + (0,)))
        ins.append(w_out)
        in_specs.append(_param_spec(w_out, l, single=True))
    rows = n_sub * TM
    out = pl.pallas_call(
        functools.partial(_ffn_kernel, i0=6 * k, fuse_outproj=mixers is not None, split_input=split,
                          tiles_per_batch=tpb, first=first),
        grid=(BATCH * tpb // n_sub,),
        in_specs=in_specs,
        out_specs=pl.BlockSpec((n_sub, TM, D_MODEL), lambda i: (i, 0, 0)),
        out_shape=jax.ShapeDtypeStruct((BATCH * tpb, TM, D_MODEL), f32),
        scratch_shapes=[pltpu.VMEM((rows, D_MODEL), f32), pltpu.VMEM((rows, D_MODEL), bf16)],
        compiler_params=pltpu.CompilerParams(
            dimension_semantics=("arbitrary",), vmem_limit_bytes=VMEM_LIMIT),
        name="swiglu_half_step",
    )(*ins)
    return out.reshape(BATCH, tpb * TM, D_MODEL)


INPROJ_SUBTILES = 3


def _pair_head_norm(x, g):
    lane = lax.broadcasted_iota(jnp.int32, (1, LANES), 1)
    lo = lane < NA_HEAD_DIM
    outs = []
    for p in range(NA_WIDTH // LANES):
        xb = x[:, p * LANES:(p + 1) * LANES]
        sq = xb * xb
        s_lo = jnp.sum(jnp.where(lo, sq, 0.0), axis=-1, keepdims=True)
        s_hi = jnp.sum(jnp.where(lo, 0.0, sq), axis=-1, keepdims=True)
        r = jnp.where(lo, lax.rsqrt(s_lo / NA_HEAD_DIM + EPS), lax.rsqrt(s_hi / NA_HEAD_DIM + EPS))
        outs.append((xb * r) * g[:, p * LANES:(p + 1) * LANES])
    return outs


def _inproj_kernel(h_ref, mod_ref, g_ref, wmain_ref, wtail_ref, naq_g_ref, nak_g_ref, cq_g_ref, ckv_g_ref,
                   wq_ref, wkv_ref, gq_ref, gqp_ref, gk_ref, gkp_ref, cos_ref, sin_ref,
                   qn_ref, kn_ref, vn_ref, qm_ref, km_ref, vm_ref, u_ref, z_scr, zt_scr):
    def project(g):
        rows = slice(g * TM, (g + 1) * TM)
        sel = jnp.minimum(pl.program_id(1) * INPROJ_SUBTILES + g, 1)
        xm = _modulated(h_ref[0, rows, :], g_ref[0, 0], mod_ref[0, sel, 3:4, :],
                        mod_ref[0, sel, 4:5, :]).astype(bf16)
        z_scr[rows, :] = _dot(xm, wmain_ref[0])
        zt_scr[rows, :] = _dot(xm, wtail_ref[0])

    def heads(g):
        rows = slice(g * TM, (g + 1) * TM)
        _inproj_heads(rows, z_scr, zt_scr, naq_g_ref, nak_g_ref, cq_g_ref, ckv_g_ref, wq_ref, wkv_ref,
                      gq_ref, gqp_ref, gk_ref, gkp_ref, cos_ref, sin_ref,
                      qn_ref, kn_ref, vn_ref, qm_ref, km_ref, vm_ref, u_ref)

    project(0)
    for g in range(INPROJ_SUBTILES):
        if g + 1 < INPROJ_SUBTILES:
            project(g + 1)
        heads(g)


def _inproj_heads(rows, z_scr, zt_scr, naq_g_ref, nak_g_ref, cq_g_ref, ckv_g_ref, wq_ref, wkv_ref,
                  gq_ref, gqp_ref, gk_ref, gkp_ref, cos_ref, sin_ref,
                  qn_ref, kn_ref, vn_ref, qm_ref, km_ref, vm_ref, u_ref):
    z = z_scr[rows, :]
    zt = zt_scr[rows, :]

    qn = _pair_head_norm(z[:, ZC_QA:ZC_QA + NA_WIDTH], naq_g_ref[0])
    kn = _pair_head_norm(z[:, ZC_KA:ZC_KA + NA_WIDTH], nak_g_ref[0])
    for p in range(NA_WIDTH // LANES):
        qn_ref[0, p * LANES:(p + 1) * LANES, rows] = (qn[p] * NA_Q_SCALE).T.astype(bf16)
        kn_ref[0, rows, p * LANES:(p + 1) * LANES] = kn[p].astype(bf16)
        vn_ref[0, p * LANES:(p + 1) * LANES, rows] = z[:, ZC_VA + p * LANES:ZC_VA + (p + 1) * LANES].T.astype(bf16)
    u_ref[0, rows, :] = zt[:, 0:S5_WIDTH]

    cos_t = cos_ref[rows, :]
    sin_t = sin_ref[rows, :]

    cq = z[:, ZC_CQ:ZC_CQ + Q_LORA]
    ncq = ((cq * _rms(cq, Q_LORA)) * cq_g_ref[0]).astype(bf16)
    qq = _dot(ncq, wq_ref[0])
    gq = gq_ref[0]
    gqp = gqp_ref[0]
    for hd in range(MLA_HEADS):
        pre = qq[:, hd * HEAD_PAD:(hd + 1) * HEAD_PAD]
        perm = qq[:, (MLA_HEADS + hd) * HEAD_PAD:(MLA_HEADS + hd + 1) * HEAD_PAD]
        r = _rms(pre, MLA_QK)
        qm_ref[0, hd * HEAD_PAD:(hd + 1) * HEAD_PAD, rows] = (
            ((pre * gq) * cos_t + (perm * gqp) * sin_t) * (r * MLA_Q_SCALE)).T.astype(bf16)

    ckv = z[:, ZC_CKV:ZC_CKV + KV_LORA]
    nkv = ((ckv * _rms(ckv, KV_LORA)) * ckv_g_ref[0]).astype(bf16)
    kv = _dot(nkv, wkv_ref[0])
    for p in range(MLA_WIDTH // LANES):
        lo_col = MLA_HEADS * HEAD_PAD + p * LANES
        vm_ref[0, p * LANES:(p + 1) * LANES, rows] = kv[:, lo_col:lo_col + LANES].T.astype(bf16)
    krb = zt[:, S5_WIDTH:S5_WIDTH + LANES]
    lane = lax.broadcasted_iota(jnp.int32, (1, LANES), 1)
    rope_lanes = (lane >= MLA_NOPE) & (lane < MLA_QK)
    kr_a = jnp.where(rope_lanes, pltpu.roll(krb, MLA_NOPE, axis=1), 0.0)
    kr_b = jnp.where(rope_lanes, pltpu.roll(krb, MLA_ROPE, axis=1), 0.0)
    gk = gk_ref[0]
    gkp = gkp_ref[0]
    rot_part = (kr_b * gkp) * sin_t
    for hd in range(MLA_HEADS):
        kfull = kv[:, hd * HEAD_PAD:(hd + 1) * HEAD_PAD] + kr_a
        r = _rms(kfull, MLA_QK)
        km_ref[0, rows, hd * HEAD_PAD:(hd + 1) * HEAD_PAD] = (
            ((kfull * gk) * cos_t + rot_part) * r).astype(bf16)


def _inproj_call(h, mod, norm_g, l, params, cos_t, sin_t):
    qk_w = MLA_HEADS * HEAD_PAD
    rows = INPROJ_SUBTILES * TM

    def tok(width):
        return pl.BlockSpec((1, rows, width), lambda b, j: (b, j, 0))

    def tok_t(width):
        return pl.BlockSpec((1, width, rows), lambda b, j: (b, 0, j))

    tab_spec = pl.BlockSpec((rows, LANES), lambda b, j: (j, 0))
    mod_spec = pl.BlockSpec((1, 2, N_MOD, D_MODEL), lambda b, j: (b, 0, 0, 0))
    names = ["win_main", "win_tail", "naq_g", "nak_g", "cq_g", "ckv_g", "wq", "wkv", "gq", "gqp", "gk", "gkp"]
    return pl.pallas_call(
        _inproj_kernel,
        grid=(BATCH, N_TILES // INPROJ_SUBTILES),
        in_specs=([tok(D_MODEL), mod_spec, _param_spec(norm_g, l, 1)]
                  + [_param_spec(params[n], l) for n in names] + [tab_spec, tab_spec]),
        out_specs=[tok_t(NA_WIDTH), tok(NA_WIDTH), tok_t(NA_WIDTH),
                   tok_t(qk_w), tok(qk_w), tok_t(MLA_WIDTH), tok(S5_WIDTH)],
        out_shape=[
            jax.ShapeDtypeStruct((BATCH, NA_WIDTH, S_ALL), bf16),
            jax.ShapeDtypeStruct((BATCH, S_ALL, NA_WIDTH), bf16),
            jax.ShapeDtypeStruct((BATCH, NA_WIDTH, S_ALL), bf16),
            jax.ShapeDtypeStruct((BATCH, qk_w, S_ALL), bf16),
            jax.ShapeDtypeStruct((BATCH, S_ALL, qk_w), bf16),
            jax.ShapeDtypeStruct((BATCH, MLA_WIDTH, S_ALL), bf16),
            jax.ShapeDtypeStruct((BATCH, S_ALL, S5_WIDTH), f32),
        ],
        scratch_shapes=[pltpu.VMEM((rows, Z_MAIN), f32), pltpu.VMEM((rows, Z_TAIL), f32)],
        compiler_params=pltpu.CompilerParams(
            dimension_semantics=("arbitrary", "arbitrary"), vmem_limit_bytes=VMEM_LIMIT),
        name="in_proj_heads",
    )(h, mod, norm_g, *[params[n] for n in names], cos_t, sin_t)


def _pair_block_diag(qt):
    d = qt.shape[0] // 2
    z = jnp.zeros((d, qt.shape[1]), qt.dtype)
    return jnp.concatenate([jnp.concatenate([qt[:d], z], axis=0),
                            jnp.concatenate([z, qt[d:]], axis=0)], axis=1)


def _pair_scores(dst_ref, key_blocks, q_bd, bias_blocks):
    r = 0
    for kk, bias in zip(key_blocks, bias_blocks):
        s = _dot(kk, q_bd)
        dst_ref[r:r + kk.shape[0], :] = s if bias is None else s + bias
        r += kk.shape[0]


def _pair_softmax_pv(src_ref, vt_blocks):
    n_keys = sum(vt.shape[1] for vt in vt_blocks)
    m = jnp.max(src_ref[0:n_keys, :], axis=0, keepdims=True)
    den = acc = None
    r = 0
    for vt in vt_blocks:
        p = jnp.exp2(src_ref[r:r + vt.shape[1], :] - m)
        li = jnp.sum(p, axis=0, keepdims=True)
        oi = _dot(vt, p.astype(bf16))
        den = li if den is None else den + li
        acc = oi if acc is None else acc + oi
        r += vt.shape[1]
    o = acc / den
    dv = o.shape[0] // 2
    return jnp.concatenate([o[:dv, :TM], o[dv:, TM:]], axis=0).T.astype(bf16)


def _pipelined_tiles(scores, finish, s_a, s_b, first_tile, last_tile, lead_in=None):
    scores(first_tile, s_a)
    if lead_in is not None:
        lead_in(s_b)

    def body(i, carry):
        t0 = first_tile + 2 * i
        scores(t0 + 1, s_b)
        finish(t0, s_a)
        scores(t0 + 2, s_a)
        finish(t0 + 1, s_b)
        return carry

    lax.fori_loop(0, (last_tile - first_tile + 1) // 2 - 1, body, 0)
    scores(last_tile, s_b)
    finish(last_tile - 1, s_a)
    finish(last_tile, s_b)


NA_TILE_ROWS = TM // GRID_W
NA_UNION_ROWS = 12
NA_UNION_KEYS = NA_UNION_ROWS * GRID_W
NA_PATTERNS = 3
NA_BIAS_SHIFTS = 2 * WIN_ROWS


def _na_window_lo(pat, rr):
    return (-rr, -(WIN_ROWS // 2), -(WIN_ROWS // 2) - rr)[pat]


def _na_fill_bias(blk_ref, bias_scr):
    lane = lax.broadcasted_iota(jnp.int32, (1, LANES), 1)
    left = lane < GRID_W
    neg = jnp.full((GRID_W, LANES), NEG_INF, f32)
    for pat in range(NA_PATTERNS):
        for ii in range(NA_UNION_ROWS):
            for t in range(NA_TILE_ROWS // 2):
                d = ii - NA_TILE_ROWS * pat - 2 * t
                ok_l = 0 <= d - _na_window_lo(pat, 2 * t) < WIN_ROWS
                ok_r = 0 <= (d - 1) - _na_window_lo(pat, 2 * t + 1) < WIN_ROWS
                for hh in range(2):
                    if ok_l or ok_r:
                        blk = blk_ref[0, hh, d + WIN_ROWS - 1]
                        if not ok_l:
                            blk = jnp.where(left, NEG_INF, blk)
                        if not ok_r:
                            blk = jnp.where(left, blk, NEG_INF)
                    else:
                        blk = neg
                    c0 = hh * TM + t * LANES
                    bias_scr[pat, ii * GRID_W:(ii + 1) * GRID_W, c0:c0 + LANES] = blk


def _query_tile(qt_ref, t):
    return _pair_block_diag(qt_ref[0, :, pl.ds(pl.multiple_of(t * TM, TM), TM)])


def _out_rows(t, with_ctx):
    return pl.ds(pl.multiple_of((t - (0 if with_ctx else 1)) * TM, TM), TM)


def _na_kernel(qt_ref, k_ref, vt_ref, blk_ref, o_ref, bias_scr, s_a, s_b, *, with_ctx):
    @pl.when(pl.program_id(1) == 0)
    def _():
        _na_fill_bias(blk_ref, bias_scr)

    kc = k_ref[0, 0:CTX_LEN, :]
    vtc = vt_ref[0, :, 0:CTX_LEN]

    def window(t):
        r0 = (t - 1) * NA_TILE_ROWS
        base = jnp.clip(r0 - WIN_ROWS // 2, 0, GRID_ROWS - NA_UNION_ROWS)
        pat = (r0 - base) // NA_TILE_ROWS
        start = pl.multiple_of(CTX_LEN + base * GRID_W, NA_TILE_ROWS * GRID_W)
        return pat, pl.ds(start, NA_UNION_KEYS)

    def scores(t, dst):
        pat, keys = window(t)
        _pair_scores(dst, [k_ref[0, keys, :], kc], _query_tile(qt_ref, t), [bias_scr[pat], None])

    def finish(t, src):
        _, keys = window(t)
        o_ref[0, _out_rows(t, with_ctx), :] = _pair_softmax_pv(src, [vt_ref[0, :, keys], vtc])

    def ctx_finish(buf):
        o_ref[0, 0:TM, :] = _pair_softmax_pv(buf, [vtc])

    if with_ctx:
        _pair_scores(s_b, [kc], _query_tile(qt_ref, 0), [None])
    _pipelined_tiles(scores, finish, s_a, s_b, 1, N_TILES - 1, ctx_finish if with_ctx else None)


def _na_call(qnt, kn, vnt, bias_blocks, l, with_ctx):
    n_pairs = NA_WIDTH // LANES
    s_out = S_ALL if with_ctx else SEQ
    logits = pltpu.VMEM((NA_UNION_KEYS + CTX_LEN, 2 * TM), f32)
    return pl.pallas_call(
        functools.partial(_na_kernel, with_ctx=with_ctx),
        grid=(n_pairs, BATCH),
        in_specs=[
            pl.BlockSpec((1, LANES, S_ALL), lambda p, b: (b, p, 0)),
            pl.BlockSpec((1, S_ALL, LANES), lambda p, b: (b, 0, p)),
            pl.BlockSpec((1, LANES, S_ALL), lambda p, b: (b, p, 0)),
            pl.BlockSpec((1, 2, NA_BIAS_SHIFTS, GRID_W, LANES), lambda p, b: (l, p, 0, 0, 0)),
        ],
        out_specs=pl.BlockSpec((1, s_out, LANES), lambda p, b: (b, 0, p)),
        out_shape=jax.ShapeDtypeStruct((BATCH, s_out, NA_WIDTH), bf16),
        scratch_shapes=[pltpu.VMEM((NA_PATTERNS, NA_UNION_KEYS, 2 * TM), f32), logits, logits],
        compiler_params=pltpu.CompilerParams(
            dimension_semantics=("arbitrary", "arbitrary"), vmem_limit_bytes=VMEM_LIMIT),
        name="na_attention",
    )(qnt, kn, vnt, bias_blocks)


MLA_KEY_BLOCK = 768


def _mla_kernel(qt_ref, k_ref, vt_ref, o_ref, s_a, s_b, *, with_ctx):
    def scores(t, dst):
        _pair_scores(dst, [k_ref[0]], _query_tile(qt_ref, t), [None])

    def finish(t, src):
        vts = [vt_ref[0, :, s0:s0 + MLA_KEY_BLOCK] for s0 in range(0, S_ALL, MLA_KEY_BLOCK)]
        o_ref[0, _out_rows(t, with_ctx), :] = _pair_softmax_pv(src, vts)

    def ctx_finish(buf):
        o_ref[0, 0:TM, :] = _pair_softmax_pv(buf, [vt_ref[0, :, 0:CTX_LEN]])

    if with_ctx:
        _pair_scores(s_b, [k_ref[0, 0:CTX_LEN, :]], _query_tile(qt_ref, 0), [None])
    _pipelined_tiles(scores, finish, s_a, s_b, 1, N_TILES - 1, ctx_finish if with_ctx else None)


def _mla_call(qmt, km, vmt, with_ctx):
    n_pairs = MLA_HEADS // 2
    s_out = S_ALL if with_ctx else SEQ
    logits = pltpu.VMEM((S_ALL, 2 * TM), f32)
    return pl.pallas_call(
        functools.partial(_mla_kernel, with_ctx=with_ctx),
        grid=(BATCH, n_pairs),
        in_specs=[
            pl.BlockSpec((1, 2 * HEAD_PAD, S_ALL), lambda b, p: (b, p, 0)),
            pl.BlockSpec((1, S_ALL, 2 * HEAD_PAD), lambda b, p: (b, 0, p)),
            pl.BlockSpec((1, LANES, S_ALL), lambda b, p: (b, p, 0)),
        ],
        out_specs=pl.BlockSpec((1, s_out, LANES), lambda b, p: (b, 0, p)),
        out_shape=jax.ShapeDtypeStruct((BATCH, s_out, MLA_WIDTH), bf16),
        scratch_shapes=[logits, logits],
        compiler_params=pltpu.CompilerParams(
            dimension_semantics=("arbitrary", "arbitrary"), vmem_limit_bytes=VMEM_LIMIT),
        name="mla_attention",
    )(qmt, km, vmt)


S5_BLOCK_T = 16
S5_BLOCK_ROWS = S5_BLOCK_T * BATCH


def _s5_scan_kernel(*refs, reverse, finish):
    u_ref, bmat_ref, cmat_ref, lam_ref = refs[:4]
    if finish:
        y_other_ref, d_ref, wglu_ref, bglu_ref = refs[4:8]
    y_ref, bu_ref, h_ref, st_ref = refs[-4:]

    @pl.when(pl.program_id(0) == 0)
    def _():
        st_ref[...] = jnp.zeros_like(st_ref)

    def rows(k):
        return slice(k * S5_BLOCK_ROWS, (k + 1) * S5_BLOCK_ROWS)

    def project(k):
        bu_ref[rows(k), :] = _dot(u_ref[rows(k), :].astype(bf16), bmat_ref[0, 0])

    def readout(k):
        y = _dot(h_ref[rows(k), :].astype(bf16), cmat_ref[0, 0])
        if not finish:
            y_ref[rows(k), :] = y
            return
        y = (y_other_ref[rows(k), :] + y) + d_ref[0] * u_ref[rows(k), :]
        c0 = math.sqrt(2.0 / math.pi)
        gl = 0.5 * y * (1.0 + jnp.tanh(c0 * (y + 0.044715 * (y * y * y))))
        o = _dot(gl.astype(bf16), wglu_ref[0]) + bglu_ref[0]
        y_ref[rows(k), :] = (o[:, :S5_WIDTH] * _sigmoid(o[:, S5_WIDTH:])).astype(bf16)

    n_blocks = S5_T // S5_BLOCK_T
    order = list(range(n_blocks))[::-1] if reverse else list(range(n_blocks))
    steps = list(range(S5_BLOCK_T))[::-1] if reverse else list(range(S5_BLOCK_T))
    hr = st_ref[:, 0:S5_LANES]
    hi = st_ref[:, S5_LANES:]
    project(order[0])
    for n, k in enumerate(order):
        if n + 1 < n_blocks:
            project(order[n + 1])
        for t in steps:
            r0 = k * S5_BLOCK_ROWS + t * BATCH
            lr = lam_ref[0, 0, :, 0:S5_LANES]
            li = lam_ref[0, 0, :, S5_LANES:]
            nr = (lr * hr - li * hi) + bu_ref[r0:r0 + BATCH, 0:S5_LANES]
            ni = (lr * hi + li * hr) + bu_ref[r0:r0 + BATCH, S5_LANES:]
            h_ref[r0:r0 + BATCH, 0:S5_LANES] = nr
            h_ref[r0:r0 + BATCH, S5_LANES:] = ni
            hr, hi = nr, ni
        if n >= 1:
            readout(order[n - 1])
    readout(order[-1])
    st_ref[:, 0:S5_LANES] = hr
    st_ref[:, S5_LANES:] = hi


def _s5_scan_call(u_t, p, l, d, y_other=None):
    reverse = d == 1
    finish = y_other is not None
    if reverse:
        def chunk(i):
            return jnp.where(i < S5_CTX_CHUNKS, S5_CTX_CHUNKS - 1 - i, S5_CHUNKS + S5_CTX_CHUNKS - 1 - i)
    else:
        def chunk(i):
            return i
    row_spec = pl.BlockSpec((S5_ROWS, S5_WIDTH), lambda i: (chunk(i), 0))
    ins = [u_t, p["bmat"], p["cmat"], p["lam"]]
    in_specs = [row_spec, _param_spec(p["bmat"], l, d), _param_spec(p["cmat"], l, d), _param_spec(p["lam"], l, d)]
    if finish:
        ins += [y_other, p["s5_d"], p["w_glu"], p["b_glu"]]
        in_specs += [row_spec, _param_spec(p["s5_d"], l), _param_spec(p["w_glu"], l), _param_spec(p["b_glu"], l)]
    return pl.pallas_call(
        functools.partial(_s5_scan_kernel, reverse=reverse, finish=finish),
        grid=(S5_CHUNKS,),
        in_specs=in_specs,
        out_specs=row_spec,
        out_shape=jax.ShapeDtypeStruct((S_ALL * BATCH, S5_WIDTH), bf16 if finish else f32),
        scratch_shapes=[pltpu.VMEM((S5_ROWS, 2 * S5_LANES), f32), pltpu.VMEM((S5_ROWS, 2 * S5_LANES), f32),
                        pltpu.VMEM((BATCH, 2 * S5_LANES), f32)],
        compiler_params=pltpu.CompilerParams(
            dimension_semantics=("arbitrary",), vmem_limit_bytes=VMEM_LIMIT),
        name="s5_scan_bwd" if reverse else "s5_scan_fwd",
    )(*ins)


def _rope_perm():
    half = MLA_ROPE // 2
    quarter = half // 2
    idx, sign = [], []
    for j in range(MLA_ROPE):
        if (j % half) < quarter:
            idx.append(j + quarter)
            sign.append(-1.0)
        else:
            idx.append(j - quarter)
            sign.append(1.0)
    return jnp.array(idx, jnp.int32), jnp.array(sign, f32)


def _rope_tables():
    quarter = MLA_ROPE // 4
    t = jnp.arange(SEQ)
    row = (t // GRID_W).astype(f32)
    col = (t % GRID_W).astype(f32)
    inv = ROPE_THETA ** (-jnp.arange(quarter, dtype=f32) / quarter)
    ang = jnp.concatenate([row[:, None] * inv] * 2 + [col[:, None] * inv] * 2, axis=-1)
    ang = jnp.concatenate([jnp.zeros((CTX_LEN, MLA_ROPE), f32), ang], axis=0)
    ones = jnp.ones((S_ALL, MLA_NOPE), f32)
    pad = jnp.zeros((S_ALL, HEAD_PAD - MLA_QK), f32)
    cos_t = jnp.concatenate([ones, jnp.cos(ang), pad], axis=-1)
    sin_t = jnp.concatenate([0.0 * ones, jnp.sin(ang), pad], axis=-1)
    return cos_t, sin_t


def _na_bias_blocks(rpb):
    cq = jnp.arange(GRID_W)[None, :]
    kc = jnp.arange(GRID_W)[:, None]
    d_col = jnp.clip(kc - cq, -(WIN_COLS - 1), WIN_COLS - 1) + WIN_COLS - 1
    col_start = jnp.clip(cq - WIN_COLS // 2, 0, GRID_W - WIN_COLS)
    in_win = (kc >= col_start) & (kc < col_start + WIN_COLS)
    onehot = (d_col[:, :, None] == jnp.arange(2 * WIN_COLS - 1)[None, None, :]).astype(f32)
    blocks = jnp.einsum('lhdc,kqc->lhdkq', rpb.astype(f32), onehot, precision=lax.Precision.HIGHEST)
    blocks = jnp.where(in_win, blocks * LOG2E, NEG_INF)
    neg = jnp.full(blocks.shape[:2] + (1, GRID_W, GRID_W), NEG_INF, f32)
    padded = jnp.concatenate([neg, blocks, neg], axis=2)
    return jnp.concatenate([padded[:, :, 1:], padded[:, :, :-1]], axis=-1)


def _pad_heads(w, width):
    lead = w.shape[:-1]
    w = w.reshape(lead + (MLA_HEADS, width))
    w = jnp.pad(w, [(0, 0)] * len(lead) + [(0, 0), (0, HEAD_PAD - width)])
    return w.reshape(lead + (MLA_HEADS * HEAD_PAD,))


def _block_diag_groups(w):
    eye = jnp.eye(S5_GROUPS, dtype=f32)
    lead = w.shape[:-3]
    a, b = w.shape[-2:]
    full = w[..., :, :, None, :] * eye[:, None, :, None]
    return full.reshape(lead + (S5_GROUPS * a, S5_GROUPS * b))


def _prepare_params(ffn_w_gu, ffn_w_down, w_in, w_out, na_qk_g, na_rpb, mla_cq_g, mla_ckv_g, mla_w_uq,
                    mla_w_ukv, mla_qk_g, s5_lam_re, s5_lam_im, s5_log_dt, s5_b_re, s5_b_im, s5_c_re,
                    s5_c_im, s5_d, s5_w_glu, s5_b_glu):
    p = {}
    perm_idx, perm_sign = _rope_perm()
    p["wgu"] = ffn_w_gu.astype(bf16)
    p["wd"] = ffn_w_down.astype(bf16)
    p["w_out"] = w_out.astype(bf16)
    o_kr = Z_MAIN
    o_u = o_kr + MLA_ROPE
    w_kr = w_in[:, :, o_kr:o_kr + MLA_ROPE]
    p["win_main"] = w_in[:, :, :Z_MAIN].astype(bf16)
    p["win_tail"] = jnp.concatenate([
        w_in[:, :, o_u:o_u + S5_WIDTH], w_kr, w_kr[:, :, perm_idx] * perm_sign,
        jnp.zeros((DEPTH, D_MODEL, LANES - 2 * MLA_ROPE), f32)], axis=-1).astype(bf16)
    p["naq_g"] = jnp.tile(na_qk_g[:, 0], (1, NA_HEADS))[:, None, :]
    p["nak_g"] = jnp.tile(na_qk_g[:, 1], (1, NA_HEADS))[:, None, :]
    p["cq_g"] = mla_cq_g[:, None, :]
    p["ckv_g"] = mla_ckv_g[:, None, :]
    wuq = mla_w_uq.reshape(DEPTH, Q_LORA, MLA_HEADS, MLA_QK)
    partner = jnp.concatenate([jnp.zeros((DEPTH, Q_LORA, MLA_HEADS, MLA_NOPE), f32),
                               wuq[..., MLA_NOPE:][..., perm_idx] * perm_sign], axis=-1)
    p["wq"] = jnp.concatenate([_pad_heads(mla_w_uq, MLA_QK),
                               _pad_heads(partner.reshape(DEPTH, Q_LORA, -1), MLA_QK)], axis=-1).astype(bf16)
    wukv = mla_w_ukv.reshape(DEPTH, KV_LORA, MLA_HEADS, MLA_NOPE + MLA_V_DIM)
    p["wkv"] = jnp.concatenate([_pad_heads(wukv[..., :MLA_NOPE].reshape(DEPTH, KV_LORA, -1), MLA_NOPE),
                                wukv[..., MLA_NOPE:].reshape(DEPTH, KV_LORA, -1)], axis=-1).astype(bf16)

    def pad_gain(g):
        z = jnp.zeros((DEPTH, HEAD_PAD - MLA_QK), f32)
        full = jnp.concatenate([g, z], axis=-1)[:, None, :]
        part = jnp.concatenate([jnp.zeros((DEPTH, MLA_NOPE), f32), g[:, MLA_NOPE:][:, perm_idx], z],
                               axis=-1)[:, None, :]
        return full, part

    p["gq"], p["gqp"] = pad_gain(mla_qk_g[:, 0])
    p["gk"], p["gkp"] = pad_gain(mla_qk_g[:, 1])
    p["na_bias"] = _na_bias_blocks(na_rpb)
    lre = s5_lam_re.astype(f32)
    lim = s5_lam_im.astype(f32)
    dt = jnp.exp(s5_log_dt.astype(f32))[..., None]
    mag = jnp.exp(lre * dt)
    bar_re = mag * jnp.cos(lim * dt)
    bar_im = mag * jnp.sin(lim * dt)
    den = lre * lre + lim * lim
    q_re = ((bar_re - 1.0) * lre + bar_im * lim) / den
    q_im = (bar_im * lre - (bar_re - 1.0) * lim) / den
    bre = s5_b_re.astype(f32)
    bim = s5_b_im.astype(f32)
    bbar_re = q_re[..., None] * bre - q_im[..., None] * bim
    bbar_im = q_re[..., None] * bim + q_im[..., None] * bre
    p["bmat"] = jnp.concatenate([_block_diag_groups(jnp.swapaxes(bbar_re, -1, -2)),
                                 _block_diag_groups(jnp.swapaxes(bbar_im, -1, -2))], axis=-1).astype(bf16)
    p["cmat"] = jnp.concatenate([_block_diag_groups(jnp.swapaxes(s5_c_re.astype(f32), -1, -2)),
                                 -_block_diag_groups(jnp.swapaxes(s5_c_im.astype(f32), -1, -2))],
                                axis=-2).astype(bf16)
    lam_row = jnp.concatenate([bar_re.reshape(DEPTH, 2, S5_LANES), bar_im.reshape(DEPTH, 2, S5_LANES)], axis=-1)
    p["lam"] = jnp.broadcast_to(lam_row[:, :, None, :], (DEPTH, 2, BATCH, 2 * S5_LANES))
    p["s5_d"] = s5_d[:, None, :]
    p["w_glu"] = s5_w_glu.astype(bf16)
    p["b_glu"] = s5_b_glu[:, None, :]
    return p


def kernel(x, c, ctx, c_ctx, w_mod, b_mod, norm_g, ffn_w_gu, ffn_w_down, w_in, w_out, na_qk_g, na_rpb, mla_cq_g, mla_ckv_g, mla_w_uq, mla_w_ukv, mla_qk_g, s5_lam_re, s5_lam_im, s5_log_dt, s5_b_re, s5_b_im, s5_c_re, s5_c_im, s5_d, s5_w_glu, s5_b_glu):
    assert x.shape == (BATCH, SEQ, D_MODEL) and ctx.shape == (BATCH, CTX_LEN, D_MODEL)
    mod_rows = 16
    cvec = jnp.concatenate([c, c_ctx[None, :], jnp.zeros((mod_rows - BATCH - 1, D_MODEL), f32)], axis=0)
    mod_all = _mod_call(cvec, w_mod, b_mod).reshape(DEPTH, mod_rows, N_MOD, D_MODEL)
    cos_t, sin_t = _rope_tables()
    p = _prepare_params(ffn_w_gu, ffn_w_down, w_in, w_out, na_qk_g, na_rpb, mla_cq_g, mla_ckv_g,
                        mla_w_uq, mla_w_ukv, mla_qk_g, s5_lam_re, s5_lam_im, s5_log_dt, s5_b_re,
                        s5_b_im, s5_c_re, s5_c_im, s5_d, s5_w_glu, s5_b_glu)
    norm_g4 = norm_g[:, :, None, :]
    h = (ctx, x)
    for l in range(DEPTH):
        mod_c = jnp.broadcast_to(mod_all[l, BATCH][None], (BATCH, N_MOD, D_MODEL))
        mod = jnp.stack([mod_c, mod_all[l, :BATCH]], axis=1)
        need_ctx = l < DEPTH - 1
        h = _ffn_call(h, mod, norm_g4, p["wgu"], p["wd"], l, 0)
        qn, kn, vn, qm, km, vm, u = _inproj_call(h, mod, norm_g4, l, p, cos_t, sin_t)
        a = _na_call(qn, kn, vn, p["na_bias"], l, need_ctx)
        bm = _mla_call(qm, km, vm, need_ctx)
        u_t = u.transpose(1, 0, 2).reshape(S_ALL * BATCH, S5_WIDTH)
        yf = _s5_scan_call(u_t, p, l, 0)
        s_t = _s5_scan_call(u_t, p, l, 1, y_other=yf)
        if not need_ctx:
            s_t = s_t[CTX_LEN * BATCH:]
        s = s_t.reshape(-1, BATCH, S5_WIDTH).transpose(1, 0, 2)
        h = _ffn_call(h, mod, norm_g4, p["wgu"], p["wd"], l, 1, mixers=(a, bm, s), w_out=p["w_out"],
                      out_ctx=need_ctx)
    return h
```

```python
import functools
import math

import jax
import jax.numpy as jnp
from jax import lax
from jax.experimental import pallas as pl
from jax.experimental.pallas import tpu as pltpu

D_MODEL = 1024
BATCH = 8
SEQ = 2048
DEPTH = 2
CTX_LEN = 256
S_ALL = CTX_LEN + SEQ
GRID_W = 64
GRID_ROWS = SEQ // GRID_W
NA_HEAD_DIM = 64
NA_WIDTH = 384
NA_HEADS = 6
WIN_ROWS = 8
WIN_COLS = 16
MLA_V_DIM = 64
MLA_WIDTH = 384
MLA_HEADS = 6
MLA_NOPE = 64
MLA_ROPE = 32
MLA_QK = 96
Q_LORA = 384
KV_LORA = 256
S5_WIDTH = 256
S5_GROUP = 16
S5_GROUPS = 16
S5_STATE = 64
S5_LANES = S5_GROUPS * S5_STATE
D_FF = 2816
ROPE_THETA = 10000.0
EPS = 1e-6
N_MOD = 9
NEG_INF = -1e30

LANES = 128
TM = 256
N_TILES = S_ALL // TM
MXU_DIM = 256
FF_CHUNKS = ((0, 6 * MXU_DIM), (6 * MXU_DIM, D_FF))
HEAD_PAD = 128
S5_T = 128
S5_ROWS = S5_T * BATCH
S5_CHUNKS = S_ALL // S5_T
S5_CTX_CHUNKS = CTX_LEN // S5_T
VMEM_LIMIT = 56 * 1024 * 1024

ZC_QA, ZC_KA, ZC_VA, ZC_CQ, ZC_CKV = 0, 384, 768, 1152, 1536
Z_MAIN = 1792
Z_TAIL = S5_WIDTH + LANES

LOG2E = math.log2(math.e)
NA_Q_SCALE = NA_HEAD_DIM ** -0.5 * LOG2E
MLA_Q_SCALE = MLA_QK ** -0.5 * LOG2E

f32 = jnp.float32
bf16 = jnp.bfloat16


def _dot(a, b):
    return jnp.dot(a, b, preferred_element_type=f32)


def _sigmoid(x):
    return 1.0 / (1.0 + jnp.exp(-x))


def _rms(x, n):
    return lax.rsqrt(jnp.sum(x * x, axis=-1, keepdims=True) / n + EPS)


def _modulated(x, g, shift, scale):
    y = x * _rms(x, D_MODEL)
    return (y * g) * (1.0 + scale) + shift


def _mod_kernel(c_ref, w_ref, b_ref, o_ref):
    c = c_ref[...]
    a = (c * _sigmoid(c)).astype(bf16)
    o_ref[0] = _dot(a, w_ref[0].astype(bf16)) + b_ref[0]


def _mod_call(cvec, w_mod, b_mod):
    rows = cvec.shape[0]
    return pl.pallas_call(
        _mod_kernel,
        grid=(DEPTH, N_MOD),
        in_specs=[
            pl.BlockSpec((rows, D_MODEL), lambda l, j: (0, 0)),
            pl.BlockSpec((1, D_MODEL, D_MODEL), lambda l, j: (l, 0, j)),
            pl.BlockSpec((1, 1, D_MODEL), lambda l, j: (l, 0, j)),
        ],
        out_specs=pl.BlockSpec((1, rows, D_MODEL), lambda l, j: (l, 0, j)),
        out_shape=jax.ShapeDtypeStruct((DEPTH, rows, N_MOD * D_MODEL), f32),
        compiler_params=pltpu.CompilerParams(
            dimension_semantics=("arbitrary", "arbitrary"), vmem_limit_bytes=VMEM_LIMIT),
        name="adaln_mod",
    )(cvec, w_mod, b_mod.reshape(DEPTH, 1, N_MOD * D_MODEL))


def _param_spec(arr, *lead, single=False):
    tail = arr.shape[len(lead):]
    idx = tuple(lead) + (0,) * len(tail)
    mode = pl.Buffered(1) if single else None
    return pl.BlockSpec((1,) * len(lead) + tail, lambda *_: idx, pipeline_mode=mode)


FFN_SUBTILES = 4


def _ffn_kernel(*refs, i0, fuse_outproj, split_input, tiles_per_batch, first):
    n_sub = FFN_SUBTILES
    refs = list(refs)
    n_src = 2 if split_input else 1
    srcs = [refs[g * n_src:(g + 1) * n_src] for g in range(n_sub)]
    refs = refs[n_sub * n_src:]
    mods = refs[:n_sub]
    g_ref, wgu_ref, wd_ref = refs[n_sub:n_sub + 3]
    refs = refs[n_sub + 3:]
    if fuse_outproj:
        mixers = [refs[3 * g:3 * g + 3] for g in range(n_sub)]
        wo_ref = refs[3 * n_sub]
        refs = refs[3 * n_sub + 1:]
    o_ref, x_scr, xm_scr = refs

    def mod_set(g):
        j = (pl.program_id(0) * n_sub + g) % tiles_per_batch + first
        return j, jnp.minimum(j, 1)

    def prologue(g):
        j, sel = mod_set(g)
        rows = slice(g * TM, (g + 1) * TM)
        mod_ref = mods[g]
        if split_input:
            x = jnp.where(j == 0, srcs[g][0][0], srcs[g][1][0])
        else:
            x = srcs[g][0][0]
        if fuse_outproj:
            a_ref, b_ref, s_ref = mixers[g]
            mixed = jnp.concatenate([a_ref[0], b_ref[0], s_ref[0]], axis=-1)
            x = x + mod_ref[0, sel, 5:6, :] * _dot(mixed, wo_ref[0])
        x_scr[rows, :] = x
        shift = mod_ref[0, sel, i0:i0 + 1, :]
        scale = mod_ref[0, sel, i0 + 1:i0 + 2, :]
        xm_scr[rows, :] = _modulated(x, g_ref[0, 0], shift, scale).astype(bf16)

    def chunk(g, c, acc):
        rows = slice(g * TM, (g + 1) * TM)
        lo, hi = FF_CHUNKS[c]
        gt = _dot(xm_scr[rows, :], wgu_ref[0, 0, :, lo:hi])
        up = _dot(xm_scr[rows, :], wgu_ref[0, 0, :, D_FF + lo:D_FF + hi])
        a = ((gt * _sigmoid(gt)) * up).astype(bf16)
        part = _dot(a, wd_ref[0, 0, lo:hi, :])
        return part if acc is None else acc + part

    def epilogue(g, acc):
        _, sel = mod_set(g)
        gate = mods[g][0, sel, i0 + 2:i0 + 3, :]
        o_ref[g] = x_scr[g * TM:(g + 1) * TM, :] + (0.5 * gate) * acc

    assert len(FF_CHUNKS) == 2
    prologue(0)
    acc_prev = chunk(0, 0, None)
    for g in range(1, n_sub):
        prologue(g)
        acc_prev = chunk(g - 1, 1, acc_prev)
        acc_cur = chunk(g, 0, None)
        epilogue(g - 1, acc_prev)
        acc_prev = acc_cur
    acc_prev = chunk(n_sub - 1, 1, acc_prev)
    epilogue(n_sub - 1, acc_prev)


def _ffn_call(h, mod, norm_g, wgu, wd, l, k, *, mixers=None, w_out=None, out_ctx=True):
    split = isinstance(h, tuple)
    first = 0 if out_ctx else 1
    tpb = N_TILES - first
    n_sub = FFN_SUBTILES

    def tile_of(i, g):
        t = i * n_sub + g
        return t // tpb, t % tpb

    ins, in_specs = [], []
    for g in range(n_sub):
        if split:
            ins += list(h)
            in_specs += [
                pl.BlockSpec((1, TM, D_MODEL), lambda i, g=g: (tile_of(i, g)[0], 0, 0)),
                pl.BlockSpec((1, TM, D_MODEL),
                             lambda i, g=g: (tile_of(i, g)[0], jnp.maximum(tile_of(i, g)[1] + first - 1, 0), 0))]
        else:
            ins.append(h)
            in_specs.append(pl.BlockSpec(
                (1, TM, D_MODEL), lambda i, g=g: (tile_of(i, g)[0], tile_of(i, g)[1] + first, 0)))
    for g in range(n_sub):
        ins.append(mod)
        in_specs.append(pl.BlockSpec((1, 2, N_MOD, D_MODEL), lambda i, g=g: (tile_of(i, g)[0], 0, 0, 0)))
    ins += [norm_g, wgu, wd]
    in_specs += [_param_spec(norm_g, l, 2 * k), _param_spec(wgu, l, k, single=True),
                 _param_spec(wd, l, k, single=True)]
    if mixers is not None:
        for g in range(n_sub):
            for arr in mixers:
                ins.append(arr)
                in_specs.append(pl.BlockSpec((1, TM, arr.shape[-1]), lambda i, g=g: tile_of(i, g) + (0,)))
        ins.append(w_out)
        in_specs.append(_param_spec(w_out, l, single=True))
    rows = n_sub * TM
    out = pl.pallas_call(
        functools.partial(_ffn_kernel, i0=6 * k, fuse_outproj=mixers is not None, split_input=split,
                          tiles_per_batch=tpb, first=first),
        grid=(BATCH * tpb // n_sub,),
        in_specs=in_specs,
        out_specs=pl.BlockSpec((n_sub, TM, D_MODEL), lambda i: (i, 0, 0)),
        out_shape=jax.ShapeDtypeStruct((BATCH * tpb, TM, D_MODEL), f32),
        scratch_shapes=[pltpu.VMEM((rows, D_MODEL), f32), pltpu.VMEM((rows, D_MODEL), bf16)],
        compiler_params=pltpu.CompilerParams(
            dimension_semantics=("arbitrary",), vmem_limit_bytes=VMEM_LIMIT),
        name="swiglu_half_step",
    )(*ins)
    return out.reshape(BATCH, tpb * TM, D_MODEL)


INPROJ_SUBTILES = 3


def _pair_head_norm(x, g):
    lane = lax.broadcasted_iota(jnp.int32, (1, LANES), 1)
    lo = lane < NA_HEAD_DIM
    outs = []
    for p in range(NA_WIDTH // LANES):
        xb = x[:, p * LANES:(p + 1) * LANES]
        sq = xb * xb
        s_lo = jnp.sum(jnp.where(lo, sq, 0.0), axis=-1, keepdims=True)
        s_hi = jnp.sum(jnp.where(lo, 0.0, sq), axis=-1, keepdims=True)
        r = jnp.where(lo, lax.rsqrt(s_lo / NA_HEAD_DIM + EPS), lax.rsqrt(s_hi / NA_HEAD_DIM + EPS))
        outs.append((xb * r) * g[:, p * LANES:(p + 1) * LANES])
    return outs


def _inproj_kernel(h_ref, mod_ref, g_ref, wmain_ref, wtail_ref, naq_g_ref, nak_g_ref, cq_g_ref, ckv_g_ref,
                   wq_ref, wkv_ref, gq_ref, gqp_ref, gk_ref, gkp_ref, cos_ref, sin_ref,
                   qn_ref, kn_ref, vn_ref, qm_ref, km_ref, vm_ref, u_ref, z_scr, zt_scr):
    def project(g):
        rows = slice(g * TM, (g + 1) * TM)
        sel = jnp.minimum(pl.program_id(1) * INPROJ_SUBTILES + g, 1)
        xm = _modulated(h_ref[0, rows, :], g_ref[0, 0], mod_ref[0, sel, 3:4, :],
                        mod_ref[0, sel, 4:5, :]).astype(bf16)
        z_scr[rows, :] = _dot(xm, wmain_ref[0])
        zt_scr[rows, :] = _dot(xm, wtail_ref[0])

    def heads(g):
        rows = slice(g * TM, (g + 1) * TM)
        _inproj_heads(rows, z_scr, zt_scr, naq_g_ref, nak_g_ref, cq_g_ref, ckv_g_ref, wq_ref, wkv_ref,
                      gq_ref, gqp_ref, gk_ref, gkp_ref, cos_ref, sin_ref,
                      qn_ref, kn_ref, vn_ref, qm_ref, km_ref, vm_ref, u_ref)

    project(0)
    for g in range(INPROJ_SUBTILES):
        if g + 1 < INPROJ_SUBTILES:
            project(g + 1)
        heads(g)


def _inproj_heads(rows, z_scr, zt_scr, naq_g_ref, nak_g_ref, cq_g_ref, ckv_g_ref, wq_ref, wkv_ref,
                  gq_ref, gqp_ref, gk_ref, gkp_ref, cos_ref, sin_ref,
                  qn_ref, kn_ref, vn_ref, qm_ref, km_ref, vm_ref, u_ref):
    z = z_scr[rows, :]
    zt = zt_scr[rows, :]

    qn = _pair_head_norm(z[:, ZC_QA:ZC_QA + NA_WIDTH], naq_g_ref[0])
    kn = _pair_head_norm(z[:, ZC_KA:ZC_KA + NA_WIDTH], nak_g_ref[0])
    for p in range(NA_WIDTH // LANES):
        qn_ref[0, p * LANES:(p + 1) * LANES, rows] = (qn[p] * NA_Q_SCALE).T.astype(bf16)
        kn_ref[0, rows, p * LANES:(p + 1) * LANES] = kn[p].astype(bf16)
        vn_ref[0, p * LANES:(p + 1) * LANES, rows] = z[:, ZC_VA + p * LANES:ZC_VA + (p + 1) * LANES].T.astype(bf16)
    u_ref[0, rows, :] = zt[:, 0:S5_WIDTH]

    cos_t = cos_ref[rows, :]
    sin_t = sin_ref[rows, :]

    cq = z[:, ZC_CQ:ZC_CQ + Q_LORA]
    ncq = ((cq * _rms(cq, Q_LORA)) * cq_g_ref[0]).astype(bf16)
    qq = _dot(ncq, wq_ref[0])
    gq = gq_ref[0]
    gqp = gqp_ref[0]
    for hd in range(MLA_HEADS):
        pre = qq[:, hd * HEAD_PAD:(hd + 1) * HEAD_PAD]
        perm = qq[:, (MLA_HEADS + hd) * HEAD_PAD:(MLA_HEADS + hd + 1) * HEAD_PAD]
        r = _rms(pre, MLA_QK)
        qm_ref[0, hd * HEAD_PAD:(hd + 1) * HEAD_PAD, rows] = (
            ((pre * gq) * cos_t + (perm * gqp) * sin_t) * (r * MLA_Q_SCALE)).T.astype(bf16)

    ckv = z[:, ZC_CKV:ZC_CKV + KV_LORA]
    nkv = ((ckv * _rms(ckv, KV_LORA)) * ckv_g_ref[0]).astype(bf16)
    kv = _dot(nkv, wkv_ref[0])
    for p in range(MLA_WIDTH // LANES):
        lo_col = MLA_HEADS * HEAD_PAD + p * LANES
        vm_ref[0, p * LANES:(p + 1) * LANES, rows] = kv[:, lo_col:lo_col + LANES].T.astype(bf16)
    krb = zt[:, S5_WIDTH:S5_WIDTH + LANES]
    lane = lax.broadcasted_iota(jnp.int32, (1, LANES), 1)
    rope_lanes = (lane >= MLA_NOPE) & (lane < MLA_QK)
    kr_a = jnp.where(rope_lanes, pltpu.roll(krb, MLA_NOPE, axis=1), 0.0)
    kr_b = jnp.where(rope_lanes, pltpu.roll(krb, MLA_ROPE, axis=1), 0.0)
    gk = gk_ref[0]
    gkp = gkp_ref[0]
    rot_part = (kr_b * gkp) * sin_t
    for hd in range(MLA_HEADS):
        kfull = kv[:, hd * HEAD_PAD:(hd + 1) * HEAD_PAD] + kr_a
        r = _rms(kfull, MLA_QK)
        km_ref[0, rows, hd * HEAD_PAD:(hd + 1) * HEAD_PAD] = (
            ((kfull * gk) * cos_t + rot_part) * r).astype(bf16)


def _inproj_call(h, mod, norm_g, l, params, cos_t, sin_t):
    qk_w = MLA_HEADS * HEAD_PAD
    rows = INPROJ_SUBTILES * TM

    def tok(width):
        return pl.BlockSpec((1, rows, width), lambda b, j: (b, j, 0))

    def tok_t(width):
        return pl.BlockSpec((1, width, rows), lambda b, j: (b, 0, j))

    tab_spec = pl.BlockSpec((rows, LANES), lambda b, j: (j, 0))
    mod_spec = pl.BlockSpec((1, 2, N_MOD, D_MODEL), lambda b, j: (b, 0, 0, 0))
    names = ["win_main", "win_tail", "naq_g", "nak_g", "cq_g", "ckv_g", "wq", "wkv", "gq", "gqp", "gk", "gkp"]
    return pl.pallas_call(
        _inproj_kernel,
        grid=(BATCH, N_TILES // INPROJ_SUBTILES),
        in_specs=([tok(D_MODEL), mod_spec, _param_spec(norm_g, l, 1)]
                  + [_param_spec(params[n], l) for n in names] + [tab_spec, tab_spec]),
        out_specs=[tok_t(NA_WIDTH), tok(NA_WIDTH), tok_t(NA_WIDTH),
                   tok_t(qk_w), tok(qk_w), tok_t(MLA_WIDTH), tok(S5_WIDTH)],
        out_shape=[
            jax.ShapeDtypeStruct((BATCH, NA_WIDTH, S_ALL), bf16),
            jax.ShapeDtypeStruct((BATCH, S_ALL, NA_WIDTH), bf16),
            jax.ShapeDtypeStruct((BATCH, NA_WIDTH, S_ALL), bf16),
            jax.ShapeDtypeStruct((BATCH, qk_w, S_ALL), bf16),
            jax.ShapeDtypeStruct((BATCH, S_ALL, qk_w), bf16),
            jax.ShapeDtypeStruct((BATCH, MLA_WIDTH, S_ALL), bf16),
            jax.ShapeDtypeStruct((BATCH, S_ALL, S5_WIDTH), f32),
        ],
        scratch_shapes=[pltpu.VMEM((rows, Z_MAIN), f32), pltpu.VMEM((rows, Z_TAIL), f32)],
        compiler_params=pltpu.CompilerParams(
            dimension_semantics=("arbitrary", "arbitrary"), vmem_limit_bytes=VMEM_LIMIT),
        name="in_proj_heads",
    )(h, mod, norm_g, *[params[n] for n in names], cos_t, sin_t)


def _pair_block_diag(qt):
    d = qt.shape[0] // 2
    z = jnp.zeros((d, qt.shape[1]), qt.dtype)
    return jnp.concatenate([jnp.concatenate([qt[:d], z], axis=0),
                            jnp.concatenate([z, qt[d:]], axis=0)], axis=1)


def _pair_scores(dst_ref, key_blocks, q_bd, bias_blocks):
    r = 0
    for kk, bias in zip(key_blocks, bias_blocks):
        s = _dot(kk, q_bd)
        dst_ref[r:r + kk.shape[0], :] = s if bias is None else s + bias
        r += kk.shape[0]


def _pair_softmax_pv(src_ref, vt_blocks):
    n_keys = sum(vt.shape[1] for vt in vt_blocks)
    m = jnp.max(src_ref[0:n_keys, :], axis=0, keepdims=True)
    den = acc = None
    r = 0
    for vt in vt_blocks:
        p = jnp.exp2(src_ref[r:r + vt.shape[1], :] - m)
        li = jnp.sum(p, axis=0, keepdims=True)
        oi = _dot(vt, p.astype(bf16))
        den = li if den is None else den + li
        acc = oi if acc is None else acc + oi
        r += vt.shape[1]
    o = acc / den
    dv = o.shape[0] // 2
    return jnp.concatenate([o[:dv, :TM], o[dv:, TM:]], axis=0).T.astype(bf16)


def _pipelined_tiles(scores, finish, s_a, s_b, first_tile, last_tile, lead_in=None):
    scores(first_tile, s_a)
    if lead_in is not None:
        lead_in(s_b)

    def body(i, carry):
        t0 = first_tile + 2 * i
        scores(t0 + 1, s_b)
        finish(t0, s_a)
        scores(t0 + 2, s_a)
        finish(t0 + 1, s_b)
        return carry

    lax.fori_loop(0, (last_tile - first_tile + 1) // 2 - 1, body, 0)
    scores(last_tile, s_b)
    finish(last_tile - 1, s_a)
    finish(last_tile, s_b)


NA_TILE_ROWS = TM // GRID_W
NA_UNION_ROWS = 12
NA_UNION_KEYS = NA_UNION_ROWS * GRID_W
NA_PATTERNS = 3
NA_BIAS_SHIFTS = 2 * WIN_ROWS


def _na_window_lo(pat, rr):
    return (-rr, -(WIN_ROWS // 2), -(WIN_ROWS // 2) - rr)[pat]


def _na_fill_bias(blk_ref, bias_scr):
    lane = lax.broadcasted_iota(jnp.int32, (1, LANES), 1)
    left = lane < GRID_W
    neg = jnp.full((GRID_W, LANES), NEG_INF, f32)
    for pat in range(NA_PATTERNS):
        for ii in range(NA_UNION_ROWS):
            for t in range(NA_TILE_ROWS // 2):
                d = ii - NA_TILE_ROWS * pat - 2 * t
                ok_l = 0 <= d - _na_window_lo(pat, 2 * t) < WIN_ROWS
                ok_r = 0 <= (d - 1) - _na_window_lo(pat, 2 * t + 1) < WIN_ROWS
                for hh in range(2):
                    if ok_l or ok_r:
                        blk = blk_ref[0, hh, d + WIN_ROWS - 1]
                        if not ok_l:
                            blk = jnp.where(left, NEG_INF, blk)
                        if not ok_r:
                            blk = jnp.where(left, blk, NEG_INF)
                    else:
                        blk = neg
                    c0 = hh * TM + t * LANES
                    bias_scr[pat, ii * GRID_W:(ii + 1) * GRID_W, c0:c0 + LANES] = blk


def _query_tile(qt_ref, t):
    return _pair_block_diag(qt_ref[0, :, pl.ds(pl.multiple_of(t * TM, TM), TM)])


def _out_rows(t, with_ctx):
    return pl.ds(pl.multiple_of((t - (0 if with_ctx else 1)) * TM, TM), TM)


def _na_kernel(qt_ref, k_ref, vt_ref, blk_ref, o_ref, bias_scr, s_a, s_b, *, with_ctx):
    @pl.when(pl.program_id(1) == 0)
    def _():
        _na_fill_bias(blk_ref, bias_scr)

    kc = k_ref[0, 0:CTX_LEN, :]
    vtc = vt_ref[0, :, 0:CTX_LEN]

    def window(t):
        r0 = (t - 1) * NA_TILE_ROWS
        base = jnp.clip(r0 - WIN_ROWS // 2, 0, GRID_ROWS - NA_UNION_ROWS)
        pat = (r0 - base) // NA_TILE_ROWS
        start = pl.multiple_of(CTX_LEN + base * GRID_W, NA_TILE_ROWS * GRID_W)
        return pat, pl.ds(start, NA_UNION_KEYS)

    def scores(t, dst):
        pat, keys = window(t)
        _pair_scores(dst, [k_ref[0, keys, :], kc], _query_tile(qt_ref, t), [bias_scr[pat], None])

    def finish(t, src):
        _, keys = window(t)
        o_ref[0, _out_rows(t, with_ctx), :] = _pair_softmax_pv(src, [vt_ref[0, :, keys], vtc])

    def ctx_finish(buf):
        o_ref[0, 0:TM, :] = _pair_softmax_pv(buf, [vtc])

    if with_ctx:
        _pair_scores(s_b, [kc], _query_tile(qt_ref, 0), [None])
    _pipelined_tiles(scores, finish, s_a, s_b, 1, N_TILES - 1, ctx_finish if with_ctx else None)


def _na_call(qnt, kn, vnt, bias_blocks, l, with_ctx):
    n_pairs = NA_WIDTH // LANES
    s_out = S_ALL if with_ctx else SEQ
    logits = pltpu.VMEM((NA_UNION_KEYS + CTX_LEN, 2 * TM), f32)
    return pl.pallas_call(
        functools.partial(_na_kernel, with_ctx=with_ctx),
        grid=(n_pairs, BATCH),
        in_specs=[
            pl.BlockSpec((1, LANES, S_ALL), lambda p, b: (b, p, 0)),
            pl.BlockSpec((1, S_ALL, LANES), lambda p, b: (b, 0, p)),
            pl.BlockSpec((1, LANES, S_ALL), lambda p, b: (b, p, 0)),
            pl.BlockSpec((1, 2, NA_BIAS_SHIFTS, GRID_W, LANES), lambda p, b: (l, p, 0, 0, 0)),
        ],
        out_specs=pl.BlockSpec((1, s_out, LANES), lambda p, b: (b, 0, p)),
        out_shape=jax.ShapeDtypeStruct((BATCH, s_out, NA_WIDTH), bf16),
        scratch_shapes=[pltpu.VMEM((NA_PATTERNS, NA_UNION_KEYS, 2 * TM), f32), logits, logits],
        compiler_params=pltpu.CompilerParams(
            dimension_semantics=("arbitrary", "arbitrary"), vmem_limit_bytes=VMEM_LIMIT),
        name="na_attention",
    )(qnt, kn, vnt, bias_blocks)


MLA_KEY_BLOCK = 768


def _mla_kernel(qt_ref, k_ref, vt_ref, o_ref, s_a, s_b, *, with_ctx):
    def scores(t, dst):
        _pair_scores(dst, [k_ref[0]], _query_tile(qt_ref, t), [None])

    def finish(t, src):
        vts = [vt_ref[0, :, s0:s0 + MLA_KEY_BLOCK] for s0 in range(0, S_ALL, MLA_KEY_BLOCK)]
        o_ref[0, _out_rows(t, with_ctx), :] = _pair_softmax_pv(src, vts)

    def ctx_finish(buf):
        o_ref[0, 0:TM, :] = _pair_softmax_pv(buf, [vt_ref[0, :, 0:CTX_LEN]])

    if with_ctx:
        _pair_scores(s_b, [k_ref[0, 0:CTX_LEN, :]], _query_tile(qt_ref, 0), [None])
    _pipelined_tiles(scores, finish, s_a, s_b, 1, N_TILES - 1, ctx_finish if with_ctx else None)


def _mla_call(qmt, km, vmt, with_ctx):
    n_pairs = MLA_HEADS // 2
    s_out = S_ALL if with_ctx else SEQ
    logits = pltpu.VMEM((S_ALL, 2 * TM), f32)
    return pl.pallas_call(
        functools.partial(_mla_kernel, with_ctx=with_ctx),
        grid=(BATCH, n_pairs),
        in_specs=[
            pl.BlockSpec((1, 2 * HEAD_PAD, S_ALL), lambda b, p: (b, p, 0)),
            pl.BlockSpec((1, S_ALL, 2 * HEAD_PAD), lambda b, p: (b, 0, p)),
            pl.BlockSpec((1, LANES, S_ALL), lambda b, p: (b, p, 0)),
        ],
        out_specs=pl.BlockSpec((1, s_out, LANES), lambda b, p: (b, 0, p)),
        out_shape=jax.ShapeDtypeStruct((BATCH, s_out, MLA_WIDTH), bf16),
        scratch_shapes=[logits, logits],
        compiler_params=pltpu.CompilerParams(
            dimension_semantics=("arbitrary", "arbitrary"), vmem_limit_bytes=VMEM_LIMIT),
        name="mla_attention",
    )(qmt, km, vmt)


S5_BLOCK_T = 16
S5_BLOCK_ROWS = S5_BLOCK_T * BATCH


def _s5_scan_kernel(*refs, reverse, finish):
    u_ref, bmat_ref, cmat_ref, lam_ref = refs[:4]
    if finish:
        y_other_ref, d_ref, wglu_ref, bglu_ref = refs[4:8]
    y_ref, bu_ref, h_ref, st_ref = refs[-4:]

    @pl.when(pl.program_id(0) == 0)
    def _():
        st_ref[...] = jnp.zeros_like(st_ref)

    def rows(k):
        return slice(k * S5_BLOCK_ROWS, (k + 1) * S5_BLOCK_ROWS)

    def project(k):
        bu_ref[rows(k), :] = _dot(u_ref[rows(k), :].astype(bf16), bmat_ref[0, 0])

    def readout(k):
        y = _dot(h_ref[rows(k), :].astype(bf16), cmat_ref[0, 0])
        if not finish:
            y_ref[rows(k), :] = y
            return
        y = (y_other_ref[rows(k), :] + y) + d_ref[0] * u_ref[rows(k), :]
        c0 = math.sqrt(2.0 / math.pi)
        gl = 0.5 * y * (1.0 + jnp.tanh(c0 * (y + 0.044715 * (y * y * y))))
        o = _dot(gl.astype(bf16), wglu_ref[0]) + bglu_ref[0]
        y_ref[rows(k), :] = (o[:, :S5_WIDTH] * _sigmoid(o[:, S5_WIDTH:])).astype(bf16)

    n_blocks = S5_T // S5_BLOCK_T
    order = list(range(n_blocks))[::-1] if reverse else list(range(n_blocks))
    steps = list(range(S5_BLOCK_T))[::-1] if reverse else list(range(S5_BLOCK_T))
    hr = st_ref[:, 0:S5_LANES]
    hi = st_ref[:, S5_LANES:]
    project(order[0])
    for n, k in enumerate(order):
        if n + 1 < n_blocks:
            project(order[n + 1])
        for t in steps:
            r0 = k * S5_BLOCK_ROWS + t * BATCH
            lr = lam_ref[0, 0, :, 0:S5_LANES]
            li = lam_ref[0, 0, :, S5_LANES:]
            nr = (lr * hr - li * hi) + bu_ref[r0:r0 + BATCH, 0:S5_LANES]
            ni = (lr * hi + li * hr) + bu_ref[r0:r0 + BATCH, S5_LANES:]
            h_ref[r0:r0 + BATCH, 0:S5_LANES] = nr
            h_ref[r0:r0 + BATCH, S5_LANES:] = ni
            hr, hi = nr, ni
        if n >= 1:
            readout(order[n - 1])
    readout(order[-1])
    st_ref[:, 0:S5_LANES] = hr
    st_ref[:, S5_LANES:] = hi


def _s5_scan_call(u_t, p, l, d, y_other=None):
    reverse = d == 1
    finish = y_other is not None
    if reverse:
        def chunk(i):
            return jnp.where(i < S5_CTX_CHUNKS, S5_CTX_CHUNKS - 1 - i, S5_CHUNKS + S5_CTX_CHUNKS - 1 - i)
    else:
        def chunk(i):
            return i
    row_spec = pl.BlockSpec((S5_ROWS, S5_WIDTH), lambda i: (chunk(i), 0))
    ins = [u_t, p["bmat"], p["cmat"], p["lam"]]
    in_specs = [row_spec, _param_spec(p["bmat"], l, d), _param_spec(p["cmat"], l, d), _param_spec(p["lam"], l, d)]
    if finish:
        ins += [y_other, p["s5_d"], p["w_glu"], p["b_glu"]]
        in_specs += [row_spec, _param_spec(p["s5_d"], l), _param_spec(p["w_glu"], l), _param_spec(p["b_glu"], l)]
    return pl.pallas_call(
        functools.partial(_s5_scan_kernel, reverse=reverse, finish=finish),
        grid=(S5_CHUNKS,),
        in_specs=in_specs,
        out_specs=row_spec,
        out_shape=jax.ShapeDtypeStruct((S_ALL * BATCH, S5_WIDTH), bf16 if finish else f32),
        scratch_shapes=[pltpu.VMEM((S5_ROWS, 2 * S5_LANES), f32), pltpu.VMEM((S5_ROWS, 2 * S5_LANES), f32),
                        pltpu.VMEM((BATCH, 2 * S5_LANES), f32)],
        compiler_params=pltpu.CompilerParams(
            dimension_semantics=("arbitrary",), vmem_limit_bytes=VMEM_LIMIT),
        name="s5_scan_bwd" if reverse else "s5_scan_fwd",
    )(*ins)


def _rope_perm():
    half = MLA_ROPE // 2
    quarter = half // 2
    idx, sign = [], []
    for j in range(MLA_ROPE):
        if (j % half) < quarter:
            idx.append(j + quarter)
            sign.append(-1.0)
        else:
            idx.append(j - quarter)
            sign.append(1.0)
    return jnp.array(idx, jnp.int32), jnp.array(sign, f32)


def _rope_tables():
    quarter = MLA_ROPE // 4
    t = jnp.arange(SEQ)
    row = (t // GRID_W).astype(f32)
    col = (t % GRID_W).astype(f32)
    inv = ROPE_THETA ** (-jnp.arange(quarter, dtype=f32) / quarter)
    ang = jnp.concatenate([row[:, None] * inv] * 2 + [col[:, None] * inv] * 2, axis=-1)
    ang = jnp.concatenate([jnp.zeros((CTX_LEN, MLA_ROPE), f32), ang], axis=0)
    ones = jnp.ones((S_ALL, MLA_NOPE), f32)
    pad = jnp.zeros((S_ALL, HEAD_PAD - MLA_QK), f32)
    cos_t = jnp.concatenate([ones, jnp.cos(ang), pad], axis=-1)
    sin_t = jnp.concatenate([0.0 * ones, jnp.sin(ang), pad], axis=-1)
    return cos_t, sin_t


def _na_bias_blocks(rpb):
    cq = jnp.arange(GRID_W)[None, :]
    kc = jnp.arange(GRID_W)[:, None]
    d_col = jnp.clip(kc - cq, -(WIN_COLS - 1), WIN_COLS - 1) + WIN_COLS - 1
    col_start = jnp.clip(cq - WIN_COLS // 2, 0, GRID_W - WIN_COLS)
    in_win = (kc >= col_start) & (kc < col_start + WIN_COLS)
    onehot = (d_col[:, :, None] == jnp.arange(2 * WIN_COLS - 1)[None, None, :]).astype(f32)
    blocks = jnp.einsum('lhdc,kqc->lhdkq', rpb.astype(f32), onehot, precision=lax.Precision.HIGHEST)
    blocks = jnp.where(in_win, blocks * LOG2E, NEG_INF)
    neg = jnp.full(blocks.shape[:2] + (1, GRID_W, GRID_W), NEG_INF, f32)
    padded = jnp.concatenate([neg, blocks, neg], axis=2)
    return jnp.concatenate([padded[:, :, 1:], padded[:, :, :-1]], axis=-1)


def _pad_heads(w, width):
    lead = w.shape[:-1]
    w = w.reshape(lead + (MLA_HEADS, width))
    w = jnp.pad(w, [(0, 0)] * len(lead) + [(0, 0), (0, HEAD_PAD - width)])
    return w.reshape(lead + (MLA_HEADS * HEAD_PAD,))


def _block_diag_groups(w):
    eye = jnp.eye(S5_GROUPS, dtype=f32)
    lead = w.shape[:-3]
    a, b = w.shape[-2:]
    full = w[..., :, :, None, :] * eye[:, None, :, None]
    return full.reshape(lead + (S5_GROUPS * a, S5_GROUPS * b))


def _prepare_params(ffn_w_gu, ffn_w_down, w_in, w_out, na_qk_g, na_rpb, mla_cq_g, mla_ckv_g, mla_w_uq,
                    mla_w_ukv, mla_qk_g, s5_lam_re, s5_lam_im, s5_log_dt, s5_b_re, s5_b_im, s5_c_re,
                    s5_c_im, s5_d, s5_w_glu, s5_b_glu):
    p = {}
    perm_idx, perm_sign = _rope_perm()
    p["wgu"] = ffn_w_gu.astype(bf16)
    p["wd"] = ffn_w_down.astype(bf16)
    p["w_out"] = w_out.astype(bf16)
    o_kr = Z_MAIN
    o_u = o_kr + MLA_ROPE
    w_kr = w_in[:, :, o_kr:o_kr + MLA_ROPE]
    p["win_main"] = w_in[:, :, :Z_MAIN].astype(bf16)
    p["win_tail"] = jnp.concatenate([
        w_in[:, :, o_u:o_u + S5_WIDTH], w_kr, w_kr[:, :, perm_idx] * perm_sign,
        jnp.zeros((DEPTH, D_MODEL, LANES - 2 * MLA_ROPE), f32)], axis=-1).astype(bf16)
    p["naq_g"] = jnp.tile(na_qk_g[:, 0], (1, NA_HEADS))[:, None, :]
    p["nak_g"] = jnp.tile(na_qk_g[:, 1], (1, NA_HEADS))[:, None, :]
    p["cq_g"] = mla_cq_g[:, None, :]
    p["ckv_g"] = mla_ckv_g[:, None, :]
    wuq = mla_w_uq.reshape(DEPTH, Q_LORA, MLA_HEADS, MLA_QK)
    partner = jnp.concatenate([jnp.zeros((DEPTH, Q_LORA, MLA_HEADS, MLA_NOPE), f32),
                               wuq[..., MLA_NOPE:][..., perm_idx] * perm_sign], axis=-1)
    p["wq"] = jnp.concatenate([_pad_heads(mla_w_uq, MLA_QK),
                               _pad_heads(partner.reshape(DEPTH, Q_LORA, -1), MLA_QK)], axis=-1).astype(bf16)
    wukv = mla_w_ukv.reshape(DEPTH, KV_LORA, MLA_HEADS, MLA_NOPE + MLA_V_DIM)
    p["wkv"] = jnp.concatenate([_pad_heads(wukv[..., :MLA_NOPE].reshape(DEPTH, KV_LORA, -1), MLA_NOPE),
                                wukv[..., MLA_NOPE:].reshape(DEPTH, KV_LORA, -1)], axis=-1).astype(bf16)

    def pad_gain(g):
        z = jnp.zeros((DEPTH, HEAD_PAD - MLA_QK), f32)
        full = jnp.concatenate([g, z], axis=-1)[:, None, :]
        part = jnp.concatenate([jnp.zeros((DEPTH, MLA_NOPE), f32), g[:, MLA_NOPE:][:, perm_idx], z],
                               axis=-1)[:, None, :]
        return full, part

    p["gq"], p["gqp"] = pad_gain(mla_qk_g[:, 0])
    p["gk"], p["gkp"] = pad_gain(mla_qk_g[:, 1])
    p["na_bias"] = _na_bias_blocks(na_rpb)
    lre = s5_lam_re.astype(f32)
    lim = s5_lam_im.astype(f32)
    dt = jnp.exp(s5_log_dt.astype(f32))[..., None]
    mag = jnp.exp(lre * dt)
    bar_re = mag * jnp.cos(lim * dt)
    bar_im = mag * jnp.sin(lim * dt)
    den = lre * lre + lim * lim
    q_re = ((bar_re - 1.0) * lre + bar_im * lim) / den
    q_im = (bar_im * lre - (bar_re - 1.0) * lim) / den
    bre = s5_b_re.astype(f32)
    bim = s5_b_im.astype(f32)
    bbar_re = q_re[..., None] * bre - q_im[..., None] * bim
    bbar_im = q_re[..., None] * bim + q_im[..., None] * bre
    p["bmat"] = jnp.concatenate([_block_diag_groups(jnp.swapaxes(bbar_re, -1, -2)),
                                 _block_diag_groups(jnp.swapaxes(bbar_im, -1, -2))], axis=-1).astype(bf16)
    p["cmat"] = jnp.concatenate([_block_diag_groups(jnp.swapaxes(s5_c_re.astype(f32), -1, -2)),
                                 -_block_diag_groups(jnp.swapaxes(s5_c_im.astype(f32), -1, -2))],
                                axis=-2).astype(bf16)
    lam_row = jnp.concatenate([bar_re.reshape(DEPTH, 2, S5_LANES), bar_im.reshape(DEPTH, 2, S5_LANES)], axis=-1)
    p["lam"] = jnp.broadcast_to(lam_row[:, :, None, :], (DEPTH, 2, BATCH, 2 * S5_LANES))
    p["s5_d"] = s5_d[:, None, :]
    p["w_glu"] = s5_w_glu.astype(bf16)
    p["b_glu"] = s5_b_glu[:, None, :]
    return p


def kernel(x, c, ctx, c_ctx, w_mod, b_mod, norm_g, ffn_w_gu, ffn_w_down, w_in, w_out, na_qk_g, na_rpb, mla_cq_g, mla_ckv_g, mla_w_uq, mla_w_ukv, mla_qk_g, s5_lam_re, s5_lam_im, s5_log_dt, s5_b_re, s5_b_im, s5_c_re, s5_c_im, s5_d, s5_w_glu, s5_b_glu):
    assert x.shape == (BATCH, SEQ, D_MODEL) and ctx.shape == (BATCH, CTX_LEN, D_MODEL)
    mod_rows = 16
    cvec = jnp.concatenate([c, c_ctx[None, :], jnp.zeros((mod_rows - BATCH - 1, D_MODEL), f32)], axis=0)
    mod_all = _mod_call(cvec, w_mod, b_mod).reshape(DEPTH, mod_rows, N_MOD, D_MODEL)
    cos_t, sin_t = _rope_tables()
    p = _prepare_params(ffn_w_gu, ffn_w_down, w_in, w_out, na_qk_g, na_rpb, mla_cq_g, mla_ckv_g,
                        mla_w_uq, mla_w_ukv, mla_qk_g, s5_lam_re, s5_lam_im, s5_log_dt, s5_b_re,
                        s5_b_im, s5_c_re, s5_c_im, s5_d, s5_w_glu, s5_b_glu)
    norm_g4 = norm_g[:, :, None, :]
    h = (ctx, x)
    for l in range(DEPTH):
        mod_c = jnp.broadcast_to(mod_all[l, BATCH][None], (BATCH, N_MOD, D_MODEL))
        mod = jnp.stack([mod_c, mod_all[l, :BATCH]], axis=1)
        need_ctx = l < DEPTH - 1
        h = _ffn_call(h, mod, norm_g4, p["wgu"], p["wd"], l, 0)
        qn, kn, vn, qm, km, vm, u = _inproj_call(h, mod, norm_g4, l, p, cos_t, sin_t)
        a = _na_call(qn, kn, vn, p["na_bias"], l, need_ctx)
        bm = _mla_call(qm, km, vm, need_ctx)
        u_t = u.transpose(1, 0, 2).reshape(S_ALL * BATCH, S5_WIDTH)
        yf = _s5_scan_call(u_t, p, l, 0)
        s_t = _s5_scan_call(u_t, p, l, 1, y_other=yf)
        if not need_ctx:
            s_t = s_t[CTX_LEN * BATCH:]
        s = s_t.reshape(-1, BATCH, S5_WIDTH).transpose(1, 0, 2)
        h = _ffn_call(h, mod, norm_g4, p["wgu"], p["wd"], l, 1, mixers=(a, bm, s), w_out=p["w_out"],
                      out_ctx=need_ctx)
    return h
```

```python
import functools
import math

import jax
import jax.numpy as jnp
from jax import lax
from jax.experimental import pallas as pl
from jax.experimental.pallas import tpu as pltpu

D_MODEL = 1024
BATCH = 8
SEQ = 2048
DEPTH = 2
CTX_LEN = 256
S_ALL = CTX_LEN + SEQ
GRID_W = 64
GRID_ROWS = SEQ // GRID_W
NA_HEAD_DIM = 64
NA_WIDTH = 384
NA_HEADS = 6
WIN_ROWS = 8
WIN_COLS = 16
MLA_V_DIM = 64
MLA_WIDTH = 384
MLA_HEADS = 6
MLA_NOPE = 64
MLA_ROPE = 32
MLA_QK = 96
Q_LORA = 384
KV_LORA = 256
S5_WIDTH = 256
S5_GROUP = 16
S5_GROUPS = 16
S5_STATE = 64
S5_LANES = S5_GROUPS * S5_STATE
D_FF = 2816
ROPE_THETA = 10000.0
EPS = 1e-6
N_MOD = 9
NEG_INF = -1e30

LANES = 128
TM = 256
N_TILES = S_ALL // TM
MXU_DIM = 256
FF_CHUNKS = ((0, 6 * MXU_DIM), (6 * MXU_DIM, D_FF))
HEAD_PAD = 128
S5_T = 256
S5_ROWS = S5_T * BATCH
S5_CHUNKS = S_ALL // S5_T
S5_CTX_CHUNKS = CTX_LEN // S5_T
VMEM_LIMIT = 56 * 1024 * 1024

ZC_QA, ZC_KA, ZC_VA, ZC_CQ, ZC_CKV = 0, 384, 768, 1152, 1536
Z_MAIN = 1792
Z_TAIL = S5_WIDTH + LANES

LOG2E = math.log2(math.e)
NA_Q_SCALE = NA_HEAD_DIM ** -0.5 * LOG2E
MLA_Q_SCALE = MLA_QK ** -0.5 * LOG2E

f32 = jnp.float32
bf16 = jnp.bfloat16


def _dot(a, b):
    return jnp.dot(a, b, preferred_element_type=f32)


def _sigmoid(x):
    return 1.0 / (1.0 + jnp.exp(-x))


def _rms(x, n):
    return lax.rsqrt(jnp.sum(x * x, axis=-1, keepdims=True) / n + EPS)


def _modulated(x, g, shift, scale):
    y = x * _rms(x, D_MODEL)
    return (y * g) * (1.0 + scale) + shift


def _mod_kernel(c_ref, w_ref, b_ref, o_ref):
    c = c_ref[...]
    a = (c * _sigmoid(c)).astype(bf16)
    o_ref[0] = _dot(a, w_ref[0].astype(bf16)) + b_ref[0]


def _mod_call(cvec, w_mod, b_mod):
    rows = cvec.shape[0]
    return pl.pallas_call(
        _mod_kernel,
        grid=(DEPTH, N_MOD),
        in_specs=[
            pl.BlockSpec((rows, D_MODEL), lambda l, j: (0, 0)),
            pl.BlockSpec((1, D_MODEL, D_MODEL), lambda l, j: (l, 0, j)),
            pl.BlockSpec((1, 1, D_MODEL), lambda l, j: (l, 0, j)),
        ],
        out_specs=pl.BlockSpec((1, rows, D_MODEL), lambda l, j: (l, 0, j)),
        out_shape=jax.ShapeDtypeStruct((DEPTH, rows, N_MOD * D_MODEL), f32),
        compiler_params=pltpu.CompilerParams(
            dimension_semantics=("arbitrary", "arbitrary"), vmem_limit_bytes=VMEM_LIMIT),
        name="adaln_mod",
    )(cvec, w_mod, b_mod.reshape(DEPTH, 1, N_MOD * D_MODEL))


def _param_spec(arr, *lead, single=False):
    tail = arr.shape[len(lead):]
    idx = tuple(lead) + (0,) * len(tail)
    mode = pl.Buffered(1) if single else None
    return pl.BlockSpec((1,) * len(lead) + tail, lambda *_: idx, pipeline_mode=mode)


FFN_SUBTILES = 4


def _ffn_kernel(*refs, i0, fuse_outproj, split_input, tiles_per_batch, first):
    n_sub = FFN_SUBTILES
    refs = list(refs)
    n_src = 2 if split_input else 1
    srcs = [refs[g * n_src:(g + 1) * n_src] for g in range(n_sub)]
    refs = refs[n_sub * n_src:]
    mods = refs[:n_sub]
    g_ref, wgu_ref, wd_ref = refs[n_sub:n_sub + 3]
    refs = refs[n_sub + 3:]
    if fuse_outproj:
        mixers = [refs[3 * g:3 * g + 3] for g in range(n_sub)]
        wo_ref = refs[3 * n_sub]
        refs = refs[3 * n_sub + 1:]
    o_ref, x_scr, xm_scr = refs

    def mod_set(g):
        j = (pl.program_id(0) * n_sub + g) % tiles_per_batch + first
        return j, jnp.minimum(j, 1)

    def prologue(g):
        j, sel = mod_set(g)
        rows = slice(g * TM, (g + 1) * TM)
        mod_ref = mods[g]
        if split_input:
            x = jnp.where(j == 0, srcs[g][0][0], srcs[g][1][0])
        else:
            x = srcs[g][0][0]
        if fuse_outproj:
            a_ref, b_ref, s_ref = mixers[g]
            mixed = jnp.concatenate([a_ref[0], b_ref[0], s_ref[0]], axis=-1)
            x = x + mod_ref[0, sel, 5:6, :] * _dot(mixed, wo_ref[0])
        x_scr[rows, :] = x
        shift = mod_ref[0, sel, i0:i0 + 1, :]
        scale = mod_ref[0, sel, i0 + 1:i0 + 2, :]
        xm_scr[rows, :] = _modulated(x, g_ref[0, 0], shift, scale).astype(bf16)

    def chunk(g, c, acc):
        rows = slice(g * TM, (g + 1) * TM)
        lo, hi = FF_CHUNKS[c]
        gt = _dot(xm_scr[rows, :], wgu_ref[0, 0, :, lo:hi])
        up = _dot(xm_scr[rows, :], wgu_ref[0, 0, :, D_FF + lo:D_FF + hi])
        a = ((gt * _sigmoid(gt)) * up).astype(bf16)
        part = _dot(a, wd_ref[0, 0, lo:hi, :])
        return part if acc is None else acc + part

    def epilogue(g, acc):
        _, sel = mod_set(g)
        gate = mods[g][0, sel, i0 + 2:i0 + 3, :]
        o_ref[g] = x_scr[g * TM:(g + 1) * TM, :] + (0.5 * gate) * acc

    assert len(FF_CHUNKS) == 2
    prologue(0)
    acc_prev = chunk(0, 0, None)
    for g in range(1, n_sub):
        prologue(g)
        acc_prev = chunk(g - 1, 1, acc_prev)
        acc_cur = chunk(g, 0, None)
        epilogue(g - 1, acc_prev)
        acc_prev = acc_cur
    acc_prev = chunk(n_sub - 1, 1, acc_prev)
    epilogue(n_sub - 1, acc_prev)


def _ffn_call(h, mod, norm_g, wgu, wd, l, k, *, mixers=None, w_out=None, out_ctx=True):
    split = isinstance(h, tuple)
    first = 0 if out_ctx else 1
    tpb = N_TILES - first
    n_sub = FFN_SUBTILES

    def tile_of(i, g):
        t = i * n_sub + g
        return t // tpb, t % tpb

    ins, in_specs = [], []
    for g in range(n_sub):
        if split:
            ins += list(h)
            in_specs += [
                pl.BlockSpec((1, TM, D_MODEL), lambda i, g=g: (tile_of(i, g)[0], 0, 0)),
                pl.BlockSpec((1, TM, D_MODEL),
                             lambda i, g=g: (tile_of(i, g)[0], jnp.maximum(tile_of(i, g)[1] + first - 1, 0), 0))]
        else:
            ins.append(h)
            in_specs.append(pl.BlockSpec(
                (1, TM, D_MODEL), lambda i, g=g: (tile_of(i, g)[0], tile_of(i, g)[1] + first, 0)))
    for g in range(n_sub):
        ins.append(mod)
        in_specs.append(pl.BlockSpec((1, 2, N_MOD, D_MODEL), lambda i, g=g: (tile_of(i, g)[0], 0, 0, 0)))
    ins += [norm_g, wgu, wd]
    in_specs += [_param_spec(norm_g, l, 2 * k), _param_spec(wgu, l, k, single=True),
                 _param_spec(wd, l, k, single=True)]
    if mixers is not None:
        for g in range(n_sub):
            for arr in mixers:
                ins.append(arr)
                in_specs.append(pl.BlockSpec((1, TM, arr.shape[-1]), lambda i, g=g: tile_of(i, g) + (0,)))
        ins.append(w_out)
        in_specs.append(_param_spec(w_out, l, single=True))
    rows = n_sub * TM
    out = pl.pallas_call(
        functools.partial(_ffn_kernel, i0=6 * k, fuse_outproj=mixers is not None, split_input=split,
                          tiles_per_batch=tpb, first=first),
        grid=(BATCH * tpb // n_sub,),
        in_specs=in_specs,
        out_specs=pl.BlockSpec((n_sub, TM, D_MODEL), lambda i: (i, 0, 0)),
        out_shape=jax.ShapeDtypeStruct((BATCH * tpb, TM, D_MODEL), f32),
        scratch_shapes=[pltpu.VMEM((rows, D_MODEL), f32), pltpu.VMEM((rows, D_MODEL), bf16)],
        compiler_params=pltpu.CompilerParams(
            dimension_semantics=("arbitrary",), vmem_limit_bytes=VMEM_LIMIT),
        name="swiglu_half_step",
    )(*ins)
    return out.reshape(BATCH, tpb * TM, D_MODEL)


INPROJ_SUBTILES = 3


def _pair_head_norm(x, g):
    lane = lax.broadcasted_iota(jnp.int32, (1, LANES), 1)
    lo = lane < NA_HEAD_DIM
    outs = []
    for p in range(NA_WIDTH // LANES):
        xb = x[:, p * LANES:(p + 1) * LANES]
        sq = xb * xb
        s_lo = jnp.sum(jnp.where(lo, sq, 0.0), axis=-1, keepdims=True)
        s_hi = jnp.sum(jnp.where(lo, 0.0, sq), axis=-1, keepdims=True)
        r = jnp.where(lo, lax.rsqrt(s_lo / NA_HEAD_DIM + EPS), lax.rsqrt(s_hi / NA_HEAD_DIM + EPS))
        outs.append((xb * r) * g[:, p * LANES:(p + 1) * LANES])
    return outs


def _inproj_kernel(h_ref, mod_ref, g_ref, wmain_ref, wtail_ref, naq_g_ref, nak_g_ref, cq_g_ref, ckv_g_ref,
                   wq_ref, wkv_ref, gq_ref, gqp_ref, gk_ref, gkp_ref, cos_ref, sin_ref,
                   qn_ref, kn_ref, vn_ref, qm_ref, km_ref, vm_ref, u_ref, z_scr, zt_scr):
    def project(g):
        rows = slice(g * TM, (g + 1) * TM)
        sel = jnp.minimum(pl.program_id(1) * INPROJ_SUBTILES + g, 1)
        xm = _modulated(h_ref[0, rows, :], g_ref[0, 0], mod_ref[0, sel, 3:4, :],
                        mod_ref[0, sel, 4:5, :]).astype(bf16)
        z_scr[rows, :] = _dot(xm, wmain_ref[0])
        zt_scr[rows, :] = _dot(xm, wtail_ref[0])

    def heads(g):
        rows = slice(g * TM, (g + 1) * TM)
        _inproj_heads(rows, z_scr, zt_scr, naq_g_ref, nak_g_ref, cq_g_ref, ckv_g_ref, wq_ref, wkv_ref,
                      gq_ref, gqp_ref, gk_ref, gkp_ref, cos_ref, sin_ref,
                      qn_ref, kn_ref, vn_ref, qm_ref, km_ref, vm_ref, u_ref)

    project(0)
    for g in range(INPROJ_SUBTILES):
        if g + 1 < INPROJ_SUBTILES:
            project(g + 1)
        heads(g)


def _inproj_heads(rows, z_scr, zt_scr, naq_g_ref, nak_g_ref, cq_g_ref, ckv_g_ref, wq_ref, wkv_ref,
                  gq_ref, gqp_ref, gk_ref, gkp_ref, cos_ref, sin_ref,
                  qn_ref, kn_ref, vn_ref, qm_ref, km_ref, vm_ref, u_ref):
    z = z_scr[rows, :]
    zt = zt_scr[rows, :]

    qn = _pair_head_norm(z[:, ZC_QA:ZC_QA + NA_WIDTH], naq_g_ref[0])
    kn = _pair_head_norm(z[:, ZC_KA:ZC_KA + NA_WIDTH], nak_g_ref[0])
    for p in range(NA_WIDTH // LANES):
        qn_ref[0, p * LANES:(p + 1) * LANES, rows] = (qn[p] * NA_Q_SCALE).T.astype(bf16)
        kn_ref[0, rows, p * LANES:(p + 1) * LANES] = kn[p].astype(bf16)
        vn_ref[0, p * LANES:(p + 1) * LANES, rows] = z[:, ZC_VA + p * LANES:ZC_VA + (p + 1) * LANES].T.astype(bf16)
    u_ref[0, rows, :] = zt[:, 0:S5_WIDTH]

    cos_t = cos_ref[rows, :]
    sin_t = sin_ref[rows, :]

    cq = z[:, ZC_CQ:ZC_CQ + Q_LORA]
    ncq = ((cq * _rms(cq, Q_LORA)) * cq_g_ref[0]).astype(bf16)
    qq = _dot(ncq, wq_ref[0])
    gq = gq_ref[0]
    gqp = gqp_ref[0]
    for hd in range(MLA_HEADS):
        pre = qq[:, hd * HEAD_PAD:(hd + 1) * HEAD_PAD]
        perm = qq[:, (MLA_HEADS + hd) * HEAD_PAD:(MLA_HEADS + hd + 1) * HEAD_PAD]
        r = _rms(pre, MLA_QK)
        qm_ref[0, hd * HEAD_PAD:(hd + 1) * HEAD_PAD, rows] = (
            ((pre * gq) * cos_t + (perm * gqp) * sin_t) * (r * MLA_Q_SCALE)).T.astype(bf16)

    ckv = z[:, ZC_CKV:ZC_CKV + KV_LORA]
    nkv = ((ckv * _rms(ckv, KV_LORA)) * ckv_g_ref[0]).astype(bf16)
    kv = _dot(nkv, wkv_ref[0])
    for p in range(MLA_WIDTH // LANES):
        lo_col = MLA_HEADS * HEAD_PAD + p * LANES
        vm_ref[0, p * LANES:(p + 1) * LANES, rows] = kv[:, lo_col:lo_col + LANES].T.astype(bf16)
    krb = zt[:, S5_WIDTH:S5_WIDTH + LANES]
    lane = lax.broadcasted_iota(jnp.int32, (1, LANES), 1)
    rope_lanes = (lane >= MLA_NOPE) & (lane < MLA_QK)
    kr_a = jnp.where(rope_lanes, pltpu.roll(krb, MLA_NOPE, axis=1), 0.0)
    kr_b = jnp.where(rope_lanes, pltpu.roll(krb, MLA_ROPE, axis=1), 0.0)
    gk = gk_ref[0]
    gkp = gkp_ref[0]
    rot_part = (kr_b * gkp) * sin_t
    for hd in range(MLA_HEADS):
        kfull = kv[:, hd * HEAD_PAD:(hd + 1) * HEAD_PAD] + kr_a
        r = _rms(kfull, MLA_QK)
        km_ref[0, rows, hd * HEAD_PAD:(hd + 1) * HEAD_PAD] = (
            ((kfull * gk) * cos_t + rot_part) * r).astype(bf16)


def _inproj_call(h, mod, norm_g, l, params, cos_t, sin_t):
    qk_w = MLA_HEADS * HEAD_PAD
    rows = INPROJ_SUBTILES * TM

    def tok(width):
        return pl.BlockSpec((1, rows, width), lambda b, j: (b, j, 0))

    def tok_t(width):
        return pl.BlockSpec((1, width, rows), lambda b, j: (b, 0, j))

    tab_spec = pl.BlockSpec((rows, LANES), lambda b, j: (j, 0))
    mod_spec = pl.BlockSpec((1, 2, N_MOD, D_MODEL), lambda b, j: (b, 0, 0, 0))
    names = ["win_main", "win_tail", "naq_g", "nak_g", "cq_g", "ckv_g", "wq", "wkv", "gq", "gqp", "gk", "gkp"]
    return pl.pallas_call(
        _inproj_kernel,
        grid=(BATCH, N_TILES // INPROJ_SUBTILES),
        in_specs=([tok(D_MODEL), mod_spec, _param_spec(norm_g, l, 1)]
                  + [_param_spec(params[n], l) for n in names] + [tab_spec, tab_spec]),
        out_specs=[tok_t(NA_WIDTH), tok(NA_WIDTH), tok_t(NA_WIDTH),
                   tok_t(qk_w), tok(qk_w), tok_t(MLA_WIDTH), tok(S5_WIDTH)],
        out_shape=[
            jax.ShapeDtypeStruct((BATCH, NA_WIDTH, S_ALL), bf16),
            jax.ShapeDtypeStruct((BATCH, S_ALL, NA_WIDTH), bf16),
            jax.ShapeDtypeStruct((BATCH, NA_WIDTH, S_ALL), bf16),
            jax.ShapeDtypeStruct((BATCH, qk_w, S_ALL), bf16),
            jax.ShapeDtypeStruct((BATCH, S_ALL, qk_w), bf16),
            jax.ShapeDtypeStruct((BATCH, MLA_WIDTH, S_ALL), bf16),
            jax.ShapeDtypeStruct((BATCH, S_ALL, S5_WIDTH), f32),
        ],
        scratch_shapes=[pltpu.VMEM((rows, Z_MAIN), f32), pltpu.VMEM((rows, Z_TAIL), f32)],
        compiler_params=pltpu.CompilerParams(
            dimension_semantics=("arbitrary", "arbitrary"), vmem_limit_bytes=VMEM_LIMIT),
        name="in_proj_heads",
    )(h, mod, norm_g, *[params[n] for n in names], cos_t, sin_t)


def _pair_block_diag(qt):
    d = qt.shape[0] // 2
    z = jnp.zeros((d, qt.shape[1]), qt.dtype)
    return jnp.concatenate([jnp.concatenate([qt[:d], z], axis=0),
                            jnp.concatenate([z, qt[d:]], axis=0)], axis=1)


def _pair_scores(dst_ref, key_blocks, q_bd, bias_blocks):
    r = 0
    for kk, bias in zip(key_blocks, bias_blocks):
        s = _dot(kk, q_bd)
        dst_ref[r:r + kk.shape[0], :] = s if bias is None else s + bias
        r += kk.shape[0]


def _pair_softmax_pv(src_ref, vt_blocks):
    n_keys = sum(vt.shape[1] for vt in vt_blocks)
    m = jnp.max(src_ref[0:n_keys, :], axis=0, keepdims=True)
    den = acc = None
    r = 0
    for vt in vt_blocks:
        p = jnp.exp2(src_ref[r:r + vt.shape[1], :] - m)
        li = jnp.sum(p, axis=0, keepdims=True)
        oi = _dot(vt, p.astype(bf16))
        den = li if den is None else den + li
        acc = oi if acc is None else acc + oi
        r += vt.shape[1]
    o = acc / den
    dv = o.shape[0] // 2
    return jnp.concatenate([o[:dv, :TM], o[dv:, TM:]], axis=0).T.astype(bf16)


def _pipelined_tiles(scores, finish, s_a, s_b, first_tile, last_tile, lead_in=None):
    scores(first_tile, s_a)
    if lead_in is not None:
        lead_in(s_b)

    def body(i, carry):
        t0 = first_tile + 2 * i
        scores(t0 + 1, s_b)
        finish(t0, s_a)
        scores(t0 + 2, s_a)
        finish(t0 + 1, s_b)
        return carry

    lax.fori_loop(0, (last_tile - first_tile + 1) // 2 - 1, body, 0)
    scores(last_tile, s_b)
    finish(last_tile - 1, s_a)
    finish(last_tile, s_b)


NA_TILE_ROWS = TM // GRID_W
NA_UNION_ROWS = 12
NA_UNION_KEYS = NA_UNION_ROWS * GRID_W
NA_PATTERNS = 3
NA_BIAS_SHIFTS = 2 * WIN_ROWS


def _na_window_lo(pat, rr):
    return (-rr, -(WIN_ROWS // 2), -(WIN_ROWS // 2) - rr)[pat]


def _na_fill_bias(blk_ref, bias_scr):
    lane = lax.broadcasted_iota(jnp.int32, (1, LANES), 1)
    left = lane < GRID_W
    neg = jnp.full((GRID_W, LANES), NEG_INF, f32)
    for pat in range(NA_PATTERNS):
        for ii in range(NA_UNION_ROWS):
            for t in range(NA_TILE_ROWS // 2):
                d = ii - NA_TILE_ROWS * pat - 2 * t
                ok_l = 0 <= d - _na_window_lo(pat, 2 * t) < WIN_ROWS
                ok_r = 0 <= (d - 1) - _na_window_lo(pat, 2 * t + 1) < WIN_ROWS
                for hh in range(2):
                    if ok_l or ok_r:
                        blk = blk_ref[0, hh, d + WIN_ROWS - 1]
                        if not ok_l:
                            blk = jnp.where(left, NEG_INF, blk)
                        if not ok_r:
                            blk = jnp.where(left, blk, NEG_INF)
                    else:
                        blk = neg
                    c0 = hh * TM + t * LANES
                    bias_scr[pat, ii * GRID_W:(ii + 1) * GRID_W, c0:c0 + LANES] = blk


def _query_tile(qt_ref, t):
    return _pair_block_diag(qt_ref[0, :, pl.ds(pl.multiple_of(t * TM, TM), TM)])


def _out_rows(t, with_ctx):
    return pl.ds(pl.multiple_of((t - (0 if with_ctx else 1)) * TM, TM), TM)


def _na_kernel(qt_ref, k_ref, vt_ref, blk_ref, o_ref, bias_scr, s_a, s_b, *, with_ctx):
    @pl.when(pl.program_id(1) == 0)
    def _():
        _na_fill_bias(blk_ref, bias_scr)

    kc = k_ref[0, 0:CTX_LEN, :]
    vtc = vt_ref[0, :, 0:CTX_LEN]

    def window(t):
        r0 = (t - 1) * NA_TILE_ROWS
        base = jnp.clip(r0 - WIN_ROWS // 2, 0, GRID_ROWS - NA_UNION_ROWS)
        pat = (r0 - base) // NA_TILE_ROWS
        start = pl.multiple_of(CTX_LEN + base * GRID_W, NA_TILE_ROWS * GRID_W)
        return pat, pl.ds(start, NA_UNION_KEYS)

    def scores(t, dst):
        pat, keys = window(t)
        _pair_scores(dst, [k_ref[0, keys, :], kc], _query_tile(qt_ref, t), [bias_scr[pat], None])

    def finish(t, src):
        _, keys = window(t)
        o_ref[0, _out_rows(t, with_ctx), :] = _pair_softmax_pv(src, [vt_ref[0, :, keys], vtc])

    def ctx_finish(buf):
        o_ref[0, 0:TM, :] = _pair_softmax_pv(buf, [vtc])

    if with_ctx:
        _pair_scores(s_b, [kc], _query_tile(qt_ref, 0), [None])
    _pipelined_tiles(scores, finish, s_a, s_b, 1, N_TILES - 1, ctx_finish if with_ctx else None)


def _na_call(qnt, kn, vnt, bias_blocks, l, with_ctx):
    n_pairs = NA_WIDTH // LANES
    s_out = S_ALL if with_ctx else SEQ
    logits = pltpu.VMEM((NA_UNION_KEYS + CTX_LEN, 2 * TM), f32)
    return pl.pallas_call(
        functools.partial(_na_kernel, with_ctx=with_ctx),
        grid=(n_pairs, BATCH),
        in_specs=[
            pl.BlockSpec((1, LANES, S_ALL), lambda p, b: (b, p, 0)),
            pl.BlockSpec((1, S_ALL, LANES), lambda p, b: (b, 0, p)),
            pl.BlockSpec((1, LANES, S_ALL), lambda p, b: (b, p, 0)),
            pl.BlockSpec((1, 2, NA_BIAS_SHIFTS, GRID_W, LANES), lambda p, b: (l, p, 0, 0, 0)),
        ],
        out_specs=pl.BlockSpec((1, s_out, LANES), lambda p, b: (b, 0, p)),
        out_shape=jax.ShapeDtypeStruct((BATCH, s_out, NA_WIDTH), bf16),
        scratch_shapes=[pltpu.VMEM((NA_PATTERNS, NA_UNION_KEYS, 2 * TM), f32), logits, logits],
        compiler_params=pltpu.CompilerParams(
            dimension_semantics=("arbitrary", "arbitrary"), vmem_limit_bytes=VMEM_LIMIT),
        name="na_attention",
    )(qnt, kn, vnt, bias_blocks)


MLA_KEY_BLOCK = 768


def _mla_kernel(qt_ref, k_ref, vt_ref, o_ref, s_a, s_b, *, with_ctx):
    def scores(t, dst):
        _pair_scores(dst, [k_ref[0]], _query_tile(qt_ref, t), [None])

    def finish(t, src):
        vts = [vt_ref[0, :, s0:s0 + MLA_KEY_BLOCK] for s0 in range(0, S_ALL, MLA_KEY_BLOCK)]
        o_ref[0, _out_rows(t, with_ctx), :] = _pair_softmax_pv(src, vts)

    def ctx_finish(buf):
        o_ref[0, 0:TM, :] = _pair_softmax_pv(buf, [vt_ref[0, :, 0:CTX_LEN]])

    if with_ctx:
        _pair_scores(s_b, [k_ref[0, 0:CTX_LEN, :]], _query_tile(qt_ref, 0), [None])
    _pipelined_tiles(scores, finish, s_a, s_b, 1, N_TILES - 1, ctx_finish if with_ctx else None)


def _mla_call(qmt, km, vmt, with_ctx):
    n_pairs = MLA_HEADS // 2
    s_out = S_ALL if with_ctx else SEQ
    logits = pltpu.VMEM((S_ALL, 2 * TM), f32)
    return pl.pallas_call(
        functools.partial(_mla_kernel, with_ctx=with_ctx),
        grid=(BATCH, n_pairs),
        in_specs=[
            pl.BlockSpec((1, 2 * HEAD_PAD, S_ALL), lambda b, p: (b, p, 0)),
            pl.BlockSpec((1, S_ALL, 2 * HEAD_PAD), lambda b, p: (b, 0, p)),
            pl.BlockSpec((1, LANES, S_ALL), lambda b, p: (b, p, 0)),
        ],
        out_specs=pl.BlockSpec((1, s_out, LANES), lambda b, p: (b, 0, p)),
        out_shape=jax.ShapeDtypeStruct((BATCH, s_out, MLA_WIDTH), bf16),
        scratch_shapes=[logits, logits],
        compiler_params=pltpu.CompilerParams(
            dimension_semantics=("arbitrary", "arbitrary"), vmem_limit_bytes=VMEM_LIMIT),
        name="mla_attention",
    )(qmt, km, vmt)


S5_BLOCK_T = 16
S5_BLOCK_ROWS = S5_BLOCK_T * BATCH


def _s5_scan_kernel(*refs, reverse, finish):
    u_ref, bmat_ref, cmat_ref, lam_ref = refs[:4]
    if finish:
        y_other_ref, d_ref, wglu_ref, bglu_ref = refs[4:8]
    y_ref, bu_ref, h_ref, st_ref = refs[-4:]

    @pl.when(pl.program_id(0) == 0)
    def _():
        st_ref[...] = jnp.zeros_like(st_ref)

    def rows(k):
        return slice(k * S5_BLOCK_ROWS, (k + 1) * S5_BLOCK_ROWS)

    def project(k):
        bu_ref[rows(k), :] = _dot(u_ref[rows(k), :].astype(bf16), bmat_ref[0, 0])

    def readout(k):
        y = _dot(h_ref[rows(k), :].astype(bf16), cmat_ref[0, 0])
        if not finish:
            y_ref[rows(k), :] = y
            return
        y = (y_other_ref[rows(k), :] + y) + d_ref[0] * u_ref[rows(k), :]
        c0 = math.sqrt(2.0 / math.pi)
        gl = 0.5 * y * (1.0 + jnp.tanh(c0 * (y + 0.044715 * (y * y * y))))
        o = _dot(gl.astype(bf16), wglu_ref[0]) + bglu_ref[0]
        y_ref[rows(k), :] = (o[:, :S5_WIDTH] * _sigmoid(o[:, S5_WIDTH:])).astype(bf16)

    n_blocks = S5_T // S5_BLOCK_T
    order = list(range(n_blocks))[::-1] if reverse else list(range(n_blocks))
    steps = list(range(S5_BLOCK_T))[::-1] if reverse else list(range(S5_BLOCK_T))
    hr = st_ref[:, 0:S5_LANES]
    hi = st_ref[:, S5_LANES:]
    project(order[0])
    for n, k in enumerate(order):
        if n + 1 < n_blocks:
            project(order[n + 1])
        for t in steps:
            r0 = k * S5_BLOCK_ROWS + t * BATCH
            lr = lam_ref[0, 0, :, 0:S5_LANES]
            li = lam_ref[0, 0, :, S5_LANES:]
            nr = (lr * hr - li * hi) + bu_ref[r0:r0 + BATCH, 0:S5_LANES]
            ni = (lr * hi + li * hr) + bu_ref[r0:r0 + BATCH, S5_LANES:]
            h_ref[r0:r0 + BATCH, 0:S5_LANES] = nr
            h_ref[r0:r0 + BATCH, S5_LANES:] = ni
            hr, hi = nr, ni
        if n >= 1:
            readout(order[n - 1])
    readout(order[-1])
    st_ref[:, 0:S5_LANES] = hr
    st_ref[:, S5_LANES:] = hi


def _s5_scan_call(u_t, p, l, d, y_other=None):
    reverse = d == 1
    finish = y_other is not None
    if reverse:
        def chunk(i):
            return jnp.where(i < S5_CTX_CHUNKS, S5_CTX_CHUNKS - 1 - i, S5_CHUNKS + S5_CTX_CHUNKS - 1 - i)
    else:
        def chunk(i):
            return i
    row_spec = pl.BlockSpec((S5_ROWS, S5_WIDTH), lambda i: (chunk(i), 0))
    ins = [u_t, p["bmat"], p["cmat"], p["lam"]]
    in_specs = [row_spec, _param_spec(p["bmat"], l, d), _param_spec(p["cmat"], l, d), _param_spec(p["lam"], l, d)]
    if finish:
        ins += [y_other, p["s5_d"], p["w_glu"], p["b_glu"]]
        in_specs += [row_spec, _param_spec(p["s5_d"], l), _param_spec(p["w_glu"], l), _param_spec(p["b_glu"], l)]
    return pl.pallas_call(
        functools.partial(_s5_scan_kernel, reverse=reverse, finish=finish),
        grid=(S5_CHUNKS,),
        in_specs=in_specs,
        out_specs=row_spec,
        out_shape=jax.ShapeDtypeStruct((S_ALL * BATCH, S5_WIDTH), bf16 if finish else f32),
        scratch_shapes=[pltpu.VMEM((S5_ROWS, 2 * S5_LANES), f32), pltpu.VMEM((S5_ROWS, 2 * S5_LANES), f32),
                        pltpu.VMEM((BATCH, 2 * S5_LANES), f32)],
        compiler_params=pltpu.CompilerParams(
            dimension_semantics=("arbitrary",), vmem_limit_bytes=VMEM_LIMIT),
        name="s5_scan_bwd" if reverse else "s5_scan_fwd",
    )(*ins)


def _rope_perm():
    half = MLA_ROPE // 2
    quarter = half // 2
    idx, sign = [], []
    for j in range(MLA_ROPE):
        if (j % half) < quarter:
            idx.append(j + quarter)
            sign.append(-1.0)
        else:
            idx.append(j - quarter)
            sign.append(1.0)
    return jnp.array(idx, jnp.int32), jnp.array(sign, f32)


def _rope_tables():
    quarter = MLA_ROPE // 4
    t = jnp.arange(SEQ)
    row = (t // GRID_W).astype(f32)
    col = (t % GRID_W).astype(f32)
    inv = ROPE_THETA ** (-jnp.arange(quarter, dtype=f32) / quarter)
    ang = jnp.concatenate([row[:, None] * inv] * 2 + [col[:, None] * inv] * 2, axis=-1)
    ang = jnp.concatenate([jnp.zeros((CTX_LEN, MLA_ROPE), f32), ang], axis=0)
    ones = jnp.ones((S_ALL, MLA_NOPE), f32)
    pad = jnp.zeros((S_ALL, HEAD_PAD - MLA_QK), f32)
    cos_t = jnp.concatenate([ones, jnp.cos(ang), pad], axis=-1)
    sin_t = jnp.concatenate([0.0 * ones, jnp.sin(ang), pad], axis=-1)
    return cos_t, sin_t


def _na_bias_blocks(rpb):
    cq = jnp.arange(GRID_W)[None, :]
    kc = jnp.arange(GRID_W)[:, None]
    d_col = jnp.clip(kc - cq, -(WIN_COLS - 1), WIN_COLS - 1) + WIN_COLS - 1
    col_start = jnp.clip(cq - WIN_COLS // 2, 0, GRID_W - WIN_COLS)
    in_win = (kc >= col_start) & (kc < col_start + WIN_COLS)
    onehot = (d_col[:, :, None] == jnp.arange(2 * WIN_COLS - 1)[None, None, :]).astype(f32)
    blocks = jnp.einsum('lhdc,kqc->lhdkq', rpb.astype(f32), onehot, precision=lax.Precision.HIGHEST)
    blocks = jnp.where(in_win, blocks * LOG2E, NEG_INF)
    neg = jnp.full(blocks.shape[:2] + (1, GRID_W, GRID_W), NEG_INF, f32)
    padded = jnp.concatenate([neg, blocks, neg], axis=2)
    return jnp.concatenate([padded[:, :, 1:], padded[:, :, :-1]], axis=-1)


def _pad_heads(w, width):
    lead = w.shape[:-1]
    w = w.reshape(lead + (MLA_HEADS, width))
    w = jnp.pad(w, [(0, 0)] * len(lead) + [(0, 0), (0, HEAD_PAD - width)])
    return w.reshape(lead + (MLA_HEADS * HEAD_PAD,))


def _block_diag_groups(w):
    lead = w.shape[:-3]
    a, b = w.shape[-2:]
    shape = (S5_GROUPS * a, S5_GROUPS * b)
    repeat = jnp.tile(jnp.eye(b, dtype=f32), (1, S5_GROUPS))
    tiled = jnp.einsum('...rb,bc->...rc', w.reshape(lead + (S5_GROUPS * a, b)).astype(f32), repeat,
                       precision=lax.Precision.HIGHEST)
    same_group = (lax.broadcasted_iota(jnp.int32, shape, 0) // a
                  == lax.broadcasted_iota(jnp.int32, shape, 1) // b)
    return jnp.where(same_group, tiled, 0.0)


def _prepare_params(ffn_w_gu, ffn_w_down, w_in, w_out, na_qk_g, na_rpb, mla_cq_g, mla_ckv_g, mla_w_uq,
                    mla_w_ukv, mla_qk_g, s5_lam_re, s5_lam_im, s5_log_dt, s5_b_re, s5_b_im, s5_c_re,
                    s5_c_im, s5_d, s5_w_glu, s5_b_glu):
    p = {}
    perm_idx, perm_sign = _rope_perm()
    p["wgu"] = ffn_w_gu.astype(bf16)
    p["wd"] = ffn_w_down.astype(bf16)
    p["w_out"] = w_out.astype(bf16)
    o_kr = Z_MAIN
    o_u = o_kr + MLA_ROPE
    w_kr = w_in[:, :, o_kr:o_kr + MLA_ROPE]
    p["win_main"] = w_in[:, :, :Z_MAIN].astype(bf16)
    p["win_tail"] = jnp.concatenate([
        w_in[:, :, o_u:o_u + S5_WIDTH], w_kr, w_kr[:, :, perm_idx] * perm_sign,
        jnp.zeros((DEPTH, D_MODEL, LANES - 2 * MLA_ROPE), f32)], axis=-1).astype(bf16)
    p["naq_g"] = jnp.tile(na_qk_g[:, 0], (1, NA_HEADS))[:, None, :]
    p["nak_g"] = jnp.tile(na_qk_g[:, 1], (1, NA_HEADS))[:, None, :]
    p["cq_g"] = mla_cq_g[:, None, :]
    p["ckv_g"] = mla_ckv_g[:, None, :]
    wuq = mla_w_uq.reshape(DEPTH, Q_LORA, MLA_HEADS, MLA_QK)
    partner = jnp.concatenate([jnp.zeros((DEPTH, Q_LORA, MLA_HEADS, MLA_NOPE), f32),
                               wuq[..., MLA_NOPE:][..., perm_idx] * perm_sign], axis=-1)
    p["wq"] = jnp.concatenate([_pad_heads(mla_w_uq, MLA_QK),
                               _pad_heads(partner.reshape(DEPTH, Q_LORA, -1), MLA_QK)], axis=-1).astype(bf16)
    wukv = mla_w_ukv.reshape(DEPTH, KV_LORA, MLA_HEADS, MLA_NOPE + MLA_V_DIM)
    p["wkv"] = jnp.concatenate([_pad_heads(wukv[..., :MLA_NOPE].reshape(DEPTH, KV_LORA, -1), MLA_NOPE),
                                wukv[..., MLA_NOPE:].reshape(DEPTH, KV_LORA, -1)], axis=-1).astype(bf16)

    def pad_gain(g):
        z = jnp.zeros((DEPTH, HEAD_PAD - MLA_QK), f32)
        full = jnp.concatenate([g, z], axis=-1)[:, None, :]
        part = jnp.concatenate([jnp.zeros((DEPTH, MLA_NOPE), f32), g[:, MLA_NOPE:][:, perm_idx], z],
                               axis=-1)[:, None, :]
        return full, part

    p["gq"], p["gqp"] = pad_gain(mla_qk_g[:, 0])
    p["gk"], p["gkp"] = pad_gain(mla_qk_g[:, 1])
    p["na_bias"] = _na_bias_blocks(na_rpb)
    lre = s5_lam_re.astype(f32)
    lim = s5_lam_im.astype(f32)
    dt = jnp.exp(s5_log_dt.astype(f32))[..., None]
    mag = jnp.exp(lre * dt)
    bar_re = mag * jnp.cos(lim * dt)
    bar_im = mag * jnp.sin(lim * dt)
    den = lre * lre + lim * lim
    q_re = ((bar_re - 1.0) * lre + bar_im * lim) / den
    q_im = (bar_im * lre - (bar_re - 1.0) * lim) / den
    bre = s5_b_re.astype(f32)
    bim = s5_b_im.astype(f32)
    bbar_re = q_re[..., None] * bre - q_im[..., None] * bim
    bbar_im = q_re[..., None] * bim + q_im[..., None] * bre
    p["bmat"] = jnp.concatenate([_block_diag_groups(jnp.swapaxes(bbar_re, -1, -2)),
                                 _block_diag_groups(jnp.swapaxes(bbar_im, -1, -2))], axis=-1).astype(bf16)
    p["cmat"] = jnp.concatenate([_block_diag_groups(jnp.swapaxes(s5_c_re.astype(f32), -1, -2)),
                                 -_block_diag_groups(jnp.swapaxes(s5_c_im.astype(f32), -1, -2))],
                                axis=-2).astype(bf16)
    lam_row = jnp.concatenate([bar_re.reshape(DEPTH, 2, S5_LANES), bar_im.reshape(DEPTH, 2, S5_LANES)], axis=-1)
    p["lam"] = jnp.broadcast_to(lam_row[:, :, None, :], (DEPTH, 2, BATCH, 2 * S5_LANES))
    p["s5_d"] = s5_d[:, None, :]
    p["w_glu"] = s5_w_glu.astype(bf16)
    p["b_glu"] = s5_b_glu[:, None, :]
    return p


def kernel(x, c, ctx, c_ctx, w_mod, b_mod, norm_g, ffn_w_gu, ffn_w_down, w_in, w_out, na_qk_g, na_rpb, mla_cq_g, mla_ckv_g, mla_w_uq, mla_w_ukv, mla_qk_g, s5_lam_re, s5_lam_im, s5_log_dt, s5_b_re, s5_b_im, s5_c_re, s5_c_im, s5_d, s5_w_glu, s5_b_glu):
    assert x.shape == (BATCH, SEQ, D_MODEL) and ctx.shape == (BATCH, CTX_LEN, D_MODEL)
    mod_rows = 16
    cvec = jnp.concatenate([c, c_ctx[None, :], jnp.zeros((mod_rows - BATCH - 1, D_MODEL), f32)], axis=0)
    mod_all = _mod_call(cvec, w_mod, b_mod).reshape(DEPTH, mod_rows, N_MOD, D_MODEL)
    cos_t, sin_t = _rope_tables()
    p = _prepare_params(ffn_w_gu, ffn_w_down, w_in, w_out, na_qk_g, na_rpb, mla_cq_g, mla_ckv_g,
                        mla_w_uq, mla_w_ukv, mla_qk_g, s5_lam_re, s5_lam_im, s5_log_dt, s5_b_re,
                        s5_b_im, s5_c_re, s5_c_im, s5_d, s5_w_glu, s5_b_glu)
    norm_g4 = norm_g[:, :, None, :]
    h = (ctx, x)
    for l in range(DEPTH):
        mod_c = jnp.broadcast_to(mod_all[l, BATCH][None], (BATCH, N_MOD, D_MODEL))
        mod = jnp.stack([mod_c, mod_all[l, :BATCH]], axis=1)
        need_ctx = l < DEPTH - 1
        h = _ffn_call(h, mod, norm_g4, p["wgu"], p["wd"], l, 0)
        qn, kn, vn, qm, km, vm, u = _inproj_call(h, mod, norm_g4, l, p, cos_t, sin_t)
        a = _na_call(qn, kn, vn, p["na_bias"], l, need_ctx)
        bm = _mla_call(qm, km, vm, need_ctx)
        u_t = u.transpose(1, 0, 2).reshape(S_ALL * BATCH, S5_WIDTH)
        yf = _s5_scan_call(u_t, p, l, 0)
        s_t = _s5_scan_call(u_t, p, l, 1, y_other=yf)
        if not need_ctx:
            s_t = s_t[CTX_LEN * BATCH:]
        s = s_t.reshape(-1, BATCH, S5_WIDTH).transpose(1, 0, 2)
        h = _ffn_call(h, mod, norm_g4, p["wgu"], p["wd"], l, 1, mixers=(a, bm, s), w_out=p["w_out"],
                      out_ctx=need_ctx)
    return h
```

```python
import functools
import math

import jax
import jax.numpy as jnp
from jax import lax
from jax.experimental import pallas as pl
from jax.experimental.pallas import tpu as pltpu

D_MODEL = 1024
BATCH = 8
SEQ = 2048
DEPTH = 2
CTX_LEN = 256
S_ALL = CTX_LEN + SEQ
GRID_W = 64
GRID_ROWS = SEQ // GRID_W
NA_HEAD_DIM = 64
NA_WIDTH = 384
NA_HEADS = 6
WIN_ROWS = 8
WIN_COLS = 16
MLA_V_DIM = 64
MLA_WIDTH = 384
MLA_HEADS = 6
MLA_NOPE = 64
MLA_ROPE = 32
MLA_QK = 96
Q_LORA = 384
KV_LORA = 256
S5_WIDTH = 256
S5_GROUP = 16
S5_GROUPS = 16
S5_STATE = 64
S5_LANES = S5_GROUPS * S5_STATE
D_FF = 2816
ROPE_THETA = 10000.0
EPS = 1e-6
N_MOD = 9
NEG_INF = -1e30

LANES = 128
TM = 256
N_TILES = S_ALL // TM
MXU_DIM = 256
FF_CHUNKS = ((0, 6 * MXU_DIM), (6 * MXU_DIM, D_FF))
HEAD_PAD = 128
S5_T = 256
S5_ROWS = S5_T * BATCH
S5_CHUNKS = S_ALL // S5_T
S5_CTX_CHUNKS = CTX_LEN // S5_T
VMEM_LIMIT = 56 * 1024 * 1024

ZC_QA, ZC_KA, ZC_VA, ZC_CQ, ZC_CKV = 0, 384, 768, 1152, 1536
Z_MAIN = 1792
Z_TAIL = S5_WIDTH + LANES

LOG2E = math.log2(math.e)
NA_Q_SCALE = NA_HEAD_DIM ** -0.5 * LOG2E
MLA_Q_SCALE = MLA_QK ** -0.5 * LOG2E

f32 = jnp.float32
bf16 = jnp.bfloat16


def _dot(a, b):
    return jnp.dot(a, b, preferred_element_type=f32)


def _sigmoid(x):
    return 1.0 / (1.0 + jnp.exp(-x))


def _rms(x, n):
    return lax.rsqrt(jnp.sum(x * x, axis=-1, keepdims=True) / n + EPS)


def _modulated(x, g, shift, scale):
    y = x * _rms(x, D_MODEL)
    return (y * g) * (1.0 + scale) + shift


def _mod_kernel(c_ref, w_ref, b_ref, o_ref):
    c = c_ref[...]
    a = (c * _sigmoid(c)).astype(bf16)
    o_ref[0] = _dot(a, w_ref[0].astype(bf16)) + b_ref[0]


def _mod_call(cvec, w_mod, b_mod):
    rows = cvec.shape[0]
    return pl.pallas_call(
        _mod_kernel,
        grid=(DEPTH, N_MOD),
        in_specs=[
            pl.BlockSpec((rows, D_MODEL), lambda l, j: (0, 0)),
            pl.BlockSpec((1, D_MODEL, D_MODEL), lambda l, j: (l, 0, j)),
            pl.BlockSpec((1, 1, D_MODEL), lambda l, j: (l, 0, j)),
        ],
        out_specs=pl.BlockSpec((1, rows, D_MODEL), lambda l, j: (l, 0, j)),
        out_shape=jax.ShapeDtypeStruct((DEPTH, rows, N_MOD * D_MODEL), f32),
        compiler_params=pltpu.CompilerParams(
            dimension_semantics=("arbitrary", "arbitrary"), vmem_limit_bytes=VMEM_LIMIT),
        name="adaln_mod",
    )(cvec, w_mod, b_mod.reshape(DEPTH, 1, N_MOD * D_MODEL))


def _param_spec(arr, *lead, single=False):
    tail = arr.shape[len(lead):]
    idx = tuple(lead) + (0,) * len(tail)
    mode = pl.Buffered(1) if single else None
    return pl.BlockSpec((1,) * len(lead) + tail, lambda *_: idx, pipeline_mode=mode)


FFN_SUBTILES = 4


def _ffn_kernel(*refs, i0, fuse_outproj, split_input, tiles_per_batch, first):
    n_sub = FFN_SUBTILES
    refs = list(refs)
    n_src = 2 if split_input else 1
    srcs = [refs[g * n_src:(g + 1) * n_src] for g in range(n_sub)]
    refs = refs[n_sub * n_src:]
    mods = refs[:n_sub]
    g_ref, wgu_ref, wd_ref = refs[n_sub:n_sub + 3]
    refs = refs[n_sub + 3:]
    if fuse_outproj:
        mixers = [refs[3 * g:3 * g + 3] for g in range(n_sub)]
        wo_ref = refs[3 * n_sub]
        refs = refs[3 * n_sub + 1:]
    o_ref, x_scr, xm_scr = refs

    def mod_set(g):
        j = (pl.program_id(0) * n_sub + g) % tiles_per_batch + first
        return j, jnp.minimum(j, 1)

    def prologue(g):
        j, sel = mod_set(g)
        rows = slice(g * TM, (g + 1) * TM)
        mod_ref = mods[g]
        if split_input:
            x = jnp.where(j == 0, srcs[g][0][0], srcs[g][1][0])
        else:
            x = srcs[g][0][0]
        if fuse_outproj:
            a_ref, b_ref, s_ref = mixers[g]
            mixed = jnp.concatenate([a_ref[0], b_ref[0], s_ref[0]], axis=-1)
            x = x + mod_ref[0, sel, 5:6, :] * _dot(mixed, wo_ref[0])
        x_scr[rows, :] = x
        shift = mod_ref[0, sel, i0:i0 + 1, :]
        scale = mod_ref[0, sel, i0 + 1:i0 + 2, :]
        xm_scr[rows, :] = _modulated(x, g_ref[0, 0], shift, scale).astype(bf16)

    def chunk(g, c, acc):
        rows = slice(g * TM, (g + 1) * TM)
        lo, hi = FF_CHUNKS[c]
        gt = _dot(xm_scr[rows, :], wgu_ref[0, 0, :, lo:hi])
        up = _dot(xm_scr[rows, :], wgu_ref[0, 0, :, D_FF + lo:D_FF + hi])
        a = ((gt * _sigmoid(gt)) * up).astype(bf16)
        part = _dot(a, wd_ref[0, 0, lo:hi, :])
        return part if acc is None else acc + part

    def epilogue(g, acc):
        _, sel = mod_set(g)
        gate = mods[g][0, sel, i0 + 2:i0 + 3, :]
        o_ref[g] = x_scr[g * TM:(g + 1) * TM, :] + (0.5 * gate) * acc

    assert len(FF_CHUNKS) == 2
    prologue(0)
    acc_prev = chunk(0, 0, None)
    for g in range(1, n_sub):
        prologue(g)
        acc_prev = chunk(g - 1, 1, acc_prev)
        acc_cur = chunk(g, 0, None)
        epilogue(g - 1, acc_prev)
        acc_prev = acc_cur
    acc_prev = chunk(n_sub - 1, 1, acc_prev)
    epilogue(n_sub - 1, acc_prev)


def _ffn_call(h, mod, norm_g, wgu, wd, l, k, *, mixers=None, w_out=None, out_ctx=True):
    split = isinstance(h, tuple)
    first = 0 if out_ctx else 1
    tpb = N_TILES - first
    n_sub = FFN_SUBTILES

    def tile_of(i, g):
        t = i * n_sub + g
        return t // tpb, t % tpb

    ins, in_specs = [], []
    for g in range(n_sub):
        if split:
            ins += list(h)
            in_specs += [
                pl.BlockSpec((1, TM, D_MODEL), lambda i, g=g: (tile_of(i, g)[0], 0, 0)),
                pl.BlockSpec((1, TM, D_MODEL),
                             lambda i, g=g: (tile_of(i, g)[0], jnp.maximum(tile_of(i, g)[1] + first - 1, 0), 0))]
        else:
            ins.append(h)
            in_specs.append(pl.BlockSpec(
                (1, TM, D_MODEL), lambda i, g=g: (tile_of(i, g)[0], tile_of(i, g)[1] + first, 0)))
    for g in range(n_sub):
        ins.append(mod)
        in_specs.append(pl.BlockSpec((1, 2, N_MOD, D_MODEL), lambda i, g=g: (tile_of(i, g)[0], 0, 0, 0)))
    ins += [norm_g, wgu, wd]
    in_specs += [_param_spec(norm_g, l, 2 * k), _param_spec(wgu, l, k, single=True),
                 _param_spec(wd, l, k, single=True)]
    if mixers is not None:
        for g in range(n_sub):
            for arr in mixers:
                ins.append(arr)
                in_specs.append(pl.BlockSpec((1, TM, arr.shape[-1]), lambda i, g=g: tile_of(i, g) + (0,)))
        ins.append(w_out)
        in_specs.append(_param_spec(w_out, l, single=True))
    rows = n_sub * TM
    out = pl.pallas_call(
        functools.partial(_ffn_kernel, i0=6 * k, fuse_outproj=mixers is not None, split_input=split,
                          tiles_per_batch=tpb, first=first),
        grid=(BATCH * tpb // n_sub,),
        in_specs=in_specs,
        out_specs=pl.BlockSpec((n_sub, TM, D_MODEL), lambda i: (i, 0, 0)),
        out_shape=jax.ShapeDtypeStruct((BATCH * tpb, TM, D_MODEL), f32),
        scratch_shapes=[pltpu.VMEM((rows, D_MODEL), f32), pltpu.VMEM((rows, D_MODEL), bf16)],
        compiler_params=pltpu.CompilerParams(
            dimension_semantics=("arbitrary",), vmem_limit_bytes=VMEM_LIMIT),
        name="swiglu_half_step",
    )(*ins)
    return out.reshape(BATCH, tpb * TM, D_MODEL)


INPROJ_SUBTILES = 3


def _pair_head_norm(x, g):
    lane = lax.broadcasted_iota(jnp.int32, (1, LANES), 1)
    lo = lane < NA_HEAD_DIM
    outs = []
    for p in range(NA_WIDTH // LANES):
        xb = x[:, p * LANES:(p + 1) * LANES]
        sq = xb * xb
        s_lo = jnp.sum(jnp.where(lo, sq, 0.0), axis=-1, keepdims=True)
        s_hi = jnp.sum(jnp.where(lo, 0.0, sq), axis=-1, keepdims=True)
        r = jnp.where(lo, lax.rsqrt(s_lo / NA_HEAD_DIM + EPS), lax.rsqrt(s_hi / NA_HEAD_DIM + EPS))
        outs.append((xb * r) * g[:, p * LANES:(p + 1) * LANES])
    return outs


def _inproj_kernel(h_ref, mod_ref, g_ref, wmain_ref, wtail_ref, naq_g_ref, nak_g_ref, cq_g_ref, ckv_g_ref,
                   wq_ref, wkv_ref, gq_ref, gqp_ref, gk_ref, gkp_ref, cos_ref, sin_ref,
                   qn_ref, kn_ref, vn_ref, qm_ref, km_ref, vm_ref, u_ref, z_scr, zt_scr):
    def project(g):
        rows = slice(g * TM, (g + 1) * TM)
        sel = jnp.minimum(pl.program_id(1) * INPROJ_SUBTILES + g, 1)
        xm = _modulated(h_ref[0, rows, :], g_ref[0, 0], mod_ref[0, sel, 3:4, :],
                        mod_ref[0, sel, 4:5, :]).astype(bf16)
        z_scr[rows, :] = _dot(xm, wmain_ref[0])
        zt_scr[rows, :] = _dot(xm, wtail_ref[0])

    def heads(g):
        rows = slice(g * TM, (g + 1) * TM)
        _inproj_heads(rows, z_scr, zt_scr, naq_g_ref, nak_g_ref, cq_g_ref, ckv_g_ref, wq_ref, wkv_ref,
                      gq_ref, gqp_ref, gk_ref, gkp_ref, cos_ref, sin_ref,
                      qn_ref, kn_ref, vn_ref, qm_ref, km_ref, vm_ref, u_ref)

    project(0)
    for g in range(INPROJ_SUBTILES):
        if g + 1 < INPROJ_SUBTILES:
            project(g + 1)
        heads(g)


def _inproj_heads(rows, z_scr, zt_scr, naq_g_ref, nak_g_ref, cq_g_ref, ckv_g_ref, wq_ref, wkv_ref,
                  gq_ref, gqp_ref, gk_ref, gkp_ref, cos_ref, sin_ref,
                  qn_ref, kn_ref, vn_ref, qm_ref, km_ref, vm_ref, u_ref):
    z = z_scr[rows, :]
    zt = zt_scr[rows, :]

    qn = _pair_head_norm(z[:, ZC_QA:ZC_QA + NA_WIDTH], naq_g_ref[0])
    kn = _pair_head_norm(z[:, ZC_KA:ZC_KA + NA_WIDTH], nak_g_ref[0])
    for p in range(NA_WIDTH // LANES):
        qn_ref[0, p * LANES:(p + 1) * LANES, rows] = (qn[p] * NA_Q_SCALE).T.astype(bf16)
        kn_ref[0, rows, p * LANES:(p + 1) * LANES] = kn[p].astype(bf16)
        vn_ref[0, p * LANES:(p + 1) * LANES, rows] = z[:, ZC_VA + p * LANES:ZC_VA + (p + 1) * LANES].T.astype(bf16)
    u_ref[0, rows, :] = zt[:, 0:S5_WIDTH]

    cos_t = cos_ref[rows, :]
    sin_t = sin_ref[rows, :]

    cq = z[:, ZC_CQ:ZC_CQ + Q_LORA]
    ncq = ((cq * _rms(cq, Q_LORA)) * cq_g_ref[0]).astype(bf16)
    qq = _dot(ncq, wq_ref[0])
    gq = gq_ref[0]
    gqp = gqp_ref[0]
    for hd in range(MLA_HEADS):
        pre = qq[:, hd * HEAD_PAD:(hd + 1) * HEAD_PAD]
        perm = qq[:, (MLA_HEADS + hd) * HEAD_PAD:(MLA_HEADS + hd + 1) * HEAD_PAD]
        r = _rms(pre, MLA_QK)
        qm_ref[0, hd * HEAD_PAD:(hd + 1) * HEAD_PAD, rows] = (
            ((pre * gq) * cos_t + (perm * gqp) * sin_t) * (r * MLA_Q_SCALE)).T.astype(bf16)

    ckv = z[:, ZC_CKV:ZC_CKV + KV_LORA]
    nkv = ((ckv * _rms(ckv, KV_LORA)) * ckv_g_ref[0]).astype(bf16)
    kv = _dot(nkv, wkv_ref[0])
    for p in range(MLA_WIDTH // LANES):
        lo_col = MLA_HEADS * HEAD_PAD + p * LANES
        vm_ref[0, p * LANES:(p + 1) * LANES, rows] = kv[:, lo_col:lo_col + LANES].T.astype(bf16)
    krb = zt[:, S5_WIDTH:S5_WIDTH + LANES]
    lane = lax.broadcasted_iota(jnp.int32, (1, LANES), 1)
    rope_lanes = (lane >= MLA_NOPE) & (lane < MLA_QK)
    kr_a = jnp.where(rope_lanes, pltpu.roll(krb, MLA_NOPE, axis=1), 0.0)
    kr_b = jnp.where(rope_lanes, pltpu.roll(krb, MLA_ROPE, axis=1), 0.0)
    gk = gk_ref[0]
    gkp = gkp_ref[0]
    rot_part = (kr_b * gkp) * sin_t
    for hd in range(MLA_HEADS):
        kfull = kv[:, hd * HEAD_PAD:(hd + 1) * HEAD_PAD] + kr_a
        r = _rms(kfull, MLA_QK)
        km_ref[0, rows, hd * HEAD_PAD:(hd + 1) * HEAD_PAD] = (
            ((kfull * gk) * cos_t + rot_part) * r).astype(bf16)


def _inproj_call(h, mod, norm_g, l, params, cos_t, sin_t):
    qk_w = MLA_HEADS * HEAD_PAD
    rows = INPROJ_SUBTILES * TM

    def tok(width):
        return pl.BlockSpec((1, rows, width), lambda b, j: (b, j, 0))

    def tok_t(width):
        return pl.BlockSpec((1, width, rows), lambda b, j: (b, 0, j))

    tab_spec = pl.BlockSpec((rows, LANES), lambda b, j: (j, 0))
    mod_spec = pl.BlockSpec((1, 2, N_MOD, D_MODEL), lambda b, j: (b, 0, 0, 0))
    main_spec = pl.BlockSpec((1, D_MODEL, Z_MAIN), lambda b, j: (l, 0, 0))
    names = ["win", "win_tail", "naq_g", "nak_g", "cq_g", "ckv_g", "wq", "wkv", "gq", "gqp", "gk", "gkp"]
    return pl.pallas_call(
        _inproj_kernel,
        grid=(BATCH, N_TILES // INPROJ_SUBTILES),
        in_specs=([tok(D_MODEL), mod_spec, _param_spec(norm_g, l, 1), main_spec]
                  + [_param_spec(params[n], l) for n in names[1:]] + [tab_spec, tab_spec]),
        out_specs=[tok_t(NA_WIDTH), tok(NA_WIDTH), tok_t(NA_WIDTH),
                   tok_t(qk_w), tok(qk_w), tok_t(MLA_WIDTH), tok(S5_WIDTH)],
        out_shape=[
            jax.ShapeDtypeStruct((BATCH, NA_WIDTH, S_ALL), bf16),
            jax.ShapeDtypeStruct((BATCH, S_ALL, NA_WIDTH), bf16),
            jax.ShapeDtypeStruct((BATCH, NA_WIDTH, S_ALL), bf16),
            jax.ShapeDtypeStruct((BATCH, qk_w, S_ALL), bf16),
            jax.ShapeDtypeStruct((BATCH, S_ALL, qk_w), bf16),
            jax.ShapeDtypeStruct((BATCH, MLA_WIDTH, S_ALL), bf16),
            jax.ShapeDtypeStruct((BATCH, S_ALL, S5_WIDTH), f32),
        ],
        scratch_shapes=[pltpu.VMEM((rows, Z_MAIN), f32), pltpu.VMEM((rows, Z_TAIL), f32)],
        compiler_params=pltpu.CompilerParams(
            dimension_semantics=("arbitrary", "arbitrary"), vmem_limit_bytes=VMEM_LIMIT),
        name="in_proj_heads",
    )(h, mod, norm_g, *[params[n] for n in names], cos_t, sin_t)


def _pair_block_diag(qt):
    d = qt.shape[0] // 2
    z = jnp.zeros((d, qt.shape[1]), qt.dtype)
    return jnp.concatenate([jnp.concatenate([qt[:d], z], axis=0),
                            jnp.concatenate([z, qt[d:]], axis=0)], axis=1)


def _pair_scores(dst_ref, key_blocks, q_bd, bias_blocks):
    r = 0
    for kk, bias in zip(key_blocks, bias_blocks):
        s = _dot(kk, q_bd)
        dst_ref[r:r + kk.shape[0], :] = s if bias is None else s + bias
        r += kk.shape[0]


def _pair_softmax_pv(src_ref, vt_blocks):
    n_keys = sum(vt.shape[1] for vt in vt_blocks)
    m = jnp.max(src_ref[0:n_keys, :], axis=0, keepdims=True)
    den = acc = None
    r = 0
    for vt in vt_blocks:
        p = jnp.exp2(src_ref[r:r + vt.shape[1], :] - m)
        li = jnp.sum(p, axis=0, keepdims=True)
        oi = _dot(vt, p.astype(bf16))
        den = li if den is None else den + li
        acc = oi if acc is None else acc + oi
        r += vt.shape[1]
    o = acc / den
    dv = o.shape[0] // 2
    return jnp.concatenate([o[:dv, :TM], o[dv:, TM:]], axis=0).T.astype(bf16)


def _pipelined_tiles(scores, finish, s_a, s_b, first_tile, last_tile, lead_in=None):
    scores(first_tile, s_a)
    if lead_in is not None:
        lead_in(s_b)

    def body(i, carry):
        t0 = first_tile + 2 * i
        scores(t0 + 1, s_b)
        finish(t0, s_a)
        scores(t0 + 2, s_a)
        finish(t0 + 1, s_b)
        return carry

    lax.fori_loop(0, (last_tile - first_tile + 1) // 2 - 1, body, 0)
    scores(last_tile, s_b)
    finish(last_tile - 1, s_a)
    finish(last_tile, s_b)


NA_TILE_ROWS = TM // GRID_W
NA_UNION_ROWS = 12
NA_UNION_KEYS = NA_UNION_ROWS * GRID_W
NA_PATTERNS = 3
NA_BIAS_SHIFTS = 2 * WIN_ROWS


def _na_window_lo(pat, rr):
    return (-rr, -(WIN_ROWS // 2), -(WIN_ROWS // 2) - rr)[pat]


def _na_fill_bias(blk_ref, bias_scr):
    lane = lax.broadcasted_iota(jnp.int32, (1, LANES), 1)
    left = lane < GRID_W
    neg = jnp.full((GRID_W, LANES), NEG_INF, f32)
    for pat in range(NA_PATTERNS):
        for ii in range(NA_UNION_ROWS):
            for t in range(NA_TILE_ROWS // 2):
                d = ii - NA_TILE_ROWS * pat - 2 * t
                ok_l = 0 <= d - _na_window_lo(pat, 2 * t) < WIN_ROWS
                ok_r = 0 <= (d - 1) - _na_window_lo(pat, 2 * t + 1) < WIN_ROWS
                for hh in range(2):
                    if ok_l or ok_r:
                        blk = blk_ref[0, hh, d + WIN_ROWS - 1]
                        if not ok_l:
                            blk = jnp.where(left, NEG_INF, blk)
                        if not ok_r:
                            blk = jnp.where(left, blk, NEG_INF)
                    else:
                        blk = neg
                    c0 = hh * TM + t * LANES
                    bias_scr[pat, ii * GRID_W:(ii + 1) * GRID_W, c0:c0 + LANES] = blk


def _query_tile(qt_ref, t):
    return _pair_block_diag(qt_ref[0, :, pl.ds(pl.multiple_of(t * TM, TM), TM)])


def _out_rows(t, with_ctx):
    return pl.ds(pl.multiple_of((t - (0 if with_ctx else 1)) * TM, TM), TM)


def _na_kernel(qt_ref, k_ref, vt_ref, blk_ref, o_ref, bias_scr, s_a, s_b, *, with_ctx):
    @pl.when(pl.program_id(1) == 0)
    def _():
        _na_fill_bias(blk_ref, bias_scr)

    kc = k_ref[0, 0:CTX_LEN, :]
    vtc = vt_ref[0, :, 0:CTX_LEN]

    def window(t):
        r0 = (t - 1) * NA_TILE_ROWS
        base = jnp.clip(r0 - WIN_ROWS // 2, 0, GRID_ROWS - NA_UNION_ROWS)
        pat = (r0 - base) // NA_TILE_ROWS
        start = pl.multiple_of(CTX_LEN + base * GRID_W, NA_TILE_ROWS * GRID_W)
        return pat, pl.ds(start, NA_UNION_KEYS)

    def scores(t, dst):
        pat, keys = window(t)
        _pair_scores(dst, [k_ref[0, keys, :], kc], _query_tile(qt_ref, t), [bias_scr[pat], None])

    def finish(t, src):
        _, keys = window(t)
        o_ref[0, _out_rows(t, with_ctx), :] = _pair_softmax_pv(src, [vt_ref[0, :, keys], vtc])

    def ctx_finish(buf):
        o_ref[0, 0:TM, :] = _pair_softmax_pv(buf, [vtc])

    if with_ctx:
        _pair_scores(s_b, [kc], _query_tile(qt_ref, 0), [None])
    _pipelined_tiles(scores, finish, s_a, s_b, 1, N_TILES - 1, ctx_finish if with_ctx else None)


def _na_call(qnt, kn, vnt, bias_blocks, l, with_ctx):
    n_pairs = NA_WIDTH // LANES
    s_out = S_ALL if with_ctx else SEQ
    logits = pltpu.VMEM((NA_UNION_KEYS + CTX_LEN, 2 * TM), f32)
    return pl.pallas_call(
        functools.partial(_na_kernel, with_ctx=with_ctx),
        grid=(n_pairs, BATCH),
        in_specs=[
            pl.BlockSpec((1, LANES, S_ALL), lambda p, b: (b, p, 0)),
            pl.BlockSpec((1, S_ALL, LANES), lambda p, b: (b, 0, p)),
            pl.BlockSpec((1, LANES, S_ALL), lambda p, b: (b, p, 0)),
            pl.BlockSpec((1, 2, NA_BIAS_SHIFTS, GRID_W, LANES), lambda p, b: (l, p, 0, 0, 0)),
        ],
        out_specs=pl.BlockSpec((1, s_out, LANES), lambda p, b: (b, 0, p)),
        out_shape=jax.ShapeDtypeStruct((BATCH, s_out, NA_WIDTH), bf16),
        scratch_shapes=[pltpu.VMEM((NA_PATTERNS, NA_UNION_KEYS, 2 * TM), f32), logits, logits],
        compiler_params=pltpu.CompilerParams(
            dimension_semantics=("arbitrary", "arbitrary"), vmem_limit_bytes=VMEM_LIMIT),
        name="na_attention",
    )(qnt, kn, vnt, bias_blocks)


MLA_KEY_BLOCK = 768


def _mla_kernel(qt_ref, k_ref, vt_ref, o_ref, s_a, s_b, *, with_ctx):
    def scores(t, dst):
        _pair_scores(dst, [k_ref[0]], _query_tile(qt_ref, t), [None])

    def finish(t, src):
        vts = [vt_ref[0, :, s0:s0 + MLA_KEY_BLOCK] for s0 in range(0, S_ALL, MLA_KEY_BLOCK)]
        o_ref[0, _out_rows(t, with_ctx), :] = _pair_softmax_pv(src, vts)

    def ctx_finish(buf):
        o_ref[0, 0:TM, :] = _pair_softmax_pv(buf, [vt_ref[0, :, 0:CTX_LEN]])

    if with_ctx:
        _pair_scores(s_b, [k_ref[0, 0:CTX_LEN, :]], _query_tile(qt_ref, 0), [None])
    _pipelined_tiles(scores, finish, s_a, s_b, 1, N_TILES - 1, ctx_finish if with_ctx else None)


def _mla_call(qmt, km, vmt, with_ctx):
    n_pairs = MLA_HEADS // 2
    s_out = S_ALL if with_ctx else SEQ
    logits = pltpu.VMEM((S_ALL, 2 * TM), f32)
    return pl.pallas_call(
        functools.partial(_mla_kernel, with_ctx=with_ctx),
        grid=(BATCH, n_pairs),
        in_specs=[
            pl.BlockSpec((1, 2 * HEAD_PAD, S_ALL), lambda b, p: (b, p, 0)),
            pl.BlockSpec((1, S_ALL, 2 * HEAD_PAD), lambda b, p: (b, 0, p)),
            pl.BlockSpec((1, LANES, S_ALL), lambda b, p: (b, p, 0)),
        ],
        out_specs=pl.BlockSpec((1, s_out, LANES), lambda b, p: (b, 0, p)),
        out_shape=jax.ShapeDtypeStruct((BATCH, s_out, MLA_WIDTH), bf16),
        scratch_shapes=[logits, logits],
        compiler_params=pltpu.CompilerParams(
            dimension_semantics=("arbitrary", "arbitrary"), vmem_limit_bytes=VMEM_LIMIT),
        name="mla_attention",
    )(qmt, km, vmt)


S5_BLOCK_T = 16
S5_BLOCK_ROWS = S5_BLOCK_T * BATCH


def _s5_scan_kernel(*refs, reverse, finish):
    u_ref, bmat_ref, cmat_ref, lam_ref = refs[:4]
    if finish:
        y_other_ref, d_ref, wglu_ref, bglu_ref = refs[4:8]
    y_ref, bu_ref, h_ref, st_ref = refs[-4:]

    @pl.when(pl.program_id(0) == 0)
    def _():
        st_ref[...] = jnp.zeros_like(st_ref)

    def rows(k):
        return slice(k * S5_BLOCK_ROWS, (k + 1) * S5_BLOCK_ROWS)

    def project(k):
        bu_ref[rows(k), :] = _dot(u_ref[rows(k), :].astype(bf16), bmat_ref[0, 0])

    def readout(k):
        y = _dot(h_ref[rows(k), :].astype(bf16), cmat_ref[0, 0])
        if not finish:
            y_ref[rows(k), :] = y
            return
        y = (y_other_ref[rows(k), :] + y) + d_ref[0] * u_ref[rows(k), :]
        c0 = math.sqrt(2.0 / math.pi)
        gl = 0.5 * y * (1.0 + jnp.tanh(c0 * (y + 0.044715 * (y * y * y))))
        o = _dot(gl.astype(bf16), wglu_ref[0]) + bglu_ref[0]
        y_ref[rows(k), :] = (o[:, :S5_WIDTH] * _sigmoid(o[:, S5_WIDTH:])).astype(bf16)

    n_blocks = S5_T // S5_BLOCK_T
    order = list(range(n_blocks))[::-1] if reverse else list(range(n_blocks))
    steps = list(range(S5_BLOCK_T))[::-1] if reverse else list(range(S5_BLOCK_T))
    hr = st_ref[:, 0:S5_LANES]
    hi = st_ref[:, S5_LANES:]
    project(order[0])
    for n, k in enumerate(order):
        if n + 1 < n_blocks:
            project(order[n + 1])
        for t in steps:
            r0 = k * S5_BLOCK_ROWS + t * BATCH
            lr = lam_ref[0, 0, :, 0:S5_LANES]
            li = lam_ref[0, 0, :, S5_LANES:]
            nr = (lr * hr - li * hi) + bu_ref[r0:r0 + BATCH, 0:S5_LANES]
            ni = (lr * hi + li * hr) + bu_ref[r0:r0 + BATCH, S5_LANES:]
            h_ref[r0:r0 + BATCH, 0:S5_LANES] = nr
            h_ref[r0:r0 + BATCH, S5_LANES:] = ni
            hr, hi = nr, ni
        if n >= 1:
            readout(order[n - 1])
    readout(order[-1])
    st_ref[:, 0:S5_LANES] = hr
    st_ref[:, S5_LANES:] = hi


def _s5_scan_call(u_t, p, l, d, y_other=None):
    reverse = d == 1
    finish = y_other is not None
    if reverse:
        def chunk(i):
            return jnp.where(i < S5_CTX_CHUNKS, S5_CTX_CHUNKS - 1 - i, S5_CHUNKS + S5_CTX_CHUNKS - 1 - i)
    else:
        def chunk(i):
            return i
    row_spec = pl.BlockSpec((S5_ROWS, S5_WIDTH), lambda i: (chunk(i), 0))
    ins = [u_t, p["bmat"], p["cmat"], p["lam"]]
    in_specs = [row_spec, _param_spec(p["bmat"], l, d), _param_spec(p["cmat"], l, d), _param_spec(p["lam"], l, d)]
    if finish:
        ins += [y_other, p["s5_d"], p["w_glu"], p["b_glu"]]
        in_specs += [row_spec, _param_spec(p["s5_d"], l), _param_spec(p["w_glu"], l), _param_spec(p["b_glu"], l)]
    return pl.pallas_call(
        functools.partial(_s5_scan_kernel, reverse=reverse, finish=finish),
        grid=(S5_CHUNKS,),
        in_specs=in_specs,
        out_specs=row_spec,
        out_shape=jax.ShapeDtypeStruct((S_ALL * BATCH, S5_WIDTH), bf16 if finish else f32),
        scratch_shapes=[pltpu.VMEM((S5_ROWS, 2 * S5_LANES), f32), pltpu.VMEM((S5_ROWS, 2 * S5_LANES), f32),
                        pltpu.VMEM((BATCH, 2 * S5_LANES), f32)],
        compiler_params=pltpu.CompilerParams(
            dimension_semantics=("arbitrary",), vmem_limit_bytes=VMEM_LIMIT),
        name="s5_scan_bwd" if reverse else "s5_scan_fwd",
    )(*ins)


def _rope_partner(x, signed=True):
    half = MLA_ROPE // 2
    quarter = half // 2
    rows = []
    for k in range(MLA_ROPE):
        row = [0.0] * MLA_ROPE
        if (k % half) < quarter:
            row[k + quarter] = 1.0
        else:
            row[k - quarter] = -1.0 if signed else 1.0
        rows.append(row)
    return jnp.einsum('...k,kj->...j', x, jnp.array(rows, f32), precision=lax.Precision.HIGHEST)


def _rope_tables():
    quarter = MLA_ROPE // 4
    t = jnp.arange(SEQ)
    row = (t // GRID_W).astype(f32)
    col = (t % GRID_W).astype(f32)
    inv = ROPE_THETA ** (-jnp.arange(quarter, dtype=f32) / quarter)
    ang = jnp.concatenate([row[:, None] * inv] * 2 + [col[:, None] * inv] * 2, axis=-1)
    ang = jnp.concatenate([jnp.zeros((CTX_LEN, MLA_ROPE), f32), ang], axis=0)
    ones = jnp.ones((S_ALL, MLA_NOPE), f32)
    pad = jnp.zeros((S_ALL, HEAD_PAD - MLA_QK), f32)
    cos_t = jnp.concatenate([ones, jnp.cos(ang), pad], axis=-1)
    sin_t = jnp.concatenate([0.0 * ones, jnp.sin(ang), pad], axis=-1)
    return cos_t, sin_t


def _na_bias_blocks(rpb):
    cq = jnp.arange(GRID_W)[None, :]
    kc = jnp.arange(GRID_W)[:, None]
    d_col = jnp.clip(kc - cq, -(WIN_COLS - 1), WIN_COLS - 1) + WIN_COLS - 1
    col_start = jnp.clip(cq - WIN_COLS // 2, 0, GRID_W - WIN_COLS)
    in_win = (kc >= col_start) & (kc < col_start + WIN_COLS)
    onehot = (d_col[:, :, None] == jnp.arange(2 * WIN_COLS - 1)[None, None, :]).astype(f32)
    blocks = jnp.einsum('lhdc,kqc->lhdkq', rpb.astype(f32), onehot, precision=lax.Precision.HIGHEST)
    blocks = jnp.where(in_win, blocks * LOG2E, NEG_INF)
    neg = jnp.full(blocks.shape[:2] + (1, GRID_W, GRID_W), NEG_INF, f32)
    padded = jnp.concatenate([neg, blocks, neg], axis=2)
    return jnp.concatenate([padded[:, :, 1:], padded[:, :, :-1]], axis=-1)


def _pad_heads(w, width):
    lead = w.shape[:-1]
    w = w.reshape(lead + (MLA_HEADS, width))
    w = jnp.pad(w, [(0, 0)] * len(lead) + [(0, 0), (0, HEAD_PAD - width)])
    return w.reshape(lead + (MLA_HEADS * HEAD_PAD,))


def _block_diag_groups(w):
    lead = w.shape[:-3]
    a, b = w.shape[-2:]
    shape = (S5_GROUPS * a, S5_GROUPS * b)
    repeat = jnp.tile(jnp.eye(b, dtype=f32), (1, S5_GROUPS))
    tiled = jnp.einsum('...rb,bc->...rc', w.reshape(lead + (S5_GROUPS * a, b)).astype(f32), repeat,
                       precision=lax.Precision.HIGHEST)
    same_group = (lax.broadcasted_iota(jnp.int32, shape, 0) // a
                  == lax.broadcasted_iota(jnp.int32, shape, 1) // b)
    return jnp.where(same_group, tiled, 0.0)


def _prepare_params(ffn_w_gu, ffn_w_down, w_in, w_out, na_qk_g, na_rpb, mla_cq_g, mla_ckv_g, mla_w_uq,
                    mla_w_ukv, mla_qk_g, s5_lam_re, s5_lam_im, s5_log_dt, s5_b_re, s5_b_im, s5_c_re,
                    s5_c_im, s5_d, s5_w_glu, s5_b_glu):
    p = {}
    p["wgu"] = ffn_w_gu.astype(bf16)
    p["wd"] = ffn_w_down.astype(bf16)
    p["w_out"] = w_out.astype(bf16)
    o_kr = Z_MAIN
    o_u = o_kr + MLA_ROPE
    w_kr = w_in[:, :, o_kr:o_kr + MLA_ROPE]
    p["win"] = w_in.astype(bf16)
    p["win_tail"] = jnp.concatenate([
        w_in[:, :, o_u:o_u + S5_WIDTH], w_kr, _rope_partner(w_kr),
        jnp.zeros((DEPTH, D_MODEL, LANES - 2 * MLA_ROPE), f32)], axis=-1).astype(bf16)
    p["naq_g"] = jnp.tile(na_qk_g[:, 0], (1, NA_HEADS))[:, None, :]
    p["nak_g"] = jnp.tile(na_qk_g[:, 1], (1, NA_HEADS))[:, None, :]
    p["cq_g"] = mla_cq_g[:, None, :]
    p["ckv_g"] = mla_ckv_g[:, None, :]
    wuq = mla_w_uq.reshape(DEPTH, Q_LORA, MLA_HEADS, MLA_QK)
    partner = jnp.concatenate([jnp.zeros((DEPTH, Q_LORA, MLA_HEADS, MLA_NOPE), f32),
                               _rope_partner(wuq[..., MLA_NOPE:])], axis=-1)
    p["wq"] = jnp.concatenate([_pad_heads(mla_w_uq, MLA_QK),
                               _pad_heads(partner.reshape(DEPTH, Q_LORA, -1), MLA_QK)], axis=-1).astype(bf16)
    wukv = mla_w_ukv.reshape(DEPTH, KV_LORA, MLA_HEADS, MLA_NOPE + MLA_V_DIM)
    p["wkv"] = jnp.concatenate([_pad_heads(wukv[..., :MLA_NOPE].reshape(DEPTH, KV_LORA, -1), MLA_NOPE),
                                wukv[..., MLA_NOPE:].reshape(DEPTH, KV_LORA, -1)], axis=-1).astype(bf16)

    def pad_gain(g):
        z = jnp.zeros((DEPTH, HEAD_PAD - MLA_QK), f32)
        full = jnp.concatenate([g, z], axis=-1)[:, None, :]
        part = jnp.concatenate([jnp.zeros((DEPTH, MLA_NOPE), f32), _rope_partner(g[:, MLA_NOPE:], signed=False), z],
                               axis=-1)[:, None, :]
        return full, part

    p["gq"], p["gqp"] = pad_gain(mla_qk_g[:, 0])
    p["gk"], p["gkp"] = pad_gain(mla_qk_g[:, 1])
    p["na_bias"] = _na_bias_blocks(na_rpb)
    lre = s5_lam_re.astype(f32)
    lim = s5_lam_im.astype(f32)
    dt = jnp.exp(s5_log_dt.astype(f32))[..., None]
    mag = jnp.exp(lre * dt)
    bar_re = mag * jnp.cos(lim * dt)
    bar_im = mag * jnp.sin(lim * dt)
    den = lre * lre + lim * lim
    q_re = ((bar_re - 1.0) * lre + bar_im * lim) / den
    q_im = (bar_im * lre - (bar_re - 1.0) * lim) / den
    bre = s5_b_re.astype(f32)
    bim = s5_b_im.astype(f32)
    bbar_re = q_re[..., None] * bre - q_im[..., None] * bim
    bbar_im = q_re[..., None] * bim + q_im[..., None] * bre
    p["bmat"] = jnp.concatenate([_block_diag_groups(jnp.swapaxes(bbar_re, -1, -2)),
                                 _block_diag_groups(jnp.swapaxes(bbar_im, -1, -2))], axis=-1).astype(bf16)
    p["cmat"] = jnp.concatenate([_block_diag_groups(jnp.swapaxes(s5_c_re.astype(f32), -1, -2)),
                                 -_block_diag_groups(jnp.swapaxes(s5_c_im.astype(f32), -1, -2))],
                                axis=-2).astype(bf16)
    lam_row = jnp.concatenate([bar_re.reshape(DEPTH, 2, S5_LANES), bar_im.reshape(DEPTH, 2, S5_LANES)], axis=-1)
    p["lam"] = jnp.broadcast_to(lam_row[:, :, None, :], (DEPTH, 2, BATCH, 2 * S5_LANES))
    p["s5_d"] = s5_d[:, None, :]
    p["w_glu"] = s5_w_glu.astype(bf16)
    p["b_glu"] = s5_b_glu[:, None, :]
    return p


def kernel(x, c, ctx, c_ctx, w_mod, b_mod, norm_g, ffn_w_gu, ffn_w_down, w_in, w_out, na_qk_g, na_rpb, mla_cq_g, mla_ckv_g, mla_w_uq, mla_w_ukv, mla_qk_g, s5_lam_re, s5_lam_im, s5_log_dt, s5_b_re, s5_b_im, s5_c_re, s5_c_im, s5_d, s5_w_glu, s5_b_glu):
    assert x.shape == (BATCH, SEQ, D_MODEL) and ctx.shape == (BATCH, CTX_LEN, D_MODEL)
    mod_rows = 16
    cvec = jnp.concatenate([c, c_ctx[None, :], jnp.zeros((mod_rows - BATCH - 1, D_MODEL), f32)], axis=0)
    mod_all = _mod_call(cvec, w_mod, b_mod).reshape(DEPTH, mod_rows, N_MOD, D_MODEL)
    cos_t, sin_t = _rope_tables()
    p = _prepare_params(ffn_w_gu, ffn_w_down, w_in, w_out, na_qk_g, na_rpb, mla_cq_g, mla_ckv_g,
                        mla_w_uq, mla_w_ukv, mla_qk_g, s5_lam_re, s5_lam_im, s5_log_dt, s5_b_re,
                        s5_b_im, s5_c_re, s5_c_im, s5_d, s5_w_glu, s5_b_glu)
    norm_g4 = norm_g[:, :, None, :]
    h = (ctx, x)
    for l in range(DEPTH):
        mod_c = jnp.broadcast_to(mod_all[l, BATCH][None], (BATCH, N_MOD, D_MODEL))
        mod = jnp.stack([mod_c, mod_all[l, :BATCH]], axis=1)
        need_ctx = l < DEPTH - 1
        h = _ffn_call(h, mod, norm_g4, p["wgu"], p["wd"], l, 0)
        qn, kn, vn, qm, km, vm, u = _inproj_call(h, mod, norm_g4, l, p, cos_t, sin_t)
        a = _na_call(qn, kn, vn, p["na_bias"], l, need_ctx)
        bm = _mla_call(qm, km, vm, need_ctx)
        u_t = u.transpose(1, 0, 2).reshape(S_ALL * BATCH, S5_WIDTH)
        yf = _s5_scan_call(u_t, p, l, 0)
        s_t = _s5_scan_call(u_t, p, l, 1, y_other=yf)
        if not need_ctx:
            s_t = s_t[CTX_LEN * BATCH:]
        s = s_t.reshape(-1, BATCH, S5_WIDTH).transpose(1, 0, 2)
        h = _ffn_call(h, mod, norm_g4, p["wgu"], p["wd"], l, 1, mixers=(a, bm, s), w_out=p["w_out"],
                      out_ctx=need_ctx)
    return h
```

```python
import functools
import math

import jax
import jax.numpy as jnp
from jax import lax
from jax.experimental import pallas as pl
from jax.experimental.pallas import tpu as pltpu

D_MODEL = 1024
BATCH = 8
SEQ = 2048
DEPTH = 2
CTX_LEN = 256
S_ALL = CTX_LEN + SEQ
GRID_W = 64
GRID_ROWS = SEQ // GRID_W
NA_HEAD_DIM = 64
NA_WIDTH = 384
NA_HEADS = 6
WIN_ROWS = 8
WIN_COLS = 16
MLA_V_DIM = 64
MLA_WIDTH = 384
MLA_HEADS = 6
MLA_NOPE = 64
MLA_ROPE = 32
MLA_QK = 96
Q_LORA = 384
KV_LORA = 256
S5_WIDTH = 256
S5_GROUP = 16
S5_GROUPS = 16
S5_STATE = 64
S5_LANES = S5_GROUPS * S5_STATE
D_FF = 2816
ROPE_THETA = 10000.0
EPS = 1e-6
N_MOD = 9
NEG_INF = -1e30

LANES = 128
TM = 256
N_TILES = S_ALL // TM
MXU_DIM = 256
FF_CHUNKS = ((0, 6 * MXU_DIM), (6 * MXU_DIM, D_FF))
HEAD_PAD = 128
S5_T = 256
S5_ROWS = S5_T * BATCH
S5_CHUNKS = S_ALL // S5_T
S5_CTX_CHUNKS = CTX_LEN // S5_T
VMEM_LIMIT = 56 * 1024 * 1024

ZC_QA, ZC_KA, ZC_VA, ZC_CQ, ZC_CKV = 0, 384, 768, 1152, 1536
Z_MAIN = 1792
Z_TAIL = S5_WIDTH + LANES

LOG2E = math.log2(math.e)
NA_Q_SCALE = NA_HEAD_DIM ** -0.5 * LOG2E
MLA_Q_SCALE = MLA_QK ** -0.5 * LOG2E

f32 = jnp.float32
bf16 = jnp.bfloat16


def _dot(a, b):
    return jnp.dot(a, b, preferred_element_type=f32)


def _sigmoid(x):
    return 1.0 / (1.0 + jnp.exp(-x))


def _rms(x, n):
    return lax.rsqrt(jnp.sum(x * x, axis=-1, keepdims=True) / n + EPS)


def _modulated(x, g, shift, scale):
    y = x * _rms(x, D_MODEL)
    return (y * g) * (1.0 + scale) + shift


def _mod_kernel(c_ref, w_ref, b_ref, o_ref):
    c = c_ref[...]
    a = (c * _sigmoid(c)).astype(bf16)
    o_ref[0] = _dot(a, w_ref[0].astype(bf16)) + b_ref[0]


def _mod_call(cvec, w_mod, b_mod):
    rows = cvec.shape[0]
    return pl.pallas_call(
        _mod_kernel,
        grid=(DEPTH, N_MOD),
        in_specs=[
            pl.BlockSpec((rows, D_MODEL), lambda l, j: (0, 0)),
            pl.BlockSpec((1, D_MODEL, D_MODEL), lambda l, j: (l, 0, j)),
            pl.BlockSpec((1, 1, D_MODEL), lambda l, j: (l, 0, j)),
        ],
        out_specs=pl.BlockSpec((1, rows, D_MODEL), lambda l, j: (l, 0, j)),
        out_shape=jax.ShapeDtypeStruct((DEPTH, rows, N_MOD * D_MODEL), f32),
        compiler_params=pltpu.CompilerParams(
            dimension_semantics=("arbitrary", "arbitrary"), vmem_limit_bytes=VMEM_LIMIT),
        name="adaln_mod",
    )(cvec, w_mod, b_mod.reshape(DEPTH, 1, N_MOD * D_MODEL))


def _param_spec(arr, *lead, single=False):
    tail = arr.shape[len(lead):]
    idx = tuple(lead) + (0,) * len(tail)
    mode = pl.Buffered(1) if single else None
    return pl.BlockSpec((1,) * len(lead) + tail, lambda *_: idx, pipeline_mode=mode)


FFN_SUBTILES = 4


def _ffn_kernel(*refs, i0, fuse_outproj, split_input, tiles_per_batch, first):
    n_sub = FFN_SUBTILES
    refs = list(refs)
    n_src = 2 if split_input else 1
    srcs = [refs[g * n_src:(g + 1) * n_src] for g in range(n_sub)]
    refs = refs[n_sub * n_src:]
    mods = refs[:n_sub]
    g_ref, wgu_ref, wd_ref = refs[n_sub:n_sub + 3]
    refs = refs[n_sub + 3:]
    if fuse_outproj:
        mixers = [refs[3 * g:3 * g + 3] for g in range(n_sub)]
        wo_ref = refs[3 * n_sub]
        refs = refs[3 * n_sub + 1:]
    o_ref, x_scr, xm_scr = refs

    def mod_set(g):
        j = (pl.program_id(0) * n_sub + g) % tiles_per_batch + first
        return j, jnp.minimum(j, 1)

    def prologue(g):
        j, sel = mod_set(g)
        rows = slice(g * TM, (g + 1) * TM)
        mod_ref = mods[g]
        if split_input:
            x = jnp.where(j == 0, srcs[g][0][0], srcs[g][1][0])
        else:
            x = srcs[g][0][0]
        if fuse_outproj:
            a_ref, b_ref, s_ref = mixers[g]
            mixed = jnp.concatenate([a_ref[0], b_ref[0], s_ref[0]], axis=-1)
            x = x + mod_ref[0, sel, 5:6, :] * _dot(mixed, wo_ref[0])
        x_scr[rows, :] = x
        shift = mod_ref[0, sel, i0:i0 + 1, :]
        scale = mod_ref[0, sel, i0 + 1:i0 + 2, :]
        xm_scr[rows, :] = _modulated(x, g_ref[0, 0], shift, scale).astype(bf16)

    def chunk(g, c, acc):
        rows = slice(g * TM, (g + 1) * TM)
        lo, hi = FF_CHUNKS[c]
        gt = _dot(xm_scr[rows, :], wgu_ref[0, 0, :, lo:hi])
        up = _dot(xm_scr[rows, :], wgu_ref[0, 0, :, D_FF + lo:D_FF + hi])
        a = ((gt * _sigmoid(gt)) * up).astype(bf16)
        part = _dot(a, wd_ref[0, 0, lo:hi, :])
        return part if acc is None else acc + part

    def epilogue(g, acc):
        _, sel = mod_set(g)
        gate = mods[g][0, sel, i0 + 2:i0 + 3, :]
        o_ref[g] = x_scr[g * TM:(g + 1) * TM, :] + (0.5 * gate) * acc

    assert len(FF_CHUNKS) == 2
    prologue(0)
    acc_prev = chunk(0, 0, None)
    for g in range(1, n_sub):
        prologue(g)
        acc_prev = chunk(g - 1, 1, acc_prev)
        acc_cur = chunk(g, 0, None)
        epilogue(g - 1, acc_prev)
        acc_prev = acc_cur
    acc_prev = chunk(n_sub - 1, 1, acc_prev)
    epilogue(n_sub - 1, acc_prev)


def _ffn_call(h, mod, norm_g, wgu, wd, l, k, *, mixers=None, w_out=None, out_ctx=True):
    split = isinstance(h, tuple)
    first = 0 if out_ctx else 1
    tpb = N_TILES - first
    n_sub = FFN_SUBTILES

    def tile_of(i, g):
        t = i * n_sub + g
        return t // tpb, t % tpb

    ins, in_specs = [], []
    for g in range(n_sub):
        if split:
            ins += list(h)
            in_specs += [
                pl.BlockSpec((1, TM, D_MODEL), lambda i, g=g: (tile_of(i, g)[0], 0, 0)),
                pl.BlockSpec((1, TM, D_MODEL),
                             lambda i, g=g: (tile_of(i, g)[0], jnp.maximum(tile_of(i, g)[1] + first - 1, 0), 0))]
        else:
            ins.append(h)
            in_specs.append(pl.BlockSpec(
                (1, TM, D_MODEL), lambda i, g=g: (tile_of(i, g)[0], tile_of(i, g)[1] + first, 0)))
    for g in range(n_sub):
        ins.append(mod)
        in_specs.append(pl.BlockSpec((1, 2, N_MOD, D_MODEL), lambda i, g=g: (tile_of(i, g)[0], 0, 0, 0)))
    ins += [norm_g, wgu, wd]
    in_specs += [_param_spec(norm_g, l, 2 * k), _param_spec(wgu, l, k, single=True),
                 _param_spec(wd, l, k, single=True)]
    if mixers is not None:
        for g in range(n_sub):
            for arr in mixers:
                ins.append(arr)
                in_specs.append(pl.BlockSpec((1, TM, arr.shape[-1]), lambda i, g=g: tile_of(i, g) + (0,)))
        ins.append(w_out)
        in_specs.append(_param_spec(w_out, l, single=True))
    rows = n_sub * TM
    out = pl.pallas_call(
        functools.partial(_ffn_kernel, i0=6 * k, fuse_outproj=mixers is not None, split_input=split,
                          tiles_per_batch=tpb, first=first),
        grid=(BATCH * tpb // n_sub,),
        in_specs=in_specs,
        out_specs=pl.BlockSpec((n_sub, TM, D_MODEL), lambda i: (i, 0, 0)),
        out_shape=jax.ShapeDtypeStruct((BATCH * tpb, TM, D_MODEL), f32),
        scratch_shapes=[pltpu.VMEM((rows, D_MODEL), f32), pltpu.VMEM((rows, D_MODEL), bf16)],
        compiler_params=pltpu.CompilerParams(
            dimension_semantics=("arbitrary",), vmem_limit_bytes=VMEM_LIMIT),
        name="swiglu_half_step",
    )(*ins)
    return out.reshape(BATCH, tpb * TM, D_MODEL)


INPROJ_SUBTILES = 3
V_HEAD = 64
VT_HEAD_ROWS = V_HEAD + 16
VT_ROWS = 2 * VT_HEAD_ROWS
VT_ALL = 3 * VT_ROWS


def _store_values_t(vt_ref, p, cols, v_pair):
    vt = v_pair.T.astype(bf16)
    ones = jnp.ones((VT_HEAD_ROWS - V_HEAD, vt.shape[1]), bf16)
    for hh in range(2):
        r0 = p * VT_ROWS + hh * VT_HEAD_ROWS
        vt_ref[0, r0:r0 + V_HEAD, cols] = vt[hh * V_HEAD:(hh + 1) * V_HEAD, :]
        vt_ref[0, r0 + V_HEAD:r0 + VT_HEAD_ROWS, cols] = ones


def _pair_head_norm(x, g):
    lane = lax.broadcasted_iota(jnp.int32, (1, LANES), 1)
    lo = lane < NA_HEAD_DIM
    outs = []
    for p in range(NA_WIDTH // LANES):
        xb = x[:, p * LANES:(p + 1) * LANES]
        sq = xb * xb
        s_lo = jnp.sum(jnp.where(lo, sq, 0.0), axis=-1, keepdims=True)
        s_hi = jnp.sum(jnp.where(lo, 0.0, sq), axis=-1, keepdims=True)
        r = jnp.where(lo, lax.rsqrt(s_lo / NA_HEAD_DIM + EPS), lax.rsqrt(s_hi / NA_HEAD_DIM + EPS))
        outs.append((xb * r) * g[:, p * LANES:(p + 1) * LANES])
    return outs


def _inproj_kernel(h_ref, mod_ref, g_ref, wmain_ref, wtail_ref, naq_g_ref, nak_g_ref, cq_g_ref, ckv_g_ref,
                   wq_ref, wkv_ref, gq_ref, gqp_ref, gk_ref, gkp_ref, cos_ref, sin_ref,
                   qn_ref, kn_ref, vn_ref, qm_ref, km_ref, vm_ref, u_ref, z_scr, zt_scr):
    def project(g):
        rows = slice(g * TM, (g + 1) * TM)
        sel = jnp.minimum(pl.program_id(1) * INPROJ_SUBTILES + g, 1)
        xm = _modulated(h_ref[0, rows, :], g_ref[0, 0], mod_ref[0, sel, 3:4, :],
                        mod_ref[0, sel, 4:5, :]).astype(bf16)
        z_scr[rows, :] = _dot(xm, wmain_ref[0])
        zt_scr[rows, :] = _dot(xm, wtail_ref[0])

    def heads(g):
        rows = slice(g * TM, (g + 1) * TM)
        _inproj_heads(rows, z_scr, zt_scr, naq_g_ref, nak_g_ref, cq_g_ref, ckv_g_ref, wq_ref, wkv_ref,
                      gq_ref, gqp_ref, gk_ref, gkp_ref, cos_ref, sin_ref,
                      qn_ref, kn_ref, vn_ref, qm_ref, km_ref, vm_ref, u_ref)

    project(0)
    for g in range(INPROJ_SUBTILES):
        if g + 1 < INPROJ_SUBTILES:
            project(g + 1)
        heads(g)


def _inproj_heads(rows, z_scr, zt_scr, naq_g_ref, nak_g_ref, cq_g_ref, ckv_g_ref, wq_ref, wkv_ref,
                  gq_ref, gqp_ref, gk_ref, gkp_ref, cos_ref, sin_ref,
                  qn_ref, kn_ref, vn_ref, qm_ref, km_ref, vm_ref, u_ref):
    z = z_scr[rows, :]
    zt = zt_scr[rows, :]

    qn = _pair_head_norm(z[:, ZC_QA:ZC_QA + NA_WIDTH], naq_g_ref[0])
    kn = _pair_head_norm(z[:, ZC_KA:ZC_KA + NA_WIDTH], nak_g_ref[0])
    for p in range(NA_WIDTH // LANES):
        qn_ref[0, p * LANES:(p + 1) * LANES, rows] = (qn[p] * NA_Q_SCALE).T.astype(bf16)
        kn_ref[0, rows, p * LANES:(p + 1) * LANES] = kn[p].astype(bf16)
        _store_values_t(vn_ref, p, rows, z[:, ZC_VA + p * LANES:ZC_VA + (p + 1) * LANES])
    u_ref[0, rows, :] = zt[:, 0:S5_WIDTH]

    cos_t = cos_ref[rows, :]
    sin_t = sin_ref[rows, :]

    cq = z[:, ZC_CQ:ZC_CQ + Q_LORA]
    ncq = ((cq * _rms(cq, Q_LORA)) * cq_g_ref[0]).astype(bf16)
    qq = _dot(ncq, wq_ref[0])
    gq = gq_ref[0]
    gqp = gqp_ref[0]
    for hd in range(MLA_HEADS):
        pre = qq[:, hd * HEAD_PAD:(hd + 1) * HEAD_PAD]
        perm = qq[:, (MLA_HEADS + hd) * HEAD_PAD:(MLA_HEADS + hd + 1) * HEAD_PAD]
        r = _rms(pre, MLA_QK)
        qm_ref[0, hd * HEAD_PAD:(hd + 1) * HEAD_PAD, rows] = (
            ((pre * gq) * cos_t + (perm * gqp) * sin_t) * (r * MLA_Q_SCALE)).T.astype(bf16)

    ckv = z[:, ZC_CKV:ZC_CKV + KV_LORA]
    nkv = ((ckv * _rms(ckv, KV_LORA)) * ckv_g_ref[0]).astype(bf16)
    kv = _dot(nkv, wkv_ref[0])
    for p in range(MLA_WIDTH // LANES):
        lo_col = MLA_HEADS * HEAD_PAD + p * LANES
        _store_values_t(vm_ref, p, rows, kv[:, lo_col:lo_col + LANES])
    krb = zt[:, S5_WIDTH:S5_WIDTH + LANES]
    lane = lax.broadcasted_iota(jnp.int32, (1, LANES), 1)
    rope_lanes = (lane >= MLA_NOPE) & (lane < MLA_QK)
    kr_a = jnp.where(rope_lanes, pltpu.roll(krb, MLA_NOPE, axis=1), 0.0)
    kr_b = jnp.where(rope_lanes, pltpu.roll(krb, MLA_ROPE, axis=1), 0.0)
    gk = gk_ref[0]
    gkp = gkp_ref[0]
    rot_part = (kr_b * gkp) * sin_t
    for hd in range(MLA_HEADS):
        kfull = kv[:, hd * HEAD_PAD:(hd + 1) * HEAD_PAD] + kr_a
        r = _rms(kfull, MLA_QK)
        km_ref[0, rows, hd * HEAD_PAD:(hd + 1) * HEAD_PAD] = (
            ((kfull * gk) * cos_t + rot_part) * r).astype(bf16)


def _inproj_call(h, mod, norm_g, l, params, cos_t, sin_t):
    qk_w = MLA_HEADS * HEAD_PAD
    rows = INPROJ_SUBTILES * TM

    def tok(width):
        return pl.BlockSpec((1, rows, width), lambda b, j: (b, j, 0))

    def tok_t(width):
        return pl.BlockSpec((1, width, rows), lambda b, j: (b, 0, j))

    tab_spec = pl.BlockSpec((rows, LANES), lambda b, j: (j, 0))
    mod_spec = pl.BlockSpec((1, 2, N_MOD, D_MODEL), lambda b, j: (b, 0, 0, 0))
    main_spec = pl.BlockSpec((1, D_MODEL, Z_MAIN), lambda b, j: (l, 0, 0))
    names = ["win", "win_tail", "naq_g", "nak_g", "cq_g", "ckv_g", "wq", "wkv", "gq", "gqp", "gk", "gkp"]
    return pl.pallas_call(
        _inproj_kernel,
        grid=(BATCH, N_TILES // INPROJ_SUBTILES),
        in_specs=([tok(D_MODEL), mod_spec, _param_spec(norm_g, l, 1), main_spec]
                  + [_param_spec(params[n], l) for n in names[1:]] + [tab_spec, tab_spec]),
        out_specs=[tok_t(NA_WIDTH), tok(NA_WIDTH), tok_t(VT_ALL),
                   tok_t(qk_w), tok(qk_w), tok_t(VT_ALL), tok(S5_WIDTH)],
        out_shape=[
            jax.ShapeDtypeStruct((BATCH, NA_WIDTH, S_ALL), bf16),
            jax.ShapeDtypeStruct((BATCH, S_ALL, NA_WIDTH), bf16),
            jax.ShapeDtypeStruct((BATCH, VT_ALL, S_ALL), bf16),
            jax.ShapeDtypeStruct((BATCH, qk_w, S_ALL), bf16),
            jax.ShapeDtypeStruct((BATCH, S_ALL, qk_w), bf16),
            jax.ShapeDtypeStruct((BATCH, VT_ALL, S_ALL), bf16),
            jax.ShapeDtypeStruct((BATCH, S_ALL, S5_WIDTH), f32),
        ],
        scratch_shapes=[pltpu.VMEM((rows, Z_MAIN), f32), pltpu.VMEM((rows, Z_TAIL), f32)],
        compiler_params=pltpu.CompilerParams(
            dimension_semantics=("arbitrary", "arbitrary"), vmem_limit_bytes=VMEM_LIMIT),
        name="in_proj_heads",
    )(h, mod, norm_g, *[params[n] for n in names], cos_t, sin_t)


def _pair_block_diag(qt):
    d = qt.shape[0] // 2
    z = jnp.zeros((d, qt.shape[1]), qt.dtype)
    return jnp.concatenate([jnp.concatenate([qt[:d], z], axis=0),
                            jnp.concatenate([z, qt[d:]], axis=0)], axis=1)


def _pair_scores(dst_ref, key_blocks, q_bd, bias_blocks):
    r = 0
    for kk, bias in zip(key_blocks, bias_blocks):
        s = _dot(kk, q_bd)
        dst_ref[r:r + kk.shape[0], :] = s if bias is None else s + bias
        r += kk.shape[0]


def _pair_softmax_pv(src_ref, vt_blocks):
    n_keys = sum(vt.shape[1] for vt in vt_blocks)
    outs = []
    for hh in range(2):
        cols = slice(hh * TM, (hh + 1) * TM)
        head = slice(hh * VT_HEAD_ROWS, (hh + 1) * VT_HEAD_ROWS)
        m = jnp.max(src_ref[0:n_keys, cols], axis=0, keepdims=True)
        acc = None
        r = 0
        for vt in vt_blocks:
            p = jnp.exp2(src_ref[r:r + vt.shape[1], cols] - m)
            oi = _dot(vt[head, :], p.astype(bf16))
            acc = oi if acc is None else acc + oi
            r += vt.shape[1]
        outs.append(acc[:V_HEAD, :] / acc[V_HEAD:V_HEAD + 1, :])
    return jnp.concatenate(outs, axis=0).T.astype(bf16)


def _pipelined_tiles(scores, finish, s_a, s_b, first_tile, last_tile, lead_in=None):
    scores(first_tile, s_a)
    if lead_in is not None:
        lead_in(s_b)

    def body(i, carry):
        t0 = first_tile + 2 * i
        scores(t0 + 1, s_b)
        finish(t0, s_a)
        scores(t0 + 2, s_a)
        finish(t0 + 1, s_b)
        return carry

    lax.fori_loop(0, (last_tile - first_tile + 1) // 2 - 1, body, 0)
    scores(last_tile, s_b)
    finish(last_tile - 1, s_a)
    finish(last_tile, s_b)


NA_TILE_ROWS = TM // GRID_W
NA_UNION_ROWS = 12
NA_UNION_KEYS = NA_UNION_ROWS * GRID_W
NA_PATTERNS = 3
NA_BIAS_SHIFTS = 2 * WIN_ROWS


def _na_window_lo(pat, rr):
    return (-rr, -(WIN_ROWS // 2), -(WIN_ROWS // 2) - rr)[pat]


def _na_fill_bias(blk_ref, bias_scr):
    lane = lax.broadcasted_iota(jnp.int32, (1, LANES), 1)
    left = lane < GRID_W
    neg = jnp.full((GRID_W, LANES), NEG_INF, f32)
    for pat in range(NA_PATTERNS):
        for ii in range(NA_UNION_ROWS):
            for t in range(NA_TILE_ROWS // 2):
                d = ii - NA_TILE_ROWS * pat - 2 * t
                ok_l = 0 <= d - _na_window_lo(pat, 2 * t) < WIN_ROWS
                ok_r = 0 <= (d - 1) - _na_window_lo(pat, 2 * t + 1) < WIN_ROWS
                for hh in range(2):
                    if ok_l or ok_r:
                        blk = blk_ref[0, hh, d + WIN_ROWS - 1]
                        if not ok_l:
                            blk = jnp.where(left, NEG_INF, blk)
                        if not ok_r:
                            blk = jnp.where(left, blk, NEG_INF)
                    else:
                        blk = neg
                    c0 = hh * TM + t * LANES
                    bias_scr[pat, ii * GRID_W:(ii + 1) * GRID_W, c0:c0 + LANES] = blk


def _query_tile(qt_ref, t):
    return _pair_block_diag(qt_ref[0, :, pl.ds(pl.multiple_of(t * TM, TM), TM)])


def _out_rows(t, with_ctx):
    return pl.ds(pl.multiple_of((t - (0 if with_ctx else 1)) * TM, TM), TM)


def _na_kernel(qt_ref, k_ref, vt_ref, blk_ref, o_ref, bias_scr, s_a, s_b, *, with_ctx):
    @pl.when(pl.program_id(1) == 0)
    def _():
        _na_fill_bias(blk_ref, bias_scr)

    kc = k_ref[0, 0:CTX_LEN, :]
    vtc = vt_ref[0, :, 0:CTX_LEN]

    def window(t):
        r0 = (t - 1) * NA_TILE_ROWS
        base = jnp.clip(r0 - WIN_ROWS // 2, 0, GRID_ROWS - NA_UNION_ROWS)
        pat = (r0 - base) // NA_TILE_ROWS
        start = pl.multiple_of(CTX_LEN + base * GRID_W, NA_TILE_ROWS * GRID_W)
        return pat, pl.ds(start, NA_UNION_KEYS)

    def scores(t, dst):
        pat, keys = window(t)
        _pair_scores(dst, [k_ref[0, keys, :], kc], _query_tile(qt_ref, t), [bias_scr[pat], None])

    def finish(t, src):
        _, keys = window(t)
        o_ref[0, _out_rows(t, with_ctx), :] = _pair_softmax_pv(src, [vt_ref[0, :, keys], vtc])

    def ctx_finish(buf):
        o_ref[0, 0:TM, :] = _pair_softmax_pv(buf, [vtc])

    if with_ctx:
        _pair_scores(s_b, [kc], _query_tile(qt_ref, 0), [None])
    _pipelined_tiles(scores, finish, s_a, s_b, 1, N_TILES - 1, ctx_finish if with_ctx else None)


def _na_call(qnt, kn, vnt, bias_blocks, l, with_ctx):
    n_pairs = NA_WIDTH // LANES
    s_out = S_ALL if with_ctx else SEQ
    logits = pltpu.VMEM((NA_UNION_KEYS + CTX_LEN, 2 * TM), f32)
    return pl.pallas_call(
        functools.partial(_na_kernel, with_ctx=with_ctx),
        grid=(n_pairs, BATCH),
        in_specs=[
            pl.BlockSpec((1, LANES, S_ALL), lambda p, b: (b, p, 0)),
            pl.BlockSpec((1, S_ALL, LANES), lambda p, b: (b, 0, p)),
            pl.BlockSpec((1, VT_ROWS, S_ALL), lambda p, b: (b, p, 0)),
            pl.BlockSpec((1, 2, NA_BIAS_SHIFTS, GRID_W, LANES), lambda p, b: (l, p, 0, 0, 0)),
        ],
        out_specs=pl.BlockSpec((1, s_out, LANES), lambda p, b: (b, 0, p)),
        out_shape=jax.ShapeDtypeStruct((BATCH, s_out, NA_WIDTH), bf16),
        scratch_shapes=[pltpu.VMEM((NA_PATTERNS, NA_UNION_KEYS, 2 * TM), f32), logits, logits],
        compiler_params=pltpu.CompilerParams(
            dimension_semantics=("arbitrary", "arbitrary"), vmem_limit_bytes=VMEM_LIMIT),
        name="na_attention",
    )(qnt, kn, vnt, bias_blocks)


MLA_KEY_BLOCK = 768


def _mla_kernel(qt_ref, k_ref, vt_ref, o_ref, s_a, s_b, *, with_ctx):
    def scores(t, dst):
        _pair_scores(dst, [k_ref[0]], _query_tile(qt_ref, t), [None])

    def finish(t, src):
        vts = [vt_ref[0, :, s0:s0 + MLA_KEY_BLOCK] for s0 in range(0, S_ALL, MLA_KEY_BLOCK)]
        o_ref[0, _out_rows(t, with_ctx), :] = _pair_softmax_pv(src, vts)

    def ctx_finish(buf):
        o_ref[0, 0:TM, :] = _pair_softmax_pv(buf, [vt_ref[0, :, 0:CTX_LEN]])

    if with_ctx:
        _pair_scores(s_b, [k_ref[0, 0:CTX_LEN, :]], _query_tile(qt_ref, 0), [None])
    _pipelined_tiles(scores, finish, s_a, s_b, 1, N_TILES - 1, ctx_finish if with_ctx else None)


def _mla_call(qmt, km, vmt, with_ctx):
    n_pairs = MLA_HEADS // 2
    s_out = S_ALL if with_ctx else SEQ
    logits = pltpu.VMEM((S_ALL, 2 * TM), f32)
    return pl.pallas_call(
        functools.partial(_mla_kernel, with_ctx=with_ctx),
        grid=(BATCH, n_pairs),
        in_specs=[
            pl.BlockSpec((1, 2 * HEAD_PAD, S_ALL), lambda b, p: (b, p, 0)),
            pl.BlockSpec((1, S_ALL, 2 * HEAD_PAD), lambda b, p: (b, 0, p)),
            pl.BlockSpec((1, VT_ROWS, S_ALL), lambda b, p: (b, p, 0)),
        ],
        out_specs=pl.BlockSpec((1, s_out, LANES), lambda b, p: (b, 0, p)),
        out_shape=jax.ShapeDtypeStruct((BATCH, s_out, MLA_WIDTH), bf16),
        scratch_shapes=[logits, logits],
        compiler_params=pltpu.CompilerParams(
            dimension_semantics=("arbitrary", "arbitrary"), vmem_limit_bytes=VMEM_LIMIT),
        name="mla_attention",
    )(qmt, km, vmt)


S5_BLOCK_T = 16
S5_BLOCK_ROWS = S5_BLOCK_T * BATCH


def _s5_scan_kernel(*refs, reverse, finish):
    u_ref, bmat_ref, cmat_ref, lam_ref = refs[:4]
    if finish:
        y_other_ref, d_ref, wglu_ref, bglu_ref = refs[4:8]
    y_ref, bu_ref, h_ref, st_ref = refs[-4:]

    @pl.when(pl.program_id(0) == 0)
    def _():
        st_ref[...] = jnp.zeros_like(st_ref)

    def rows(k):
        return slice(k * S5_BLOCK_ROWS, (k + 1) * S5_BLOCK_ROWS)

    def project(k):
        bu_ref[rows(k), :] = _dot(u_ref[rows(k), :].astype(bf16), bmat_ref[0, 0])

    def readout(k):
        y = _dot(h_ref[rows(k), :].astype(bf16), cmat_ref[0, 0])
        if not finish:
            y_ref[rows(k), :] = y
            return
        y = (y_other_ref[rows(k), :] + y) + d_ref[0] * u_ref[rows(k), :]
        c0 = math.sqrt(2.0 / math.pi)
        gl = 0.5 * y * (1.0 + jnp.tanh(c0 * (y + 0.044715 * (y * y * y))))
        o = _dot(gl.astype(bf16), wglu_ref[0]) + bglu_ref[0]
        y_ref[rows(k), :] = (o[:, :S5_WIDTH] * _sigmoid(o[:, S5_WIDTH:])).astype(bf16)

    n_blocks = S5_T // S5_BLOCK_T
    order = list(range(n_blocks))[::-1] if reverse else list(range(n_blocks))
    steps = list(range(S5_BLOCK_T))[::-1] if reverse else list(range(S5_BLOCK_T))
    hr = st_ref[:, 0:S5_LANES]
    hi = st_ref[:, S5_LANES:]
    project(order[0])
    for n, k in enumerate(order):
        if n + 1 < n_blocks:
            project(order[n + 1])
        for t in steps:
            r0 = k * S5_BLOCK_ROWS + t * BATCH
            lr = lam_ref[0, 0, :, 0:S5_LANES]
            li = lam_ref[0, 0, :, S5_LANES:]
            nr = (lr * hr - li * hi) + bu_ref[r0:r0 + BATCH, 0:S5_LANES]
            ni = (lr * hi + li * hr) + bu_ref[r0:r0 + BATCH, S5_LANES:]
            h_ref[r0:r0 + BATCH, 0:S5_LANES] = nr
            h_ref[r0:r0 + BATCH, S5_LANES:] = ni
            hr, hi = nr, ni
        if n >= 1:
            readout(order[n - 1])
    readout(order[-1])
    st_ref[:, 0:S5_LANES] = hr
    st_ref[:, S5_LANES:] = hi


def _s5_scan_call(u_t, p, l, d, y_other=None):
    reverse = d == 1
    finish = y_other is not None
    if reverse:
        def chunk(i):
            return jnp.where(i < S5_CTX_CHUNKS, S5_CTX_CHUNKS - 1 - i, S5_CHUNKS + S5_CTX_CHUNKS - 1 - i)
    else:
        def chunk(i):
            return i
    row_spec = pl.BlockSpec((S5_ROWS, S5_WIDTH), lambda i: (chunk(i), 0))
    ins = [u_t, p["bmat"], p["cmat"], p["lam"]]
    in_specs = [row_spec, _param_spec(p["bmat"], l, d), _param_spec(p["cmat"], l, d), _param_spec(p["lam"], l, d)]
    if finish:
        ins += [y_other, p["s5_d"], p["w_glu"], p["b_glu"]]
        in_specs += [row_spec, _param_spec(p["s5_d"], l), _param_spec(p["w_glu"], l), _param_spec(p["b_glu"], l)]
    return pl.pallas_call(
        functools.partial(_s5_scan_kernel, reverse=reverse, finish=finish),
        grid=(S5_CHUNKS,),
        in_specs=in_specs,
        out_specs=row_spec,
        out_shape=jax.ShapeDtypeStruct((S_ALL * BATCH, S5_WIDTH), bf16 if finish else f32),
        scratch_shapes=[pltpu.VMEM((S5_ROWS, 2 * S5_LANES), f32), pltpu.VMEM((S5_ROWS, 2 * S5_LANES), f32),
                        pltpu.VMEM((BATCH, 2 * S5_LANES), f32)],
        compiler_params=pltpu.CompilerParams(
            dimension_semantics=("arbitrary",), vmem_limit_bytes=VMEM_LIMIT),
        name="s5_scan_bwd" if reverse else "s5_scan_fwd",
    )(*ins)


def _rope_partner(x, signed=True):
    half = MLA_ROPE // 2
    quarter = half // 2
    rows = []
    for k in range(MLA_ROPE):
        row = [0.0] * MLA_ROPE
        if (k % half) < quarter:
            row[k + quarter] = 1.0
        else:
            row[k - quarter] = -1.0 if signed else 1.0
        rows.append(row)
    return jnp.einsum('...k,kj->...j', x, jnp.array(rows, f32), precision=lax.Precision.HIGHEST)


def _rope_tables():
    quarter = MLA_ROPE // 4
    t = jnp.arange(SEQ)
    row = (t // GRID_W).astype(f32)
    col = (t % GRID_W).astype(f32)
    inv = ROPE_THETA ** (-jnp.arange(quarter, dtype=f32) / quarter)
    ang = jnp.concatenate([row[:, None] * inv] * 2 + [col[:, None] * inv] * 2, axis=-1)
    ang = jnp.concatenate([jnp.zeros((CTX_LEN, MLA_ROPE), f32), ang], axis=0)
    ones = jnp.ones((S_ALL, MLA_NOPE), f32)
    pad = jnp.zeros((S_ALL, HEAD_PAD - MLA_QK), f32)
    cos_t = jnp.concatenate([ones, jnp.cos(ang), pad], axis=-1)
    sin_t = jnp.concatenate([0.0 * ones, jnp.sin(ang), pad], axis=-1)
    return cos_t, sin_t


def _na_bias_blocks(rpb):
    cq = jnp.arange(GRID_W)[None, :]
    kc = jnp.arange(GRID_W)[:, None]
    d_col = jnp.clip(kc - cq, -(WIN_COLS - 1), WIN_COLS - 1) + WIN_COLS - 1
    col_start = jnp.clip(cq - WIN_COLS // 2, 0, GRID_W - WIN_COLS)
    in_win = (kc >= col_start) & (kc < col_start + WIN_COLS)
    onehot = (d_col[:, :, None] == jnp.arange(2 * WIN_COLS - 1)[None, None, :]).astype(f32)
    blocks = jnp.einsum('lhdc,kqc->lhdkq', rpb.astype(f32), onehot, precision=lax.Precision.HIGHEST)
    blocks = jnp.where(in_win, blocks * LOG2E, NEG_INF)
    neg = jnp.full(blocks.shape[:2] + (1, GRID_W, GRID_W), NEG_INF, f32)
    padded = jnp.concatenate([neg, blocks, neg], axis=2)
    return jnp.concatenate([padded[:, :, 1:], padded[:, :, :-1]], axis=-1)


def _pad_heads(w, width):
    lead = w.shape[:-1]
    w = w.reshape(lead + (MLA_HEADS, width))
    w = jnp.pad(w, [(0, 0)] * len(lead) + [(0, 0), (0, HEAD_PAD - width)])
    return w.reshape(lead + (MLA_HEADS * HEAD_PAD,))


def _block_diag_groups(w):
    lead = w.shape[:-3]
    a, b = w.shape[-2:]
    shape = (S5_GROUPS * a, S5_GROUPS * b)
    repeat = jnp.tile(jnp.eye(b, dtype=f32), (1, S5_GROUPS))
    tiled = jnp.einsum('...rb,bc->...rc', w.reshape(lead + (S5_GROUPS * a, b)).astype(f32), repeat,
                       precision=lax.Precision.HIGHEST)
    same_group = (lax.broadcasted_iota(jnp.int32, shape, 0) // a
                  == lax.broadcasted_iota(jnp.int32, shape, 1) // b)
    return jnp.where(same_group, tiled, 0.0)


def _prepare_params(ffn_w_gu, ffn_w_down, w_in, w_out, na_qk_g, na_rpb, mla_cq_g, mla_ckv_g, mla_w_uq,
                    mla_w_ukv, mla_qk_g, s5_lam_re, s5_lam_im, s5_log_dt, s5_b_re, s5_b_im, s5_c_re,
                    s5_c_im, s5_d, s5_w_glu, s5_b_glu):
    p = {}
    p["wgu"] = ffn_w_gu.astype(bf16)
    p["wd"] = ffn_w_down.astype(bf16)
    p["w_out"] = w_out.astype(bf16)
    o_kr = Z_MAIN
    o_u = o_kr + MLA_ROPE
    w_kr = w_in[:, :, o_kr:o_kr + MLA_ROPE]
    p["win"] = w_in.astype(bf16)
    p["win_tail"] = jnp.concatenate([
        w_in[:, :, o_u:o_u + S5_WIDTH], w_kr, _rope_partner(w_kr),
        jnp.zeros((DEPTH, D_MODEL, LANES - 2 * MLA_ROPE), f32)], axis=-1).astype(bf16)
    p["naq_g"] = jnp.tile(na_qk_g[:, 0], (1, NA_HEADS))[:, None, :]
    p["nak_g"] = jnp.tile(na_qk_g[:, 1], (1, NA_HEADS))[:, None, :]
    p["cq_g"] = mla_cq_g[:, None, :]
    p["ckv_g"] = mla_ckv_g[:, None, :]
    wuq = mla_w_uq.reshape(DEPTH, Q_LORA, MLA_HEADS, MLA_QK)
    partner = jnp.concatenate([jnp.zeros((DEPTH, Q_LORA, MLA_HEADS, MLA_NOPE), f32),
                               _rope_partner(wuq[..., MLA_NOPE:])], axis=-1)
    p["wq"] = jnp.concatenate([_pad_heads(mla_w_uq, MLA_QK),
                               _pad_heads(partner.reshape(DEPTH, Q_LORA, -1), MLA_QK)], axis=-1).astype(bf16)
    wukv = mla_w_ukv.reshape(DEPTH, KV_LORA, MLA_HEADS, MLA_NOPE + MLA_V_DIM)
    p["wkv"] = jnp.concatenate([_pad_heads(wukv[..., :MLA_NOPE].reshape(DEPTH, KV_LORA, -1), MLA_NOPE),
                                wukv[..., MLA_NOPE:].reshape(DEPTH, KV_LORA, -1)], axis=-1).astype(bf16)

    def pad_gain(g):
        z = jnp.zeros((DEPTH, HEAD_PAD - MLA_QK), f32)
        full = jnp.concatenate([g, z], axis=-1)[:, None, :]
        part = jnp.concatenate([jnp.zeros((DEPTH, MLA_NOPE), f32), _rope_partner(g[:, MLA_NOPE:], signed=False), z],
                               axis=-1)[:, None, :]
        return full, part

    p["gq"], p["gqp"] = pad_gain(mla_qk_g[:, 0])
    p["gk"], p["gkp"] = pad_gain(mla_qk_g[:, 1])
    p["na_bias"] = _na_bias_blocks(na_rpb)
    lre = s5_lam_re.astype(f32)
    lim = s5_lam_im.astype(f32)
    dt = jnp.exp(s5_log_dt.astype(f32))[..., None]
    mag = jnp.exp(lre * dt)
    bar_re = mag * jnp.cos(lim * dt)
    bar_im = mag * jnp.sin(lim * dt)
    den = lre * lre + lim * lim
    q_re = ((bar_re - 1.0) * lre + bar_im * lim) / den
    q_im = (bar_im * lre - (bar_re - 1.0) * lim) / den
    bre = s5_b_re.astype(f32)
    bim = s5_b_im.astype(f32)
    bbar_re = q_re[..., None] * bre - q_im[..., None] * bim
    bbar_im = q_re[..., None] * bim + q_im[..., None] * bre
    p["bmat"] = jnp.concatenate([_block_diag_groups(jnp.swapaxes(bbar_re, -1, -2)),
                                 _block_diag_groups(jnp.swapaxes(bbar_im, -1, -2))], axis=-1).astype(bf16)
    p["cmat"] = jnp.concatenate([_block_diag_groups(jnp.swapaxes(s5_c_re.astype(f32), -1, -2)),
                                 -_block_diag_groups(jnp.swapaxes(s5_c_im.astype(f32), -1, -2))],
                                axis=-2).astype(bf16)
    lam_row = jnp.concatenate([bar_re.reshape(DEPTH, 2, S5_LANES), bar_im.reshape(DEPTH, 2, S5_LANES)], axis=-1)
    p["lam"] = jnp.broadcast_to(lam_row[:, :, None, :], (DEPTH, 2, BATCH, 2 * S5_LANES))
    p["s5_d"] = s5_d[:, None, :]
    p["w_glu"] = s5_w_glu.astype(bf16)
    p["b_glu"] = s5_b_glu[:, None, :]
    return p


def kernel(x, c, ctx, c_ctx, w_mod, b_mod, norm_g, ffn_w_gu, ffn_w_down, w_in, w_out, na_qk_g, na_rpb, mla_cq_g, mla_ckv_g, mla_w_uq, mla_w_ukv, mla_qk_g, s5_lam_re, s5_lam_im, s5_log_dt, s5_b_re, s5_b_im, s5_c_re, s5_c_im, s5_d, s5_w_glu, s5_b_glu):
    assert x.shape == (BATCH, SEQ, D_MODEL) and ctx.shape == (BATCH, CTX_LEN, D_MODEL)
    mod_rows = 16
    cvec = jnp.concatenate([c, c_ctx[None, :], jnp.zeros((mod_rows - BATCH - 1, D_MODEL), f32)], axis=0)
    mod_all = _mod_call(cvec, w_mod, b_mod).reshape(DEPTH, mod_rows, N_MOD, D_MODEL)
    cos_t, sin_t = _rope_tables()
    p = _prepare_params(ffn_w_gu, ffn_w_down, w_in, w_out, na_qk_g, na_rpb, mla_cq_g, mla_ckv_g,
                        mla_w_uq, mla_w_ukv, mla_qk_g, s5_lam_re, s5_lam_im, s5_log_dt, s5_b_re,
                        s5_b_im, s5_c_re, s5_c_im, s5_d, s5_w_glu, s5_b_glu)
    norm_g4 = norm_g[:, :, None, :]
    h = (ctx, x)
    for l in range(DEPTH):
        mod_c = jnp.broadcast_to(mod_all[l, BATCH][None], (BATCH, N_MOD, D_MODEL))
        mod = jnp.stack([mod_c, mod_all[l, :BATCH]], axis=1)
        need_ctx = l < DEPTH - 1
        h = _ffn_call(h, mod, norm_g4, p["wgu"], p["wd"], l, 0)
        qn, kn, vn, qm, km, vm, u = _inproj_call(h, mod, norm_g4, l, p, cos_t, sin_t)
        a = _na_call(qn, kn, vn, p["na_bias"], l, need_ctx)
        bm = _mla_call(qm, km, vm, need_ctx)
        u_t = u.transpose(1, 0, 2).reshape(S_ALL * BATCH, S5_WIDTH)
        yf = _s5_scan_call(u_t, p, l, 0)
        s_t = _s5_scan_call(u_t, p, l, 1, y_other=yf)
        if not need_ctx:
            s_t = s_t[CTX_LEN * BATCH:]
        s = s_t.reshape(-1, BATCH, S5_WIDTH).transpose(1, 0, 2)
        h = _ffn_call(h, mod, norm_g4, p["wgu"], p["wd"], l, 1, mixers=(a, bm, s), w_out=p["w_out"],
                      out_ctx=need_ctx)
    return h
```

```python
import functools
import math

import jax
import jax.numpy as jnp
from jax import lax
from jax.experimental import pallas as pl
from jax.experimental.pallas import tpu as pltpu

D_MODEL = 1024
BATCH = 8
SEQ = 2048
DEPTH = 2
CTX_LEN = 256
S_ALL = CTX_LEN + SEQ
GRID_W = 64
GRID_ROWS = SEQ // GRID_W
NA_HEAD_DIM = 64
NA_WIDTH = 384
NA_HEADS = 6
WIN_ROWS = 8
WIN_COLS = 16
MLA_V_DIM = 64
MLA_WIDTH = 384
MLA_HEADS = 6
MLA_NOPE = 64
MLA_ROPE = 32
MLA_QK = 96
Q_LORA = 384
KV_LORA = 256
S5_WIDTH = 256
S5_GROUP = 16
S5_GROUPS = 16
S5_STATE = 64
S5_LANES = S5_GROUPS * S5_STATE
D_FF = 2816
ROPE_THETA = 10000.0
EPS = 1e-6
N_MOD = 9
NEG_INF = -1e30

LANES = 128
TM = 256
N_TILES = S_ALL // TM
MXU_DIM = 256
FF_CHUNKS = ((0, 6 * MXU_DIM), (6 * MXU_DIM, D_FF))
HEAD_PAD = 128
S5_T = 256
S5_ROWS = S5_T * BATCH
S5_CHUNKS = S_ALL // S5_T
S5_CTX_CHUNKS = CTX_LEN // S5_T
VMEM_LIMIT = 56 * 1024 * 1024

ZC_QA, ZC_KA, ZC_VA, ZC_CQ, ZC_CKV = 0, 384, 768, 1152, 1536
Z_MAIN = 1792
Z_TAIL = S5_WIDTH + LANES

LOG2E = math.log2(math.e)
NA_Q_SCALE = NA_HEAD_DIM ** -0.5 * LOG2E
MLA_Q_SCALE = MLA_QK ** -0.5 * LOG2E

f32 = jnp.float32
bf16 = jnp.bfloat16


def _dot(a, b):
    return jnp.dot(a, b, preferred_element_type=f32)


def _sigmoid(x):
    return 1.0 / (1.0 + jnp.exp(-x))


def _rms(x, n):
    return lax.rsqrt(jnp.sum(x * x, axis=-1, keepdims=True) / n + EPS)


def _modulated(x, g, shift, scale):
    y = x * _rms(x, D_MODEL)
    return (y * g) * (1.0 + scale) + shift


def _mod_kernel(c_ref, w_ref, b_ref, o_ref):
    c = c_ref[...]
    a = (c * _sigmoid(c)).astype(bf16)
    o_ref[0] = _dot(a, w_ref[0].astype(bf16)) + b_ref[0]


def _mod_call(cvec, w_mod, b_mod):
    rows = cvec.shape[0]
    return pl.pallas_call(
        _mod_kernel,
        grid=(DEPTH, N_MOD),
        in_specs=[
            pl.BlockSpec((rows, D_MODEL), lambda l, j: (0, 0)),
            pl.BlockSpec((1, D_MODEL, D_MODEL), lambda l, j: (l, 0, j)),
            pl.BlockSpec((1, 1, D_MODEL), lambda l, j: (l, 0, j)),
        ],
        out_specs=pl.BlockSpec((1, rows, D_MODEL), lambda l, j: (l, 0, j)),
        out_shape=jax.ShapeDtypeStruct((DEPTH, rows, N_MOD * D_MODEL), f32),
        compiler_params=pltpu.CompilerParams(
            dimension_semantics=("arbitrary", "arbitrary"), vmem_limit_bytes=VMEM_LIMIT),
        name="adaln_mod",
    )(cvec, w_mod, b_mod.reshape(DEPTH, 1, N_MOD * D_MODEL))


def _param_spec(arr, *lead, single=False):
    tail = arr.shape[len(lead):]
    idx = tuple(lead) + (0,) * len(tail)
    mode = pl.Buffered(1) if single else None
    return pl.BlockSpec((1,) * len(lead) + tail, lambda *_: idx, pipeline_mode=mode)


FFN_SUBTILES = 4


def _ffn_kernel(*refs, i0, fuse_outproj, split_input, tiles_per_batch, first):
    n_sub = FFN_SUBTILES
    refs = list(refs)
    n_src = 2 if split_input else 1
    srcs = [refs[g * n_src:(g + 1) * n_src] for g in range(n_sub)]
    refs = refs[n_sub * n_src:]
    mods = refs[:n_sub]
    g_ref, wgu_ref, wd_ref = refs[n_sub:n_sub + 3]
    refs = refs[n_sub + 3:]
    if fuse_outproj:
        mixers = [refs[3 * g:3 * g + 3] for g in range(n_sub)]
        wo_ref = refs[3 * n_sub]
        refs = refs[3 * n_sub + 1:]
    o_ref, x_scr, xm_scr = refs

    def mod_set(g):
        j = (pl.program_id(0) * n_sub + g) % tiles_per_batch + first
        return j, jnp.minimum(j, 1)

    def prologue(g):
        j, sel = mod_set(g)
        rows = slice(g * TM, (g + 1) * TM)
        mod_ref = mods[g]
        if split_input:
            x = jnp.where(j == 0, srcs[g][0][0], srcs[g][1][0])
        else:
            x = srcs[g][0][0]
        if fuse_outproj:
            a_ref, b_ref, s_ref = mixers[g]
            mixed = jnp.concatenate([a_ref[0], b_ref[0], s_ref[0]], axis=-1)
            x = x + mod_ref[0, sel, 5:6, :] * _dot(mixed, wo_ref[0])
        x_scr[rows, :] = x
        shift = mod_ref[0, sel, i0:i0 + 1, :]
        scale = mod_ref[0, sel, i0 + 1:i0 + 2, :]
        xm_scr[rows, :] = _modulated(x, g_ref[0, 0], shift, scale).astype(bf16)

    def chunk(g, c, acc):
        rows = slice(g * TM, (g + 1) * TM)
        lo, hi = FF_CHUNKS[c]
        gt = _dot(xm_scr[rows, :], wgu_ref[0, 0, :, lo:hi])
        up = _dot(xm_scr[rows, :], wgu_ref[0, 0, :, D_FF + lo:D_FF + hi])
        a = ((gt * _sigmoid(gt)) * up).astype(bf16)
        part = _dot(a, wd_ref[0, 0, lo:hi, :])
        return part if acc is None else acc + part

    def epilogue(g, acc):
        _, sel = mod_set(g)
        gate = mods[g][0, sel, i0 + 2:i0 + 3, :]
        o_ref[g] = x_scr[g * TM:(g + 1) * TM, :] + (0.5 * gate) * acc

    assert len(FF_CHUNKS) == 2
    prologue(0)
    acc_prev = chunk(0, 0, None)
    for g in range(1, n_sub):
        prologue(g)
        acc_prev = chunk(g - 1, 1, acc_prev)
        acc_cur = chunk(g, 0, None)
        epilogue(g - 1, acc_prev)
        acc_prev = acc_cur
    acc_prev = chunk(n_sub - 1, 1, acc_prev)
    epilogue(n_sub - 1, acc_prev)


def _ffn_call(h, mod, norm_g, wgu, wd, l, k, *, mixers=None, w_out=None, out_ctx=True):
    split = isinstance(h, tuple)
    first = 0 if out_ctx else 1
    tpb = N_TILES - first
    n_sub = FFN_SUBTILES

    def tile_of(i, g):
        t = i * n_sub + g
        return t // tpb, t % tpb

    ins, in_specs = [], []
    for g in range(n_sub):
        if split:
            ins += list(h)
            in_specs += [
                pl.BlockSpec((1, TM, D_MODEL), lambda i, g=g: (tile_of(i, g)[0], 0, 0)),
                pl.BlockSpec((1, TM, D_MODEL),
                             lambda i, g=g: (tile_of(i, g)[0], jnp.maximum(tile_of(i, g)[1] + first - 1, 0), 0))]
        else:
            ins.append(h)
            in_specs.append(pl.BlockSpec(
                (1, TM, D_MODEL), lambda i, g=g: (tile_of(i, g)[0], tile_of(i, g)[1] + first, 0)))
    for g in range(n_sub):
        ins.append(mod)
        in_specs.append(pl.BlockSpec((1, 2, N_MOD, D_MODEL), lambda i, g=g: (tile_of(i, g)[0], 0, 0, 0)))
    ins += [norm_g, wgu, wd]
    in_specs += [_param_spec(norm_g, l, 2 * k), _param_spec(wgu, l, k, single=True),
                 _param_spec(wd, l, k, single=True)]
    if mixers is not None:
        for g in range(n_sub):
            for arr in mixers:
                ins.append(arr)
                in_specs.append(pl.BlockSpec((1, TM, arr.shape[-1]), lambda i, g=g: tile_of(i, g) + (0,)))
        ins.append(w_out)
        in_specs.append(_param_spec(w_out, l, single=True))
    rows = n_sub * TM
    out = pl.pallas_call(
        functools.partial(_ffn_kernel, i0=6 * k, fuse_outproj=mixers is not None, split_input=split,
                          tiles_per_batch=tpb, first=first),
        grid=(BATCH * tpb // n_sub,),
        in_specs=in_specs,
        out_specs=pl.BlockSpec((n_sub, TM, D_MODEL), lambda i: (i, 0, 0)),
        out_shape=jax.ShapeDtypeStruct((BATCH * tpb, TM, D_MODEL), f32),
        scratch_shapes=[pltpu.VMEM((rows, D_MODEL), f32), pltpu.VMEM((rows, D_MODEL), bf16)],
        compiler_params=pltpu.CompilerParams(
            dimension_semantics=("arbitrary",), vmem_limit_bytes=VMEM_LIMIT),
        name="swiglu_half_step",
    )(*ins)
    return out.reshape(BATCH, tpb * TM, D_MODEL)


INPROJ_SUBTILES = 3
V_HEAD = 64
VT_HEAD_ROWS = V_HEAD + 16
VT_ROWS = 2 * VT_HEAD_ROWS
VT_ALL = 3 * VT_ROWS


def _store_values_t(vt_ref, p, cols, v_pair):
    vt = v_pair.T.astype(bf16)
    ones = jnp.ones((VT_HEAD_ROWS - V_HEAD, vt.shape[1]), bf16)
    for hh in range(2):
        r0 = p * VT_ROWS + hh * VT_HEAD_ROWS
        vt_ref[0, r0:r0 + V_HEAD, cols] = vt[hh * V_HEAD:(hh + 1) * V_HEAD, :]
        vt_ref[0, r0 + V_HEAD:r0 + VT_HEAD_ROWS, cols] = ones


def _pair_head_norm(x, g):
    lane = lax.broadcasted_iota(jnp.int32, (1, LANES), 1)
    lo = lane < NA_HEAD_DIM
    outs = []
    for p in range(NA_WIDTH // LANES):
        xb = x[:, p * LANES:(p + 1) * LANES]
        sq = xb * xb
        s_lo = jnp.sum(jnp.where(lo, sq, 0.0), axis=-1, keepdims=True)
        s_hi = jnp.sum(jnp.where(lo, 0.0, sq), axis=-1, keepdims=True)
        r = jnp.where(lo, lax.rsqrt(s_lo / NA_HEAD_DIM + EPS), lax.rsqrt(s_hi / NA_HEAD_DIM + EPS))
        outs.append((xb * r) * g[:, p * LANES:(p + 1) * LANES])
    return outs


def _inproj_kernel(h_ref, mod_ref, g_ref, wmain_ref, wtail_ref, naq_g_ref, nak_g_ref, cq_g_ref, ckv_g_ref,
                   wq_ref, wkv_ref, gq_ref, gqp_ref, gk_ref, gkp_ref, cos_ref, sin_ref,
                   qn_ref, kn_ref, vn_ref, qm_ref, km_ref, vm_ref, u_ref, z_scr, zt_scr):
    def project(g):
        rows = slice(g * TM, (g + 1) * TM)
        sel = jnp.minimum(pl.program_id(1) * INPROJ_SUBTILES + g, 1)
        xm = _modulated(h_ref[0, rows, :], g_ref[0, 0], mod_ref[0, sel, 3:4, :],
                        mod_ref[0, sel, 4:5, :]).astype(bf16)
        z_scr[rows, :] = _dot(xm, wmain_ref[0])
        zt_scr[rows, :] = _dot(xm, wtail_ref[0])

    def heads(g):
        rows = slice(g * TM, (g + 1) * TM)
        _inproj_heads(rows, z_scr, zt_scr, naq_g_ref, nak_g_ref, cq_g_ref, ckv_g_ref, wq_ref, wkv_ref,
                      gq_ref, gqp_ref, gk_ref, gkp_ref, cos_ref, sin_ref,
                      qn_ref, kn_ref, vn_ref, qm_ref, km_ref, vm_ref, u_ref)

    project(0)
    for g in range(INPROJ_SUBTILES):
        if g + 1 < INPROJ_SUBTILES:
            project(g + 1)
        heads(g)


def _inproj_heads(rows, z_scr, zt_scr, naq_g_ref, nak_g_ref, cq_g_ref, ckv_g_ref, wq_ref, wkv_ref,
                  gq_ref, gqp_ref, gk_ref, gkp_ref, cos_ref, sin_ref,
                  qn_ref, kn_ref, vn_ref, qm_ref, km_ref, vm_ref, u_ref):
    z = z_scr[rows, :]
    zt = zt_scr[rows, :]

    qn = _pair_head_norm(z[:, ZC_QA:ZC_QA + NA_WIDTH], naq_g_ref[0])
    kn = _pair_head_norm(z[:, ZC_KA:ZC_KA + NA_WIDTH], nak_g_ref[0])
    for p in range(NA_WIDTH // LANES):
        qn_ref[0, p * LANES:(p + 1) * LANES, rows] = (qn[p] * NA_Q_SCALE).T.astype(bf16)
        kn_ref[0, rows, p * LANES:(p + 1) * LANES] = kn[p].astype(bf16)
        _store_values_t(vn_ref, p, rows, z[:, ZC_VA + p * LANES:ZC_VA + (p + 1) * LANES])
    u_ref[0, rows, :] = zt[:, 0:S5_WIDTH]

    cos_t = cos_ref[rows, :]
    sin_t = sin_ref[rows, :]

    cq = z[:, ZC_CQ:ZC_CQ + Q_LORA]
    ncq = ((cq * _rms(cq, Q_LORA)) * cq_g_ref[0]).astype(bf16)
    qq = _dot(ncq, wq_ref[0])
    gq = gq_ref[0]
    gqp = gqp_ref[0]
    for hd in range(MLA_HEADS):
        pre = qq[:, hd * HEAD_PAD:(hd + 1) * HEAD_PAD]
        perm = qq[:, (MLA_HEADS + hd) * HEAD_PAD:(MLA_HEADS + hd + 1) * HEAD_PAD]
        r = _rms(pre, MLA_QK)
        qm_ref[0, hd * HEAD_PAD:(hd + 1) * HEAD_PAD, rows] = (
            ((pre * gq) * cos_t + (perm * gqp) * sin_t) * (r * MLA_Q_SCALE)).T.astype(bf16)

    ckv = z[:, ZC_CKV:ZC_CKV + KV_LORA]
    nkv = ((ckv * _rms(ckv, KV_LORA)) * ckv_g_ref[0]).astype(bf16)
    kv = _dot(nkv, wkv_ref[0])
    for p in range(MLA_WIDTH // LANES):
        lo_col = MLA_HEADS * HEAD_PAD + p * LANES
        _store_values_t(vm_ref, p, rows, kv[:, lo_col:lo_col + LANES])
    krb = zt[:, S5_WIDTH:S5_WIDTH + LANES]
    lane = lax.broadcasted_iota(jnp.int32, (1, LANES), 1)
    rope_lanes = (lane >= MLA_NOPE) & (lane < MLA_QK)
    kr_a = jnp.where(rope_lanes, pltpu.roll(krb, MLA_NOPE, axis=1), 0.0)
    kr_b = jnp.where(rope_lanes, pltpu.roll(krb, MLA_ROPE, axis=1), 0.0)
    gk = gk_ref[0]
    gkp = gkp_ref[0]
    rot_part = (kr_b * gkp) * sin_t
    for hd in range(MLA_HEADS):
        kfull = kv[:, hd * HEAD_PAD:(hd + 1) * HEAD_PAD] + kr_a
        r = _rms(kfull, MLA_QK)
        km_ref[0, rows, hd * HEAD_PAD:(hd + 1) * HEAD_PAD] = (
            ((kfull * gk) * cos_t + rot_part) * r).astype(bf16)


def _inproj_call(h, mod, norm_g, l, params, cos_t, sin_t):
    qk_w = MLA_HEADS * HEAD_PAD
    rows = INPROJ_SUBTILES * TM

    def tok(width):
        return pl.BlockSpec((1, rows, width), lambda b, j: (b, j, 0))

    def tok_t(width):
        return pl.BlockSpec((1, width, rows), lambda b, j: (b, 0, j))

    tab_spec = pl.BlockSpec((rows, LANES), lambda b, j: (j, 0))
    mod_spec = pl.BlockSpec((1, 2, N_MOD, D_MODEL), lambda b, j: (b, 0, 0, 0))
    main_spec = pl.BlockSpec((1, D_MODEL, Z_MAIN), lambda b, j: (l, 0, 0))
    names = ["win", "win_tail", "naq_g", "nak_g", "cq_g", "ckv_g", "wq", "wkv", "gq", "gqp", "gk", "gkp"]
    return pl.pallas_call(
        _inproj_kernel,
        grid=(BATCH, N_TILES // INPROJ_SUBTILES),
        in_specs=([tok(D_MODEL), mod_spec, _param_spec(norm_g, l, 1), main_spec]
                  + [_param_spec(params[n], l) for n in names[1:]] + [tab_spec, tab_spec]),
        out_specs=[tok_t(NA_WIDTH), tok(NA_WIDTH), tok_t(VT_ALL),
                   tok_t(qk_w), tok(qk_w), tok_t(VT_ALL), tok(S5_WIDTH)],
        out_shape=[
            jax.ShapeDtypeStruct((BATCH, NA_WIDTH, S_ALL), bf16),
            jax.ShapeDtypeStruct((BATCH, S_ALL, NA_WIDTH), bf16),
            jax.ShapeDtypeStruct((BATCH, VT_ALL, S_ALL), bf16),
            jax.ShapeDtypeStruct((BATCH, qk_w, S_ALL), bf16),
            jax.ShapeDtypeStruct((BATCH, S_ALL, qk_w), bf16),
            jax.ShapeDtypeStruct((BATCH, VT_ALL, S_ALL), bf16),
            jax.ShapeDtypeStruct((BATCH, S_ALL, S5_WIDTH), f32),
        ],
        scratch_shapes=[pltpu.VMEM((rows, Z_MAIN), f32), pltpu.VMEM((rows, Z_TAIL), f32)],
        compiler_params=pltpu.CompilerParams(
            dimension_semantics=("arbitrary", "arbitrary"), vmem_limit_bytes=VMEM_LIMIT),
        name="in_proj_heads",
    )(h, mod, norm_g, *[params[n] for n in names], cos_t, sin_t)


def _pair_block_diag(qt):
    d = qt.shape[0] // 2
    z = jnp.zeros((d, qt.shape[1]), qt.dtype)
    return jnp.concatenate([jnp.concatenate([qt[:d], z], axis=0),
                            jnp.concatenate([z, qt[d:]], axis=0)], axis=1)


def _pair_scores(dst_ref, key_blocks, q_bd, bias_blocks):
    r = 0
    m = None
    for kk, bias in zip(key_blocks, bias_blocks):
        s = _dot(kk, q_bd)
        s = s if bias is None else s + bias
        dst_ref[r:r + kk.shape[0], :] = s
        mi = jnp.max(s, axis=0, keepdims=True)
        m = mi if m is None else jnp.maximum(m, mi)
        r += kk.shape[0]
    return m


def _pair_softmax_pv(src_ref, vt_blocks, m_all):
    n_keys = sum(vt.shape[1] for vt in vt_blocks)
    outs = []
    for hh in range(2):
        cols = slice(hh * TM, (hh + 1) * TM)
        head = slice(hh * VT_HEAD_ROWS, (hh + 1) * VT_HEAD_ROWS)
        m = m_all[:, cols]
        acc = None
        r = 0
        for vt in vt_blocks:
            p = jnp.exp2(src_ref[r:r + vt.shape[1], cols] - m)
            oi = _dot(vt[head, :], p.astype(bf16))
            acc = oi if acc is None else acc + oi
            r += vt.shape[1]
        outs.append(acc[:V_HEAD, :] / acc[V_HEAD:V_HEAD + 1, :])
    return jnp.concatenate(outs, axis=0).T.astype(bf16)


def _pipelined_tiles(scores, finish, s_a, s_b, first_tile, last_tile, lead_in=None):
    m_a = scores(first_tile, s_a)
    if lead_in is not None:
        lead_in(s_b)

    def body(i, m_a):
        t0 = first_tile + 2 * i
        m_b = scores(t0 + 1, s_b)
        finish(t0, s_a, m_a)
        m_next = scores(t0 + 2, s_a)
        finish(t0 + 1, s_b, m_b)
        return m_next

    m_a = lax.fori_loop(0, (last_tile - first_tile + 1) // 2 - 1, body, m_a)
    m_b = scores(last_tile, s_b)
    finish(last_tile - 1, s_a, m_a)
    finish(last_tile, s_b, m_b)


NA_TILE_ROWS = TM // GRID_W
NA_UNION_ROWS = 12
NA_UNION_KEYS = NA_UNION_ROWS * GRID_W
NA_PATTERNS = 3
NA_BIAS_SHIFTS = 2 * WIN_ROWS


def _na_window_lo(pat, rr):
    return (-rr, -(WIN_ROWS // 2), -(WIN_ROWS // 2) - rr)[pat]


def _na_fill_bias(blk_ref, bias_scr):
    lane = lax.broadcasted_iota(jnp.int32, (1, LANES), 1)
    left = lane < GRID_W
    neg = jnp.full((GRID_W, LANES), NEG_INF, f32)
    for pat in range(NA_PATTERNS):
        for ii in range(NA_UNION_ROWS):
            for t in range(NA_TILE_ROWS // 2):
                d = ii - NA_TILE_ROWS * pat - 2 * t
                ok_l = 0 <= d - _na_window_lo(pat, 2 * t) < WIN_ROWS
                ok_r = 0 <= (d - 1) - _na_window_lo(pat, 2 * t + 1) < WIN_ROWS
                for hh in range(2):
                    if ok_l or ok_r:
                        blk = blk_ref[0, hh, d + WIN_ROWS - 1]
                        if not ok_l:
                            blk = jnp.where(left, NEG_INF, blk)
                        if not ok_r:
                            blk = jnp.where(left, blk, NEG_INF)
                    else:
                        blk = neg
                    c0 = hh * TM + t * LANES
                    bias_scr[pat, ii * GRID_W:(ii + 1) * GRID_W, c0:c0 + LANES] = blk


def _query_tile(qt_ref, t):
    return _pair_block_diag(qt_ref[0, :, pl.ds(pl.multiple_of(t * TM, TM), TM)])


def _out_rows(t, with_ctx):
    return pl.ds(pl.multiple_of((t - (0 if with_ctx else 1)) * TM, TM), TM)


def _na_kernel(qt_ref, k_ref, vt_ref, blk_ref, o_ref, bias_scr, s_a, s_b, *, with_ctx):
    @pl.when(pl.program_id(1) == 0)
    def _():
        _na_fill_bias(blk_ref, bias_scr)

    kc = k_ref[0, 0:CTX_LEN, :]
    vtc = vt_ref[0, :, 0:CTX_LEN]

    def window(t):
        r0 = (t - 1) * NA_TILE_ROWS
        base = jnp.clip(r0 - WIN_ROWS // 2, 0, GRID_ROWS - NA_UNION_ROWS)
        pat = (r0 - base) // NA_TILE_ROWS
        start = pl.multiple_of(CTX_LEN + base * GRID_W, NA_TILE_ROWS * GRID_W)
        return pat, pl.ds(start, NA_UNION_KEYS)

    def scores(t, dst):
        pat, keys = window(t)
        return _pair_scores(dst, [k_ref[0, keys, :], kc], _query_tile(qt_ref, t), [bias_scr[pat], None])

    def finish(t, src, m):
        _, keys = window(t)
        o_ref[0, _out_rows(t, with_ctx), :] = _pair_softmax_pv(src, [vt_ref[0, :, keys], vtc], m)

    m_ctx = _pair_scores(s_b, [kc], _query_tile(qt_ref, 0), [None]) if with_ctx else None

    def ctx_finish(buf):
        o_ref[0, 0:TM, :] = _pair_softmax_pv(buf, [vtc], m_ctx)

    _pipelined_tiles(scores, finish, s_a, s_b, 1, N_TILES - 1, ctx_finish if with_ctx else None)


def _na_call(qnt, kn, vnt, bias_blocks, l, with_ctx):
    n_pairs = NA_WIDTH // LANES
    s_out = S_ALL if with_ctx else SEQ
    logits = pltpu.VMEM((NA_UNION_KEYS + CTX_LEN, 2 * TM), f32)
    return pl.pallas_call(
        functools.partial(_na_kernel, with_ctx=with_ctx),
        grid=(n_pairs, BATCH),
        in_specs=[
            pl.BlockSpec((1, LANES, S_ALL), lambda p, b: (b, p, 0)),
            pl.BlockSpec((1, S_ALL, LANES), lambda p, b: (b, 0, p)),
            pl.BlockSpec((1, VT_ROWS, S_ALL), lambda p, b: (b, p, 0)),
            pl.BlockSpec((1, 2, NA_BIAS_SHIFTS, GRID_W, LANES), lambda p, b: (l, p, 0, 0, 0)),
        ],
        out_specs=pl.BlockSpec((1, s_out, LANES), lambda p, b: (b, 0, p)),
        out_shape=jax.ShapeDtypeStruct((BATCH, s_out, NA_WIDTH), bf16),
        scratch_shapes=[pltpu.VMEM((NA_PATTERNS, NA_UNION_KEYS, 2 * TM), f32), logits, logits],
        compiler_params=pltpu.CompilerParams(
            dimension_semantics=("arbitrary", "arbitrary"), vmem_limit_bytes=VMEM_LIMIT),
        name="na_attention",
    )(qnt, kn, vnt, bias_blocks)


MLA_KEY_BLOCK = 768


def _mla_kernel(qt_ref, k_ref, vt_ref, o_ref, s_a, s_b, *, with_ctx):
    def scores(t, dst):
        return _pair_scores(dst, [k_ref[0]], _query_tile(qt_ref, t), [None])

    def finish(t, src, m):
        vts = [vt_ref[0, :, s0:s0 + MLA_KEY_BLOCK] for s0 in range(0, S_ALL, MLA_KEY_BLOCK)]
        o_ref[0, _out_rows(t, with_ctx), :] = _pair_softmax_pv(src, vts, m)

    m_ctx = (_pair_scores(s_b, [k_ref[0, 0:CTX_LEN, :]], _query_tile(qt_ref, 0), [None])
             if with_ctx else None)

    def ctx_finish(buf):
        o_ref[0, 0:TM, :] = _pair_softmax_pv(buf, [vt_ref[0, :, 0:CTX_LEN]], m_ctx)

    _pipelined_tiles(scores, finish, s_a, s_b, 1, N_TILES - 1, ctx_finish if with_ctx else None)


def _mla_call(qmt, km, vmt, with_ctx):
    n_pairs = MLA_HEADS // 2
    s_out = S_ALL if with_ctx else SEQ
    logits = pltpu.VMEM((S_ALL, 2 * TM), f32)
    return pl.pallas_call(
        functools.partial(_mla_kernel, with_ctx=with_ctx),
        grid=(BATCH, n_pairs),
        in_specs=[
            pl.BlockSpec((1, 2 * HEAD_PAD, S_ALL), lambda b, p: (b, p, 0)),
            pl.BlockSpec((1, S_ALL, 2 * HEAD_PAD), lambda b, p: (b, 0, p)),
            pl.BlockSpec((1, VT_ROWS, S_ALL), lambda b, p: (b, p, 0)),
        ],
        out_specs=pl.BlockSpec((1, s_out, LANES), lambda b, p: (b, 0, p)),
        out_shape=jax.ShapeDtypeStruct((BATCH, s_out, MLA_WIDTH), bf16),
        scratch_shapes=[logits, logits],
        compiler_params=pltpu.CompilerParams(
            dimension_semantics=("arbitrary", "arbitrary"), vmem_limit_bytes=VMEM_LIMIT),
        name="mla_attention",
    )(qmt, km, vmt)


S5_BLOCK_T = 16
S5_BLOCK_ROWS = S5_BLOCK_T * BATCH


def _s5_scan_kernel(*refs, reverse, finish):
    u_ref, bmat_ref, cmat_ref, lam_ref = refs[:4]
    if finish:
        y_other_ref, d_ref, wglu_ref, bglu_ref = refs[4:8]
    y_ref, bu_ref, h_ref, st_ref = refs[-4:]

    @pl.when(pl.program_id(0) == 0)
    def _():
        st_ref[...] = jnp.zeros_like(st_ref)

    def rows(k):
        return slice(k * S5_BLOCK_ROWS, (k + 1) * S5_BLOCK_ROWS)

    def project(k):
        bu_ref[rows(k), :] = _dot(u_ref[rows(k), :].astype(bf16), bmat_ref[0, 0])

    def readout(k):
        y = _dot(h_ref[rows(k), :].astype(bf16), cmat_ref[0, 0])
        if not finish:
            y_ref[rows(k), :] = y
            return
        y = (y_other_ref[rows(k), :] + y) + d_ref[0] * u_ref[rows(k), :]
        c0 = math.sqrt(2.0 / math.pi)
        gl = 0.5 * y * (1.0 + jnp.tanh(c0 * (y + 0.044715 * (y * y * y))))
        o = _dot(gl.astype(bf16), wglu_ref[0]) + bglu_ref[0]
        y_ref[rows(k), :] = (o[:, :S5_WIDTH] * _sigmoid(o[:, S5_WIDTH:])).astype(bf16)

    n_blocks = S5_T // S5_BLOCK_T
    order = list(range(n_blocks))[::-1] if reverse else list(range(n_blocks))
    steps = list(range(S5_BLOCK_T))[::-1] if reverse else list(range(S5_BLOCK_T))
    hr = st_ref[:, 0:S5_LANES]
    hi = st_ref[:, S5_LANES:]
    project(order[0])
    for n, k in enumerate(order):
        if n + 1 < n_blocks:
            project(order[n + 1])
        for t in steps:
            r0 = k * S5_BLOCK_ROWS + t * BATCH
            lr = lam_ref[0, 0, :, 0:S5_LANES]
            li = lam_ref[0, 0, :, S5_LANES:]
            nr = (lr * hr - li * hi) + bu_ref[r0:r0 + BATCH, 0:S5_LANES]
            ni = (lr * hi + li * hr) + bu_ref[r0:r0 + BATCH, S5_LANES:]
            h_ref[r0:r0 + BATCH, 0:S5_LANES] = nr
            h_ref[r0:r0 + BATCH, S5_LANES:] = ni
            hr, hi = nr, ni
        if n >= 1:
            readout(order[n - 1])
    readout(order[-1])
    st_ref[:, 0:S5_LANES] = hr
    st_ref[:, S5_LANES:] = hi


def _s5_scan_call(u_t, p, l, d, y_other=None):
    reverse = d == 1
    finish = y_other is not None
    if reverse:
        def chunk(i):
            return jnp.where(i < S5_CTX_CHUNKS, S5_CTX_CHUNKS - 1 - i, S5_CHUNKS + S5_CTX_CHUNKS - 1 - i)
    else:
        def chunk(i):
            return i
    row_spec = pl.BlockSpec((S5_ROWS, S5_WIDTH), lambda i: (chunk(i), 0))
    ins = [u_t, p["bmat"], p["cmat"], p["lam"]]
    in_specs = [row_spec, _param_spec(p["bmat"], l, d), _param_spec(p["cmat"], l, d), _param_spec(p["lam"], l, d)]
    if finish:
        ins += [y_other, p["s5_d"], p["w_glu"], p["b_glu"]]
        in_specs += [row_spec, _param_spec(p["s5_d"], l), _param_spec(p["w_glu"], l), _param_spec(p["b_glu"], l)]
    return pl.pallas_call(
        functools.partial(_s5_scan_kernel, reverse=reverse, finish=finish),
        grid=(S5_CHUNKS,),
        in_specs=in_specs,
        out_specs=row_spec,
        out_shape=jax.ShapeDtypeStruct((S_ALL * BATCH, S5_WIDTH), bf16 if finish else f32),
        scratch_shapes=[pltpu.VMEM((S5_ROWS, 2 * S5_LANES), f32), pltpu.VMEM((S5_ROWS, 2 * S5_LANES), f32),
                        pltpu.VMEM((BATCH, 2 * S5_LANES), f32)],
        compiler_params=pltpu.CompilerParams(
            dimension_semantics=("arbitrary",), vmem_limit_bytes=VMEM_LIMIT),
        name="s5_scan_bwd" if reverse else "s5_scan_fwd",
    )(*ins)


def _rope_partner(x, signed=True):
    half = MLA_ROPE // 2
    quarter = half // 2
    rows = []
    for k in range(MLA_ROPE):
        row = [0.0] * MLA_ROPE
        if (k % half) < quarter:
            row[k + quarter] = 1.0
        else:
            row[k - quarter] = -1.0 if signed else 1.0
        rows.append(row)
    return jnp.einsum('...k,kj->...j', x, jnp.array(rows, f32), precision=lax.Precision.HIGHEST)


def _rope_tables():
    quarter = MLA_ROPE // 4
    t = jnp.arange(SEQ)
    row = (t // GRID_W).astype(f32)
    col = (t % GRID_W).astype(f32)
    inv = ROPE_THETA ** (-jnp.arange(quarter, dtype=f32) / quarter)
    ang = jnp.concatenate([row[:, None] * inv] * 2 + [col[:, None] * inv] * 2, axis=-1)
    ang = jnp.concatenate([jnp.zeros((CTX_LEN, MLA_ROPE), f32), ang], axis=0)
    ones = jnp.ones((S_ALL, MLA_NOPE), f32)
    pad = jnp.zeros((S_ALL, HEAD_PAD - MLA_QK), f32)
    cos_t = jnp.concatenate([ones, jnp.cos(ang), pad], axis=-1)
    sin_t = jnp.concatenate([0.0 * ones, jnp.sin(ang), pad], axis=-1)
    return cos_t, sin_t


def _na_bias_blocks(rpb):
    cq = jnp.arange(GRID_W)[None, :]
    kc = jnp.arange(GRID_W)[:, None]
    d_col = jnp.clip(kc - cq, -(WIN_COLS - 1), WIN_COLS - 1) + WIN_COLS - 1
    col_start = jnp.clip(cq - WIN_COLS // 2, 0, GRID_W - WIN_COLS)
    in_win = (kc >= col_start) & (kc < col_start + WIN_COLS)
    onehot = (d_col[:, :, None] == jnp.arange(2 * WIN_COLS - 1)[None, None, :]).astype(f32)
    blocks = jnp.einsum('lhdc,kqc->lhdkq', rpb.astype(f32), onehot, precision=lax.Precision.HIGHEST)
    blocks = jnp.where(in_win, blocks * LOG2E, NEG_INF)
    neg = jnp.full(blocks.shape[:2] + (1, GRID_W, GRID_W), NEG_INF, f32)
    padded = jnp.concatenate([neg, blocks, neg], axis=2)
    return jnp.concatenate([padded[:, :, 1:], padded[:, :, :-1]], axis=-1)


def _pad_heads(w, width):
    lead = w.shape[:-1]
    w = w.reshape(lead + (MLA_HEADS, width))
    w = jnp.pad(w, [(0, 0)] * len(lead) + [(0, 0), (0, HEAD_PAD - width)])
    return w.reshape(lead + (MLA_HEADS * HEAD_PAD,))


def _block_diag_groups(w):
    lead = w.shape[:-3]
    a, b = w.shape[-2:]
    shape = (S5_GROUPS * a, S5_GROUPS * b)
    repeat = jnp.tile(jnp.eye(b, dtype=f32), (1, S5_GROUPS))
    tiled = jnp.einsum('...rb,bc->...rc', w.reshape(lead + (S5_GROUPS * a, b)).astype(f32), repeat,
                       precision=lax.Precision.HIGHEST)
    same_group = (lax.broadcasted_iota(jnp.int32, shape, 0) // a
                  == lax.broadcasted_iota(jnp.int32, shape, 1) // b)
    return jnp.where(same_group, tiled, 0.0)


def _prepare_params(ffn_w_gu, ffn_w_down, w_in, w_out, na_qk_g, na_rpb, mla_cq_g, mla_ckv_g, mla_w_uq,
                    mla_w_ukv, mla_qk_g, s5_lam_re, s5_lam_im, s5_log_dt, s5_b_re, s5_b_im, s5_c_re,
                    s5_c_im, s5_d, s5_w_glu, s5_b_glu):
    p = {}
    p["wgu"] = ffn_w_gu.astype(bf16)
    p["wd"] = ffn_w_down.astype(bf16)
    p["w_out"] = w_out.astype(bf16)
    o_kr = Z_MAIN
    o_u = o_kr + MLA_ROPE
    w_kr = w_in[:, :, o_kr:o_kr + MLA_ROPE]
    p["win"] = w_in.astype(bf16)
    p["win_tail"] = jnp.concatenate([
        w_in[:, :, o_u:o_u + S5_WIDTH], w_kr, _rope_partner(w_kr),
        jnp.zeros((DEPTH, D_MODEL, LANES - 2 * MLA_ROPE), f32)], axis=-1).astype(bf16)
    p["naq_g"] = jnp.tile(na_qk_g[:, 0], (1, NA_HEADS))[:, None, :]
    p["nak_g"] = jnp.tile(na_qk_g[:, 1], (1, NA_HEADS))[:, None, :]
    p["cq_g"] = mla_cq_g[:, None, :]
    p["ckv_g"] = mla_ckv_g[:, None, :]
    wuq = mla_w_uq.reshape(DEPTH, Q_LORA, MLA_HEADS, MLA_QK)
    partner = jnp.concatenate([jnp.zeros((DEPTH, Q_LORA, MLA_HEADS, MLA_NOPE), f32),
                               _rope_partner(wuq[..., MLA_NOPE:])], axis=-1)
    p["wq"] = jnp.concatenate([_pad_heads(mla_w_uq, MLA_QK),
                               _pad_heads(partner.reshape(DEPTH, Q_LORA, -1), MLA_QK)], axis=-1).astype(bf16)
    wukv = mla_w_ukv.reshape(DEPTH, KV_LORA, MLA_HEADS, MLA_NOPE + MLA_V_DIM)
    p["wkv"] = jnp.concatenate([_pad_heads(wukv[..., :MLA_NOPE].reshape(DEPTH, KV_LORA, -1), MLA_NOPE),
                                wukv[..., MLA_NOPE:].reshape(DEPTH, KV_LORA, -1)], axis=-1).astype(bf16)

    def pad_gain(g):
        z = jnp.zeros((DEPTH, HEAD_PAD - MLA_QK), f32)
        full = jnp.concatenate([g, z], axis=-1)[:, None, :]
        part = jnp.concatenate([jnp.zeros((DEPTH, MLA_NOPE), f32), _rope_partner(g[:, MLA_NOPE:], signed=False), z],
                               axis=-1)[:, None, :]
        return full, part

    p["gq"], p["gqp"] = pad_gain(mla_qk_g[:, 0])
    p["gk"], p["gkp"] = pad_gain(mla_qk_g[:, 1])
    p["na_bias"] = _na_bias_blocks(na_rpb)
    lre = s5_lam_re.astype(f32)
    lim = s5_lam_im.astype(f32)
    dt = jnp.exp(s5_log_dt.astype(f32))[..., None]
    mag = jnp.exp(lre * dt)
    bar_re = mag * jnp.cos(lim * dt)
    bar_im = mag * jnp.sin(lim * dt)
    den = lre * lre + lim * lim
    q_re = ((bar_re - 1.0) * lre + bar_im * lim) / den
    q_im = (bar_im * lre - (bar_re - 1.0) * lim) / den
    bre = s5_b_re.astype(f32)
    bim = s5_b_im.astype(f32)
    bbar_re = q_re[..., None] * bre - q_im[..., None] * bim
    bbar_im = q_re[..., None] * bim + q_im[..., None] * bre
    p["bmat"] = jnp.concatenate([_block_diag_groups(jnp.swapaxes(bbar_re, -1, -2)),
                                 _block_diag_groups(jnp.swapaxes(bbar_im, -1, -2))], axis=-1).astype(bf16)
    p["cmat"] = jnp.concatenate([_block_diag_groups(jnp.swapaxes(s5_c_re.astype(f32), -1, -2)),
                                 -_block_diag_groups(jnp.swapaxes(s5_c_im.astype(f32), -1, -2))],
                                axis=-2).astype(bf16)
    lam_row = jnp.concatenate([bar_re.reshape(DEPTH, 2, S5_LANES), bar_im.reshape(DEPTH, 2, S5_LANES)], axis=-1)
    p["lam"] = jnp.broadcast_to(lam_row[:, :, None, :], (DEPTH, 2, BATCH, 2 * S5_LANES))
    p["s5_d"] = s5_d[:, None, :]
    p["w_glu"] = s5_w_glu.astype(bf16)
    p["b_glu"] = s5_b_glu[:, None, :]
    return p


def kernel(x, c, ctx, c_ctx, w_mod, b_mod, norm_g, ffn_w_gu, ffn_w_down, w_in, w_out, na_qk_g, na_rpb, mla_cq_g, mla_ckv_g, mla_w_uq, mla_w_ukv, mla_qk_g, s5_lam_re, s5_lam_im, s5_log_dt, s5_b_re, s5_b_im, s5_c_re, s5_c_im, s5_d, s5_w_glu, s5_b_glu):
    assert x.shape == (BATCH, SEQ, D_MODEL) and ctx.shape == (BATCH, CTX_LEN, D_MODEL)
    mod_rows = 16
    cvec = jnp.concatenate([c, c_ctx[None, :], jnp.zeros((mod_rows - BATCH - 1, D_MODEL), f32)], axis=0)
    mod_all = _mod_call(cvec, w_mod, b_mod).reshape(DEPTH, mod_rows, N_MOD, D_MODEL)
    cos_t, sin_t = _rope_tables()
    p = _prepare_params(ffn_w_gu, ffn_w_down, w_in, w_out, na_qk_g, na_rpb, mla_cq_g, mla_ckv_g,
                        mla_w_uq, mla_w_ukv, mla_qk_g, s5_lam_re, s5_lam_im, s5_log_dt, s5_b_re,
                        s5_b_im, s5_c_re, s5_c_im, s5_d, s5_w_glu, s5_b_glu)
    norm_g4 = norm_g[:, :, None, :]
    h = (ctx, x)
    for l in range(DEPTH):
        mod_c = jnp.broadcast_to(mod_all[l, BATCH][None], (BATCH, N_MOD, D_MODEL))
        mod = jnp.stack([mod_c, mod_all[l, :BATCH]], axis=1)
        need_ctx = l < DEPTH - 1
        h = _ffn_call(h, mod, norm_g4, p["wgu"], p["wd"], l, 0)
        qn, kn, vn, qm, km, vm, u = _inproj_call(h, mod, norm_g4, l, p, cos_t, sin_t)
        a = _na_call(qn, kn, vn, p["na_bias"], l, need_ctx)
        bm = _mla_call(qm, km, vm, need_ctx)
        u_t = u.transpose(1, 0, 2).reshape(S_ALL * BATCH, S5_WIDTH)
        yf = _s5_scan_call(u_t, p, l, 0)
        s_t = _s5_scan_call(u_t, p, l, 1, y_other=yf)
        if not need_ctx:
            s_t = s_t[CTX_LEN * BATCH:]
        s = s_t.reshape(-1, BATCH, S5_WIDTH).transpose(1, 0, 2)
        h = _ffn_call(h, mod, norm_g4, p["wgu"], p["wd"], l, 1, mixers=(a, bm, s), w_out=p["w_out"],
                      out_ctx=need_ctx)
    return h
```
